```python
import jax, jax.numpy as jnp
from jax import lax
import numpy as np

D_MODEL = 2048
BATCH = 4
SEQ = 2048
DEPTH = 2

N_A_LAYERS = DEPTH // 2
N_B_LAYERS = DEPTH - N_A_LAYERS
PLE_DIM = 256
D_FF = 4 * D_MODEL
HGRN_HEAD_DIM = 128
HGRN_HEADS = D_MODEL // HGRN_HEAD_DIM
HGRN_CHUNK = 16
FOX_HEAD_DIM = 128
FOX_HEADS = D_MODEL // FOX_HEAD_DIM
Q_BLOCK = 128
EPS = 1e-6

kernel_name = "yoco_hgrn2_fox_hybrid"


def rms_norm(x, gain):
    xf = x.astype(jnp.float32)
    y = xf * lax.rsqrt(jnp.mean(xf * xf, axis=-1, keepdims=True) + EPS)
    return (y * gain.astype(jnp.float32)).astype(x.dtype)


def hgrn2_mixer(u, w_in, lb, head_gain, w_out):
    bsz, seq, _ = u.shape
    nc = seq // HGRN_CHUNK
    q, f, i, g = jnp.split(u @ w_in, 4, axis=-1)

    def to_chunks(t):
        t = t.reshape(bsz, nc, HGRN_CHUNK, HGRN_HEADS, HGRN_HEAD_DIM)
        return t.transpose(0, 3, 1, 2, 4).astype(jnp.float32)

    lb_h = lb.astype(jnp.float32).reshape(HGRN_HEADS, 1, 1, HGRN_HEAD_DIM)
    fg = lb_h + (1.0 - lb_h) * jax.nn.sigmoid(to_chunks(f))
    k = 1.0 - fg
    b = jnp.cumsum(jnp.log(fg), axis=3)
    b_last = b[:, :, :, -1:, :]
    qc = jax.nn.silu(to_chunks(q)) * (HGRN_HEAD_DIM ** -0.5)
    v = to_chunks(i)

    q_in = qc * jnp.exp(b)
    k_in = k * jnp.exp(-b)
    k_end = k * jnp.exp(b_last - b)
    causal = jnp.tril(jnp.ones((HGRN_CHUNK, HGRN_CHUNK), dtype=bool))
    att = jnp.where(causal, jnp.einsum('bhncd,bhnsd->bhncs', q_in, k_in), 0.0)
    o_intra = jnp.einsum('bhncs,bhnse->bhnce', att, v)

    def step(state, inp):
        q_n, k_n, v_n, dec_n = inp
        o_n = jnp.einsum('bhcd,bhde->bhce', q_n, state)
        state = dec_n[..., None] * state + jnp.einsum('bhcd,bhce->bhde', k_n, v_n)
        return state, o_n

    xs = (jnp.moveaxis(q_in, 2, 0), jnp.moveaxis(k_end, 2, 0), jnp.moveaxis(v, 2, 0),
          jnp.moveaxis(jnp.exp(b_last[:, :, :, 0, :]), 2, 0))
    init = jnp.zeros((bsz, HGRN_HEADS, HGRN_HEAD_DIM, HGRN_HEAD_DIM), jnp.float32)
    _, o_inter = lax.scan(step, init, xs)
    o = o_intra + jnp.moveaxis(o_inter, 0, 2)

    o = o.transpose(0, 2, 3, 1, 4).reshape(bsz, seq, HGRN_HEADS, HGRN_HEAD_DIM)
    o = o * lax.rsqrt(jnp.mean(o * o, axis=-1, keepdims=True) + EPS) * head_gain.astype(jnp.float32)
    o = o.reshape(bsz, seq, D_MODEL) * jax.nn.silu(g.astype(jnp.float32))
    return o.astype(u.dtype) @ w_out


def shared_kv(stream, kv_norm, w_kvf, b_f):
    bsz, seq, _ = stream.shape
    hk = rms_norm(stream, kv_norm) @ w_kvf
    k = hk[..., :D_MODEL].reshape(bsz, seq, FOX_HEADS, FOX_HEAD_DIM)
    v = hk[..., D_MODEL:2 * D_MODEL].reshape(bsz, seq, FOX_HEADS, FOX_HEAD_DIM)
    f_logit = hk[..., 2 * D_MODEL:].astype(jnp.float32) + b_f.astype(jnp.float32)
    dcum = jnp.cumsum(jax.nn.log_sigmoid(f_logit), axis=1).transpose(0, 2, 1)
    return k, v, dcum


def fox_mixer(u, k, v, dcum, w_q, w_out):
    bsz, seq, _ = u.shape
    q = (u @ w_q).reshape(bsz, seq, FOX_HEADS, FOX_HEAD_DIM) * (FOX_HEAD_DIM ** -0.5)
    outs = []
    for blk in range(seq // Q_BLOCK):
        start, end = blk * Q_BLOCK, (blk + 1) * Q_BLOCK
        logits = jnp.einsum('bqhd,bkhd->bhqk', q[:, start:end], k[:, :end]).astype(jnp.float32)
        logits = logits + dcum[:, :, start:end, None] - dcum[:, :, None, :end]
        causal = jnp.arange(start, end)[:, None] >= jnp.arange(end)[None, :]
        probs = jax.nn.softmax(jnp.where(causal, logits, -jnp.inf), axis=-1)
        outs.append(jnp.einsum('bhqk,bkhd->bqhd', probs.astype(v.dtype), v[:, :end]))
    o = jnp.concatenate(outs, axis=1).reshape(bsz, seq, D_MODEL)
    return o @ w_out


def sq_relu_mlp(u, w_up, w_down):
    hid = jax.nn.relu(u @ w_up)
    return (hid * hid) @ w_down


def setup_inputs(seed: int = 0) -> dict:
    key = jax.random.key(seed)
    ks = jax.random.split(key, 20)
    f32 = jnp.float32

    def dense(k, shape, fan_in):
        return jax.random.normal(k, shape, f32) * (fan_in ** -0.5)

    def gain(k, shape):
        return 1.0 + 0.02 * jax.random.normal(k, shape, f32)

    return {
        "x": jax.random.normal(ks[0], (BATCH, SEQ, D_MODEL), f32),
        "p": jax.random.normal(ks[1], (DEPTH, BATCH, SEQ, PLE_DIM), f32),
        "mix_norm": gain(ks[2], (DEPTH, D_MODEL)),
        "mlp_norm": gain(ks[3], (DEPTH, D_MODEL)),
        "ple_norm": gain(ks[4], (DEPTH, D_MODEL)),
        "w_a_in": dense(ks[5], (N_A_LAYERS, D_MODEL, 4 * D_MODEL), D_MODEL),
        "a_lb_logits": 0.3 * jax.random.normal(ks[6], (N_A_LAYERS + 1, D_MODEL), f32),
        "a_head_gain": gain(ks[7], (N_A_LAYERS, HGRN_HEAD_DIM)),
        "w_a_out": dense(ks[8], (N_A_LAYERS, D_MODEL, D_MODEL), D_MODEL),
        "kv_norm": gain(ks[9], (D_MODEL,)),
        "w_kvf": dense(ks[10], (D_MODEL, 2 * D_MODEL + FOX_HEADS), D_MODEL),
        "b_f": 2.0 + 0.5 * jax.random.normal(ks[11], (FOX_HEADS,), f32),
        "w_b_q": dense(ks[12], (N_B_LAYERS, D_MODEL, D_MODEL), D_MODEL),
        "w_b_out": dense(ks[13], (N_B_LAYERS, D_MODEL, D_MODEL), D_MODEL),
        "w_mlp_up": dense(ks[14], (DEPTH, D_MODEL, D_FF), D_MODEL),
        "w_mlp_down": dense(ks[15], (DEPTH, D_FF, D_MODEL), D_FF),
        "w_ple_gate": dense(ks[16], (DEPTH, D_MODEL, D_MODEL), D_MODEL),
        "w_ple_up": dense(ks[17], (DEPTH, PLE_DIM, D_MODEL), PLE_DIM),
        "final_norm": gain(ks[18], (D_MODEL,)),
    }


def reference(x, p, mix_norm, mlp_norm, ple_norm, w_a_in, a_lb_logits, a_head_gain, w_a_out,
              kv_norm, w_kvf, b_f, w_b_q, w_b_out, w_mlp_up, w_mlp_down, w_ple_gate, w_ple_up,
              final_norm):
    lb_all = jnp.cumsum(jax.nn.softmax(a_lb_logits.astype(jnp.float32), axis=0), axis=0)
    h = x
    k_sh = v_sh = d_sh = None
    for layer in range(DEPTH):
        u = rms_norm(h, mix_norm[layer])
        if layer < N_A_LAYERS:
            h = h + hgrn2_mixer(u, w_a_in[layer], lb_all[layer], a_head_gain[layer], w_a_out[layer])
        else:
            j = layer - N_A_LAYERS
            h = h + fox_mixer(u, k_sh, v_sh, d_sh, w_b_q[j], w_b_out[j])
        h = h + sq_relu_mlp(rms_norm(h, mlp_norm[layer]), w_mlp_up[layer], w_mlp_down[layer])
        gate = jax.nn.sigmoid(rms_norm(h, ple_norm[layer]) @ w_ple_gate[layer])
        h = h + (p[layer].astype(h.dtype) @ w_ple_up[layer]) * gate
        if layer == N_A_LAYERS - 1:
            k_sh, v_sh, d_sh = shared_kv(h, kv_norm, w_kvf, b_f)
    return rms_norm(h, final_norm)
```

```python
import functools

import jax
import jax.numpy as jnp
from jax import lax
from jax.experimental import pallas as pl
from jax.experimental.pallas import tpu as pltpu

F32 = jnp.float32
BF16 = jnp.bfloat16
EPS = 1e-6

HEAD_DIM = 128
PLE_DIM = 256
V7X_VMEM_LIMIT = 56 * 1024 * 1024
NORM_ROWS = 256


def _params(sem):
    return pltpu.CompilerParams(dimension_semantics=sem, vmem_limit_bytes=V7X_VMEM_LIMIT)


def _sigmoid(x):
    return 1.0 / (1.0 + jnp.exp(-x))


def _norm_rows(h_ref, gains, out_refs):
    rows = h_ref.shape[0]
    rc = min(NORM_ROWS, rows)

    def body(r, carry):
        r0 = pl.multiple_of(r * rc, rc)
        hb = h_ref[pl.ds(r0, rc), :].astype(F32)
        y = hb * lax.rsqrt(jnp.mean(hb * hb, axis=-1, keepdims=True) + EPS)
        for g, o_ref in zip(gains, out_refs):
            o_ref[pl.ds(r0, rc), :] = (y * g).astype(o_ref.dtype)
        return carry

    lax.fori_loop(0, rows // rc, body, 0)


def _norm_mm_kernel(x_ref, g_ref, w_ref, o_ref, xn_ref):
    @pl.when(pl.program_id(1) == 0)
    def _():
        _norm_rows(x_ref, [g_ref[...]], [xn_ref])

    o_ref[...] = jnp.dot(xn_ref[...], w_ref[...], preferred_element_type=F32).astype(o_ref.dtype)


def norm_matmul(x, gain, w, *, tm, tn, out_dtype):
    m, k = x.shape
    n = w.shape[1]
    return pl.pallas_call(
        _norm_mm_kernel,
        grid=(m // tm, n // tn),
        in_specs=[
            pl.BlockSpec((tm, k), lambda i, j: (i, 0)),
            pl.BlockSpec((1, k), lambda i, j: (0, 0)),
            pl.BlockSpec((k, tn), lambda i, j: (0, j)),
        ],
        out_specs=pl.BlockSpec((tm, tn), lambda i, j: (i, j)),
        out_shape=jax.ShapeDtypeStruct((m, n), out_dtype),
        scratch_shapes=[pltpu.VMEM((tm, k), BF16)],
        compiler_params=_params(("parallel", "arbitrary")),
        name="norm_matmul",
    )(x, gain.reshape(1, k), w)


def _mm_kernel(a_ref, w_ref, o_ref, *, scale):
    acc = jnp.dot(a_ref[...], w_ref[...], preferred_element_type=F32)
    if scale != 1.0:
        acc = acc * scale
    o_ref[...] = acc.astype(o_ref.dtype)


def matmul(a, w, *, tm, tn, out_dtype, scale=1.0):
    m, k = a.shape
    n = w.shape[1]
    return pl.pallas_call(
        functools.partial(_mm_kernel, scale=scale),
        grid=(m // tm, n // tn),
        in_specs=[
            pl.BlockSpec((tm, k), lambda i, j: (i, 0)),
            pl.BlockSpec((k, tn), lambda i, j: (0, j)),
        ],
        out_specs=pl.BlockSpec((tm, tn), lambda i, j: (i, j)),
        out_shape=jax.ShapeDtypeStruct((m, n), out_dtype),
        compiler_params=_params(("parallel", "arbitrary")),
        name="matmul",
    )(a, w)


def _mm_res_norm_kernel(a_ref, w_ref, x_ref, g_ref, h_ref, u_ref, *, tn):
    j = pl.program_id(1)
    acc = jnp.dot(a_ref[...], w_ref[...], preferred_element_type=F32)
    c0 = pl.multiple_of(j * tn, tn)
    h_ref[:, pl.ds(c0, tn)] = x_ref[...] + acc

    @pl.when(j == pl.num_programs(1) - 1)
    def _():
        _norm_rows(h_ref, [g_ref[...]], [u_ref])


def matmul_residual_norm(a, w, x, gain, *, tm, tn):
    m, k = a.shape
    n = w.shape[1]
    return pl.pallas_call(
        functools.partial(_mm_res_norm_kernel, tn=tn),
        grid=(m // tm, n // tn),
        in_specs=[
            pl.BlockSpec((tm, k), lambda i, j: (i, 0)),
            pl.BlockSpec((k, tn), lambda i, j: (0, j)),
            pl.BlockSpec((tm, tn), lambda i, j: (i, j)),
            pl.BlockSpec((1, n), lambda i, j: (0, 0)),
        ],
        out_specs=[
            pl.BlockSpec((tm, n), lambda i, j: (i, 0)),
            pl.BlockSpec((tm, n), lambda i, j: (i, 0)),
        ],
        out_shape=[
            jax.ShapeDtypeStruct((m, n), F32),
            jax.ShapeDtypeStruct((m, n), BF16),
        ],
        compiler_params=_params(("parallel", "arbitrary")),
        name="matmul_residual_norm",
    )(a, w, x, gain.reshape(1, n))


def _mlp_kernel(u_ref, wup_ref, wdn_ref, h_ref, g_ref, ho_ref, uo_ref):
    f = pl.program_id(1)

    @pl.when(f == 0)
    def _():
        ho_ref[...] = h_ref[...]

    hid = jnp.maximum(jnp.dot(u_ref[...], wup_ref[...], preferred_element_type=F32), 0.0)
    hid = (hid * hid).astype(BF16)
    ho_ref[...] += jnp.dot(hid, wdn_ref[...], preferred_element_type=F32)

    @pl.when(f == pl.num_programs(1) - 1)
    def _():
        _norm_rows(ho_ref, [g_ref[...]], [uo_ref])


def mlp_residual_norm(u, w_up, w_down, h, gain, *, tm, tf):
    m, d = u.shape
    dff = w_up.shape[1]
    return pl.pallas_call(
        _mlp_kernel,
        grid=(m // tm, dff // tf),
        in_specs=[
            pl.BlockSpec((tm, d), lambda i, f: (i, 0)),
            pl.BlockSpec((d, tf), lambda i, f: (0, f)),
            pl.BlockSpec((tf, d), lambda i, f: (f, 0)),
            pl.BlockSpec((tm, d), lambda i, f: (i, 0)),
            pl.BlockSpec((1, d), lambda i, f: (0, 0)),
        ],
        out_specs=[
            pl.BlockSpec((tm, d), lambda i, f: (i, 0)),
            pl.BlockSpec((tm, d), lambda i, f: (i, 0)),
        ],
        out_shape=[
            jax.ShapeDtypeStruct((m, d), F32),
            jax.ShapeDtypeStruct((m, d), BF16),
        ],
        compiler_params=_params(("parallel", "arbitrary")),
        name="mlp_residual_norm",
    )(u, w_up, w_down, h, gain.reshape(1, d))


def _ple_kernel(u_ref, p_ref, wg_ref, wu_ref, h_ref, *rest, tn, n_norm, emit_h):
    g_refs = rest[:n_norm]
    out_refs = rest[n_norm:]
    if emit_h:
        ho_ref, norm_refs = out_refs[0], out_refs[1:1 + n_norm]
    else:
        norm_refs, ho_ref = out_refs[:n_norm], out_refs[n_norm]
    j = pl.program_id(1)
    gate = _sigmoid(jnp.dot(u_ref[...], wg_ref[...], preferred_element_type=F32))
    up = jnp.dot(p_ref[...].astype(BF16), wu_ref[...], preferred_element_type=F32)
    c0 = pl.multiple_of(j * tn, tn)
    ho_ref[:, pl.ds(c0, tn)] = h_ref[...] + up * gate

    @pl.when(j == pl.num_programs(1) - 1)
    def _():
        _norm_rows(ho_ref, [g[...] for g in g_refs], list(norm_refs))


def ple_residual_norm(u, p, w_gate, w_up, h, gains, norm_dtypes, *, tm, tn, emit_h):
    m, d = u.shape
    n_norm = len(gains)
    row_spec = pl.BlockSpec((tm, d), lambda i, j: (i, 0))
    out_specs = [row_spec] * n_norm
    out_shape = [jax.ShapeDtypeStruct((m, d), dt) for dt in norm_dtypes]
    scratch = []
    if emit_h:
        out_specs = [row_spec] + out_specs
        out_shape = [jax.ShapeDtypeStruct((m, d), F32)] + out_shape
    else:
        scratch = [pltpu.VMEM((tm, d), F32)]
    return pl.pallas_call(
        functools.partial(_ple_kernel, tn=tn, n_norm=n_norm, emit_h=emit_h),
        grid=(m // tm, d // tn),
        in_specs=[
            row_spec,
            pl.BlockSpec((tm, PLE_DIM), lambda i, j: (i, 0)),
            pl.BlockSpec((d, tn), lambda i, j: (0, j)),
            pl.BlockSpec((PLE_DIM, tn), lambda i, j: (0, j)),
            pl.BlockSpec((tm, tn), lambda i, j: (i, j)),
        ] + [pl.BlockSpec((1, d), lambda i, j: (0, 0))] * n_norm,
        out_specs=out_specs,
        out_shape=out_shape,
        scratch_shapes=scratch,
        compiler_params=_params(("parallel", "arbitrary")),
        name="ple_residual_norm",
    )(u, p, w_gate, w_up, h, *[g.reshape(1, d) for g in gains])


def _split3(x):
    hi = x.astype(BF16)
    r1 = x - hi.astype(F32)
    mid = r1.astype(BF16)
    lo = (r1 - mid.astype(F32)).astype(BF16)
    return hi, mid, lo


def _tri_cumsum(tri, x):
    hi, mid, lo = _split3(x)
    dot = lambda t: jnp.dot(tri, t, preferred_element_type=F32)
    return dot(hi) + dot(mid) + dot(lo)


def _hgrn_kernel(q_ref, f_ref, i_ref, g_ref, lbl_ref, hg_ref, o_ref, st_ref, *, chunk, layer):
    seq = q_ref.shape[0]
    lg = lbl_ref[...]
    e = jnp.exp(lg - jnp.max(lg, axis=0, keepdims=True))
    sm = e / jnp.sum(e, axis=0, keepdims=True)
    lb = jnp.sum(sm[:layer + 1], axis=0, keepdims=True)
    head_gain = hg_ref[...]
    row = lax.broadcasted_iota(jnp.int32, (chunk, chunk), 0)
    col = lax.broadcasted_iota(jnp.int32, (chunk, chunk), 1)
    causal = row >= col
    tri = jnp.where(causal, 1.0, 0.0).astype(BF16)
    st_ref[...] = jnp.zeros_like(st_ref)
    nt = (((1,), (1,)), ((), ()))
    tn = (((0,), (0,)), ((), ()))

    def body(c, carry):
        r0 = pl.multiple_of(c * chunk, chunk)
        rows = pl.ds(r0, chunk)
        fg = lb + (1.0 - lb) * _sigmoid(f_ref[rows, :])
        k = 1.0 - fg
        b = _tri_cumsum(tri, jnp.log(fg))
        b_last = b[chunk - 1:chunk, :]
        qv = q_ref[rows, :]
        q_in = (qv * _sigmoid(qv) * (HEAD_DIM ** -0.5) * jnp.exp(b)).astype(BF16)
        k_in = (k * jnp.exp(-b)).astype(BF16)
        k_end = (k * jnp.exp(b_last - b)).astype(BF16)
        v = i_ref[rows, :].astype(BF16)
        att = lax.dot_general(q_in, k_in, nt, preferred_element_type=F32)
        att = jnp.where(causal, att, 0.0).astype(BF16)
        st = st_ref[...]
        o = jnp.dot(att, v, preferred_element_type=F32)
        o = o + lax.dot_general(q_in, st.astype(BF16), nt, preferred_element_type=F32)
        st_ref[...] = jnp.exp(b_last) * st + lax.dot_general(v, k_end, tn, preferred_element_type=F32)
        o = o * lax.rsqrt(jnp.mean(o * o, axis=-1, keepdims=True) + EPS) * head_gain
        gv = g_ref[rows, :]
        o_ref[rows, :] = (o * (gv * _sigmoid(gv))).astype(o_ref.dtype)
        return carry

    lax.fori_loop(0, seq // chunk, body, 0)


def hgrn2(proj, lb_logits, head_gain, *, layer, chunk):
    bsz, seq, d4 = proj.shape
    d = d4 // 4
    heads = d // HEAD_DIM
    nl = lb_logits.shape[0]
    col = lambda part: pl.BlockSpec((None, seq, HEAD_DIM), lambda b, h: (b, 0, part * heads + h))
    return pl.pallas_call(
        functools.partial(_hgrn_kernel, chunk=chunk, layer=layer),
        grid=(bsz, heads),
        in_specs=[
            col(0), col(1), col(2), col(3),
            pl.BlockSpec((nl, HEAD_DIM), lambda b, h: (0, h)),
            pl.BlockSpec((1, HEAD_DIM), lambda b, h: (0, 0)),
        ],
        out_specs=pl.BlockSpec((None, seq, HEAD_DIM), lambda b, h: (b, 0, h)),
        out_shape=jax.ShapeDtypeStruct((bsz, seq, d), BF16),
        scratch_shapes=[pltpu.VMEM((HEAD_DIM, HEAD_DIM), F32)],
        compiler_params=_params(("parallel", "parallel")),
        name="hgrn2",
    )(proj, proj, proj, proj, lb_logits, head_gain.reshape(1, HEAD_DIM))


def _dcum_kernel(u_ref, wf_ref, bf_ref, dcol_ref, drow_ref, *, blk, heads):
    seq = u_ref.shape[0]
    row = lax.broadcasted_iota(jnp.int32, (blk, blk), 0)
    col = lax.broadcasted_iota(jnp.int32, (blk, blk), 1)
    tri = jnp.where(row >= col, 1.0, 0.0).astype(BF16)
    carry = jnp.zeros((1, wf_ref.shape[1]), F32)
    for n in range(seq // blk):
        rows = slice(n * blk, (n + 1) * blk)
        logit = jnp.dot(u_ref[rows, :], wf_ref[...], preferred_element_type=F32) + bf_ref[...]
        logsig = jnp.minimum(logit, 0.0) - jnp.log(1.0 + jnp.exp(-jnp.abs(logit)))
        c = _tri_cumsum(tri, logsig) + carry
        dcol_ref[rows, :] = c
        carry = c[blk - 1:blk, :]
    drow_ref[...] = dcol_ref[...].T[:heads, :]


def forget_cumsum(u, w_f, b_f, *, bsz, heads, blk):
    t, d = u.shape
    seq = t // bsz
    lanes = w_f.shape[1]
    return pl.pallas_call(
        functools.partial(_dcum_kernel, blk=blk, heads=heads),
        grid=(bsz,),
        in_specs=[
            pl.BlockSpec((seq, d), lambda b: (b, 0)),
            pl.BlockSpec((d, lanes), lambda b: (0, 0)),
            pl.BlockSpec((1, lanes), lambda b: (0, 0)),
        ],
        out_specs=[
            pl.BlockSpec((None, seq, lanes), lambda b: (b, 0, 0)),
            pl.BlockSpec((None, heads, seq), lambda b: (b, 0, 0)),
        ],
        out_shape=[
            jax.ShapeDtypeStruct((bsz, seq, lanes), F32),
            jax.ShapeDtypeStruct((bsz, heads, seq), F32),
        ],
        compiler_params=_params(("parallel",)),
        name="forget_cumsum",
    )(u, w_f, b_f)


def _fox_kernel(q_ref, k_ref, v_ref, dcol_ref, drow_ref, o_ref, *, tq):
    seq = q_ref.shape[0]
    head = pl.program_id(1)
    lane = lax.broadcasted_iota(jnp.int32, (tq, dcol_ref.shape[1]), 1)
    row = lax.broadcasted_iota(jnp.int32, (tq, tq), 0)
    col = lax.broadcasted_iota(jnp.int32, (tq, tq), 1)
    causal = row >= col
    nt = (((1,), (1,)), ((), ()))
    for qi in range(seq // tq):
        rows = slice(qi * tq, (qi + 1) * tq)
        q = q_ref[rows, :]
        dq = jnp.sum(jnp.where(lane == head, dcol_ref[rows, :], 0.0), axis=1, keepdims=True)
        m = jnp.full((tq, 1), -jnp.inf, F32)
        l = jnp.zeros((tq, 1), F32)
        acc = jnp.zeros((tq, HEAD_DIM), F32)
        for ki in range(qi + 1):
            cols = slice(ki * tq, (ki + 1) * tq)
            s = lax.dot_general(q, k_ref[cols, :], nt, preferred_element_type=F32)
            s = s + dq - drow_ref[:, cols]
            if ki == qi:
                s = jnp.where(causal, s, -jnp.inf)
            m_new = jnp.maximum(m, jnp.max(s, axis=1, keepdims=True))
            alpha = jnp.exp(m - m_new)
            prob = jnp.exp(s - m_new)
            l = alpha * l + jnp.sum(prob, axis=1, keepdims=True)
            acc = alpha * acc + jnp.dot(prob.astype(BF16), v_ref[cols, :], preferred_element_type=F32)
            m = m_new
        o_ref[rows, :] = (acc / l).astype(o_ref.dtype)


def fox_attention(q, kv, dcol, drow, *, tq):
    bsz, seq, d = q.shape
    heads = d // HEAD_DIM
    return pl.pallas_call(
        functools.partial(_fox_kernel, tq=tq),
        grid=(bsz, heads),
        in_specs=[
            pl.BlockSpec((None, seq, HEAD_DIM), lambda b, h: (b, 0, h)),
            pl.BlockSpec((None, seq, HEAD_DIM), lambda b, h: (b, 0, h)),
            pl.BlockSpec((None, seq, HEAD_DIM), lambda b, h: (b, 0, heads + h)),
            pl.BlockSpec((None, seq, dcol.shape[2]), lambda b, h: (b, 0, 0)),
            pl.BlockSpec((None, 1, seq), lambda b, h: (b * heads + h, 0, 0)),
        ],
        out_specs=pl.BlockSpec((None, seq, HEAD_DIM), lambda b, h: (b, 0, h)),
        out_shape=jax.ShapeDtypeStruct((bsz, seq, d), BF16),
        compiler_params=_params(("parallel", "parallel")),
        name="fox_attention",
    )(q, kv, kv, dcol, drow)


def kernel(x, p, mix_norm, mlp_norm, ple_norm, w_a_in, a_lb_logits, a_head_gain, w_a_out,
           kv_norm, w_kvf, b_f, w_b_q, w_b_out, w_mlp_up, w_mlp_down, w_ple_gate, w_ple_up,
           final_norm):
    bsz, seq, d = x.shape
    depth = p.shape[0]
    n_a = w_a_in.shape[0]
    t = bsz * seq
    heads = d // HEAD_DIM
    assert depth == n_a + w_b_q.shape[0] and n_a >= 1 and depth > n_a
    bf = lambda w: w.astype(BF16)

    h = x.reshape(t, d)
    u = None
    kv = dcol = drow = None
    out = None
    for layer in range(depth):
        if layer < n_a:
            if u is None:
                proj = norm_matmul(h, mix_norm[layer], bf(w_a_in[layer]), tm=1024, tn=512, out_dtype=F32)
            else:
                proj = matmul(u, bf(w_a_in[layer]), tm=1024, tn=512, out_dtype=F32)
            mixed = hgrn2(proj.reshape(bsz, seq, 4 * d), a_lb_logits, a_head_gain[layer],
                          layer=layer, chunk=64).reshape(t, d)
            w_out = w_a_out[layer]
        else:
            j = layer - n_a
            if u is None:
                raise NotImplementedError("attention layers need a preceding HGRN2 layer")
            q = matmul(u, bf(w_b_q[j]), tm=1024, tn=512, out_dtype=BF16, scale=HEAD_DIM ** -0.5)
            mixed = fox_attention(q.reshape(bsz, seq, d), kv, dcol, drow, tq=512).reshape(t, d)
            w_out = w_b_out[j]
        h, u = matmul_residual_norm(mixed, bf(w_out), h, mlp_norm[layer], tm=1024, tn=512)
        h, u = mlp_residual_norm(u, bf(w_mlp_up[layer]), bf(w_mlp_down[layer]), h, ple_norm[layer],
                                 tm=512, tf=512)
        p_l = p[layer].reshape(t, PLE_DIM)
        if layer == depth - 1:
            (out,) = ple_residual_norm(u, p_l, bf(w_ple_gate[layer]), bf(w_ple_up[layer]), h,
                                       [final_norm], [x.dtype], tm=512, tn=512, emit_h=False)
        elif layer == n_a - 1:
            h, u_kv, u = ple_residual_norm(u, p_l, bf(w_ple_gate[layer]), bf(w_ple_up[layer]), h,
                                           [kv_norm, mix_norm[layer + 1]], [BF16, BF16],
                                           tm=512, tn=512, emit_h=True)
            kv = matmul(u_kv, bf(w_kvf[:, :2 * d]), tm=1024, tn=512, out_dtype=BF16).reshape(bsz, seq, 2 * d)
            w_f = jnp.pad(bf(w_kvf[:, 2 * d:]), ((0, 0), (0, HEAD_DIM - heads)))
            b_fp = jnp.pad(b_f.astype(F32), (0, HEAD_DIM - heads)).reshape(1, HEAD_DIM)
            dcol, drow = forget_cumsum(u_kv, w_f, b_fp, bsz=bsz, heads=heads, blk=256)
            drow = drow.reshape(bsz * heads, 1, seq)
        else:
            h, u = ple_residual_norm(u, p_l, bf(w_ple_gate[layer]), bf(w_ple_up[layer]), h,
                                     [mix_norm[layer + 1]], [BF16], tm=512, tn=512, emit_h=True)
    return out.reshape(bsz, seq, d)
```

```python
import functools

import jax
import jax.numpy as jnp
from jax import lax
from jax.experimental import pallas as pl
from jax.experimental.pallas import tpu as pltpu

F32 = jnp.float32
BF16 = jnp.bfloat16
EPS = 1e-6

HEAD_DIM = 128
LANES = 128
V7X_VMEM_LIMIT = 56 * 1024 * 1024
NORM_ROWS = 256

TILES = dict(
    in_proj=dict(tm=1024, tn=1024),
    proj=dict(tm=1024, tn=1024),
    out_proj=dict(tm=1024, tn=512),
    mlp=dict(tm=1024, tf=512),
    ple=dict(tm=512, tn=512),
    hgrn=dict(chunk=64, unroll=8),
    dcum=dict(blk=256),
    attn=dict(tq=512),
)


def _params(sem):
    return pltpu.CompilerParams(dimension_semantics=sem, vmem_limit_bytes=V7X_VMEM_LIMIT)


def _once(block_shape, index_map):
    return pl.BlockSpec(block_shape, index_map, pipeline_mode=pl.Buffered(1))


def _sigmoid(x):
    return 1.0 / (1.0 + jnp.exp(-x))


def _norm_rows(h_ref, gains, out_refs):
    rows = h_ref.shape[0]
    rc = min(NORM_ROWS, rows)

    def body(r, carry):
        r0 = pl.multiple_of(r * rc, rc)
        hb = h_ref[pl.ds(r0, rc), :].astype(F32)
        y = hb * lax.rsqrt(jnp.mean(hb * hb, axis=-1, keepdims=True) + EPS)
        for g, o_ref in zip(gains, out_refs):
            o_ref[pl.ds(r0, rc), :] = (y * g).astype(o_ref.dtype)
        return carry

    lax.fori_loop(0, rows // rc, body, 0)


def _dot(a, w_ref_or_val):
    return jnp.dot(a, w_ref_or_val.astype(BF16), preferred_element_type=F32)


def _gain3(g):
    return g.reshape(g.shape[0], 1, g.shape[1])


def _gain_spec(layer, d):
    return pl.BlockSpec((None, 1, d), lambda i, j: (layer, 0, 0))


def _norm_mm_kernel(x_ref, g_ref, w_ref, o_ref, xn_ref):
    @pl.when(pl.program_id(1) == 0)
    def _():
        _norm_rows(x_ref, [g_ref[...]], [xn_ref])

    o_ref[...] = _dot(xn_ref[...], w_ref[...]).astype(o_ref.dtype)


def norm_matmul(x, gains, w, layer, *, tm, tn, out_dtype):
    m, k = x.shape
    n = w.shape[2]
    return pl.pallas_call(
        _norm_mm_kernel,
        grid=(m // tm, n // tn),
        in_specs=[
            pl.BlockSpec((tm, k), lambda i, j: (i, 0)),
            _gain_spec(layer, k),
            pl.BlockSpec((None, k, tn), lambda i, j: (layer, 0, j)),
        ],
        out_specs=pl.BlockSpec((tm, tn), lambda i, j: (i, j)),
        out_shape=jax.ShapeDtypeStruct((m, n), out_dtype),
        scratch_shapes=[pltpu.VMEM((tm, k), BF16)],
        compiler_params=_params(("parallel", "arbitrary")),
        name="norm_matmul",
    )(x, _gain3(gains), w)


def _mm_kernel(a_ref, w_ref, o_ref, *, scale):
    acc = _dot(a_ref[...], w_ref[...])
    if scale != 1.0:
        acc = acc * scale
    o_ref[...] = acc.astype(o_ref.dtype)


def matmul(a, w, layer, n, *, tm, tn, out_dtype, scale=1.0):
    m, k = a.shape
    if w.ndim == 3:
        w_spec = pl.BlockSpec((None, k, tn), lambda i, j: (layer, 0, j))
    else:
        w_spec = pl.BlockSpec((k, tn), lambda i, j: (0, j))
    return pl.pallas_call(
        functools.partial(_mm_kernel, scale=scale),
        grid=(m // tm, n // tn),
        in_specs=[pl.BlockSpec((tm, k), lambda i, j: (i, 0)), w_spec],
        out_specs=pl.BlockSpec((tm, tn), lambda i, j: (i, j)),
        out_shape=jax.ShapeDtypeStruct((m, n), out_dtype),
        compiler_params=_params(("parallel", "arbitrary")),
        name="matmul",
    )(a, w)


def _mm_res_norm_kernel(a_ref, w_ref, x_ref, g_ref, h_ref, u_ref, *, tn):
    j = pl.program_id(1)
    c0 = pl.multiple_of(j * tn, tn)
    h_ref[:, pl.ds(c0, tn)] = x_ref[...] + _dot(a_ref[...], w_ref[...])

    @pl.when(j == pl.num_programs(1) - 1)
    def _():
        _norm_rows(h_ref, [g_ref[...]], [u_ref])


def matmul_residual_norm(a, w, layer, x, gains, g_layer, *, tm, tn):
    m, k = a.shape
    n = w.shape[2]
    return pl.pallas_call(
        functools.partial(_mm_res_norm_kernel, tn=tn),
        grid=(m // tm, n // tn),
        in_specs=[
            pl.BlockSpec((tm, k), lambda i, j: (i, 0)),
            pl.BlockSpec((None, k, tn), lambda i, j: (layer, 0, j)),
            pl.BlockSpec((tm, tn), lambda i, j: (i, j)),
            _gain_spec(g_layer, n),
        ],
        out_specs=[
            pl.BlockSpec((tm, n), lambda i, j: (i, 0)),
            pl.BlockSpec((tm, n), lambda i, j: (i, 0)),
        ],
        out_shape=[
            jax.ShapeDtypeStruct((m, n), F32),
            jax.ShapeDtypeStruct((m, n), BF16),
        ],
        compiler_params=_params(("parallel", "arbitrary")),
        name="matmul_residual_norm",
    )(a, w, x, _gain3(gains))


def _mlp_kernel(u_ref, wup_ref, wdn_ref, h_ref, g_ref, ho_ref, uo_ref, *, cn):
    f = pl.program_id(1)
    d = ho_ref.shape[1]
    hw = h_ref.shape[1]

    @pl.when(f == 0)
    def _():
        ho_ref[...] = jnp.zeros_like(ho_ref)

    hid = jnp.maximum(_dot(u_ref[...], wup_ref[...]), 0.0)
    hid = (hid * hid).astype(BF16)
    for n in range(d // cn):
        cs = slice(n * cn, (n + 1) * cn)
        ho_ref[:, cs] += _dot(hid, wdn_ref[:, cs])

    @pl.when(f < d // hw)
    def _():
        c0 = pl.multiple_of(f * hw, hw)
        ho_ref[:, pl.ds(c0, hw)] += h_ref[...]

    @pl.when(f == pl.num_programs(1) - 1)
    def _():
        _norm_rows(ho_ref, [g_ref[...]], [uo_ref])


def mlp_residual_norm(u, w_up, w_down, layer, h, gains, *, tm, tf):
    m, d = u.shape
    dff = w_up.shape[2]
    nf = dff // tf
    hw = max(LANES, d // nf)
    n_pieces = d // hw
    return pl.pallas_call(
        functools.partial(_mlp_kernel, cn=512),
        grid=(m // tm, nf),
        in_specs=[
            _once((tm, d), lambda i, f: (i, 0)),
            pl.BlockSpec((None, d, tf), lambda i, f: (layer, 0, f)),
            pl.BlockSpec((None, tf, d), lambda i, f: (layer, f, 0)),
            pl.BlockSpec((tm, hw), lambda i, f: (i, jnp.minimum(f, n_pieces - 1))),
            _gain_spec(layer, d),
        ],
        out_specs=[
            _once((tm, d), lambda i, f: (i, 0)),
            _once((tm, d), lambda i, f: (i, 0)),
        ],
        out_shape=[
            jax.ShapeDtypeStruct((m, d), F32),
            jax.ShapeDtypeStruct((m, d), BF16),
        ],
        compiler_params=_params(("parallel", "arbitrary")),
        name="mlp_residual_norm",
    )(u, w_up, w_down, h, _gain3(gains))


def _ple_kernel(u_ref, p_ref, wg_ref, wu_ref, h_ref, *rest, tn, n_norm, emit_h):
    g_refs = rest[:n_norm]
    out_refs = rest[n_norm:]
    if emit_h:
        ho_ref, norm_refs = out_refs[0], out_refs[1:1 + n_norm]
    else:
        norm_refs, ho_ref = out_refs[:n_norm], out_refs[n_norm]
    j = pl.program_id(1)
    gate = _sigmoid(_dot(u_ref[...], wg_ref[...]))
    up = _dot(p_ref[...].astype(BF16), wu_ref[...])
    c0 = pl.multiple_of(j * tn, tn)
    ho_ref[:, pl.ds(c0, tn)] = h_ref[...] + up * gate

    @pl.when(j == pl.num_programs(1) - 1)
    def _():
        _norm_rows(ho_ref, [g[...] for g in g_refs], list(norm_refs))


def ple_residual_norm(u, p, w_gate, w_up, layer, h, gains, norm_dtypes, *, tm, tn, emit_h):
    m, d = u.shape
    ple_dim = p.shape[2]
    n_norm = len(gains)
    row_spec = pl.BlockSpec((tm, d), lambda i, j: (i, 0))
    out_specs = [row_spec] * n_norm
    out_shape = [jax.ShapeDtypeStruct((m, d), dt) for dt in norm_dtypes]
    scratch = []
    if emit_h:
        out_specs = [row_spec] + out_specs
        out_shape = [jax.ShapeDtypeStruct((m, d), F32)] + out_shape
    else:
        scratch = [pltpu.VMEM((tm, d), F32)]
    return pl.pallas_call(
        functools.partial(_ple_kernel, tn=tn, n_norm=n_norm, emit_h=emit_h),
        grid=(m // tm, d // tn),
        in_specs=[
            row_spec,
            pl.BlockSpec((None, tm, ple_dim), lambda i, j: (layer, i, 0)),
            pl.BlockSpec((None, d, tn), lambda i, j: (layer, 0, j)),
            pl.BlockSpec((None, ple_dim, tn), lambda i, j: (layer, 0, j)),
            pl.BlockSpec((tm, tn), lambda i, j: (i, j)),
        ] + [_gain_spec(gl, d) for _, gl in gains],
        out_specs=out_specs,
        out_shape=out_shape,
        scratch_shapes=scratch,
        compiler_params=_params(("parallel", "arbitrary")),
        name="ple_residual_norm",
    )(u, p, w_gate, w_up, h, *[g for g, _ in gains])


def _split3(x):
    hi = x.astype(BF16)
    r1 = x - hi.astype(F32)
    mid = r1.astype(BF16)
    lo = (r1 - mid.astype(F32)).astype(BF16)
    return hi, mid, lo


def _tri_cumsum(tri, x):
    hi, mid, lo = _split3(x)
    dot = lambda t: jnp.dot(tri, t, preferred_element_type=F32)
    return dot(hi) + dot(mid) + dot(lo)


def _hgrn_kernel(q_ref, f_ref, i_ref, g_ref, lbl_ref, hg_ref, o_ref, st_ref, *, chunk, unroll, layer):
    seq = q_ref.shape[0]
    blk = chunk * unroll
    mid = chunk // 2
    lg = lbl_ref[...]
    e = jnp.exp(lg - jnp.max(lg, axis=0, keepdims=True))
    sm = e / jnp.sum(e, axis=0, keepdims=True)
    lb = jnp.sum(sm[:layer + 1], axis=0, keepdims=True)
    head_gain = hg_ref[...]
    row = lax.broadcasted_iota(jnp.int32, (chunk, chunk), 0)
    col = lax.broadcasted_iota(jnp.int32, (chunk, chunk), 1)
    causal = row >= col
    tri = jnp.where(causal, 1.0, 0.0).astype(BF16)
    st_ref[...] = jnp.zeros_like(st_ref)
    nt = (((1,), (1,)), ((), ()))
    tn = (((0,), (0,)), ((), ()))

    def body(c, carry):
        r0 = pl.multiple_of(c * blk, blk)
        rows = pl.ds(r0, blk)
        fg = lb + (1.0 - lb) * _sigmoid(f_ref[rows, :])
        k_all = 1.0 - fg
        logf = jnp.log(fg)
        qv = q_ref[rows, :]
        qc = qv * _sigmoid(qv) * (HEAD_DIM ** -0.5)
        v_all = i_ref[rows, :].astype(BF16)
        parts = []
        for u in range(unroll):
            sl = slice(u * chunk, (u + 1) * chunk)
            b = _tri_cumsum(tri, logf[sl])
            b_mid = b[mid - 1:mid, :]
            b_last = b[chunk - 1:chunk, :]
            q_mid = (qc[sl] * jnp.exp(b - b_mid)).astype(BF16)
            k_mid = (k_all[sl] * jnp.exp(b_mid - b)).astype(BF16)
            q_in = (qc[sl] * jnp.exp(b)).astype(BF16)
            k_end = (k_all[sl] * jnp.exp(b_last - b)).astype(BF16)
            att = lax.dot_general(q_mid, k_mid, nt, preferred_element_type=F32)
            att = jnp.where(causal, att, 0.0).astype(BF16)
            o_intra = jnp.dot(att, v_all[sl], preferred_element_type=F32)
            kv = lax.dot_general(v_all[sl], k_end, tn, preferred_element_type=F32)
            parts.append((q_in, o_intra, kv, jnp.exp(b_last)))
        st = st_ref[...]
        outs = []
        for q_in, o_intra, kv, dec in parts:
            outs.append(o_intra + lax.dot_general(q_in, st.astype(BF16), nt, preferred_element_type=F32))
            st = dec * st + kv
        st_ref[...] = st
        o = jnp.concatenate(outs, axis=0)
        o = o * lax.rsqrt(jnp.mean(o * o, axis=-1, keepdims=True) + EPS) * head_gain
        gv = g_ref[rows, :]
        o_ref[rows, :] = (o * (gv * _sigmoid(gv))).astype(o_ref.dtype)
        return carry

    lax.fori_loop(0, seq // blk, body, 0)


def hgrn2(proj, lb_logits, head_gains, layer, *, chunk, unroll):
    bsz, seq, d4 = proj.shape
    d = d4 // 4
    heads = d // HEAD_DIM
    nl = lb_logits.shape[0]
    col = lambda part: pl.BlockSpec((None, seq, HEAD_DIM), lambda b, h: (b, 0, part * heads + h))
    return pl.pallas_call(
        functools.partial(_hgrn_kernel, chunk=chunk, unroll=unroll, layer=layer),
        grid=(bsz, heads),
        in_specs=[
            col(0), col(1), col(2), col(3),
            pl.BlockSpec((nl, HEAD_DIM), lambda b, h: (0, h)),
            pl.BlockSpec((None, 1, HEAD_DIM), lambda b, h: (layer, 0, 0)),
        ],
        out_specs=pl.BlockSpec((None, seq, HEAD_DIM), lambda b, h: (b, 0, h)),
        out_shape=jax.ShapeDtypeStruct((bsz, seq, d), BF16),
        scratch_shapes=[pltpu.VMEM((HEAD_DIM, HEAD_DIM), F32)],
        compiler_params=_params(("parallel", "parallel")),
        name="hgrn2",
    )(proj, proj, proj, proj, lb_logits, _gain3(head_gains))


def _dcum_kernel(u_ref, wf_ref, bf_ref, dcol_ref, drow_ref, *, blk, heads):
    seq = u_ref.shape[0]
    row = lax.broadcasted_iota(jnp.int32, (blk, blk), 0)
    col = lax.broadcasted_iota(jnp.int32, (blk, blk), 1)
    tri = jnp.where(row >= col, 1.0, 0.0).astype(BF16)
    w_f = wf_ref[...].astype(BF16)
    carry = jnp.zeros((1, wf_ref.shape[1]), F32)
    for n in range(seq // blk):
        rows = slice(n * blk, (n + 1) * blk)
        logit = jnp.dot(u_ref[rows, :], w_f, preferred_element_type=F32) + bf_ref[...]
        logsig = jnp.minimum(logit, 0.0) - jnp.log(1.0 + jnp.exp(-jnp.abs(logit)))
        c = _tri_cumsum(tri, logsig) + carry
        dcol_ref[rows, :] = c
        carry = c[blk - 1:blk, :]
    drow_ref[...] = dcol_ref[...].T[:heads, :]


def forget_cumsum(u, w_f, b_f, *, bsz, heads, blk):
    t, d = u.shape
    seq = t // bsz
    lanes = w_f.shape[1]
    return pl.pallas_call(
        functools.partial(_dcum_kernel, blk=blk, heads=heads),
        grid=(bsz,),
        in_specs=[
            pl.BlockSpec((seq, d), lambda b: (b, 0)),
            pl.BlockSpec((d, lanes), lambda b: (0, 0)),
            pl.BlockSpec((1, lanes), lambda b: (0, 0)),
        ],
        out_specs=[
            pl.BlockSpec((None, seq, lanes), lambda b: (b, 0, 0)),
            pl.BlockSpec((None, heads, seq), lambda b: (b, 0, 0)),
        ],
        out_shape=[
            jax.ShapeDtypeStruct((bsz, seq, lanes), F32),
            jax.ShapeDtypeStruct((bsz, heads, seq), F32),
        ],
        compiler_params=_params(("parallel",)),
        name="forget_cumsum",
    )(u, w_f, b_f)


def _fox_kernel(q_ref, k_ref, v_ref, dcol_ref, drow_ref, o_ref, *, tq):
    seq = q_ref.shape[0]
    head = pl.program_id(1)
    lane = lax.broadcasted_iota(jnp.int32, (tq, dcol_ref.shape[1]), 1)
    row = lax.broadcasted_iota(jnp.int32, (tq, tq), 0)
    col = lax.broadcasted_iota(jnp.int32, (tq, tq), 1)
    causal = row >= col
    nt = (((1,), (1,)), ((), ()))
    for qi in range(seq // tq):
        rows = slice(qi * tq, (qi + 1) * tq)
        q = q_ref[rows, :]
        dq = jnp.sum(jnp.where(lane == head, dcol_ref[rows, :], 0.0), axis=1, keepdims=True)
        m = jnp.full((tq, 1), -jnp.inf, F32)
        l = jnp.zeros((tq, 1), F32)
        acc = jnp.zeros((tq, HEAD_DIM), F32)
        for ki in range(qi + 1):
            cols = slice(ki * tq, (ki + 1) * tq)
            s = lax.dot_general(q, k_ref[cols, :], nt, preferred_element_type=F32)
            s = s + dq - drow_ref[:, cols]
            if ki == qi:
                s = jnp.where(causal, s, -jnp.inf)
            m_new = jnp.maximum(m, jnp.max(s, axis=1, keepdims=True))
            alpha = jnp.exp(m - m_new)
            prob = jnp.exp(s - m_new)
            l = alpha * l + jnp.sum(prob, axis=1, keepdims=True)
            acc = alpha * acc + jnp.dot(prob.astype(BF16), v_ref[cols, :], preferred_element_type=F32)
            m = m_new
        o_ref[rows, :] = (acc / l).astype(o_ref.dtype)


def fox_attention(q, kv, dcol, drow, *, tq):
    bsz, seq, d = q.shape
    heads = d // HEAD_DIM
    return pl.pallas_call(
        functools.partial(_fox_kernel, tq=tq),
        grid=(bsz, heads),
        in_specs=[
            pl.BlockSpec((None, seq, HEAD_DIM), lambda b, h: (b, 0, h)),
            pl.BlockSpec((None, seq, HEAD_DIM), lambda b, h: (b, 0, h)),
            pl.BlockSpec((None, seq, HEAD_DIM), lambda b, h: (b, 0, heads + h)),
            pl.BlockSpec((None, seq, dcol.shape[2]), lambda b, h: (b, 0, 0)),
            pl.BlockSpec((None, 1, seq), lambda b, h: (b * heads + h, 0, 0)),
        ],
        out_specs=pl.BlockSpec((None, seq, HEAD_DIM), lambda b, h: (b, 0, h)),
        out_shape=jax.ShapeDtypeStruct((bsz, seq, d), BF16),
        compiler_params=_params(("parallel", "parallel")),
        name="fox_attention",
    )(q, kv, kv, dcol, drow)


def kernel(x, p, mix_norm, mlp_norm, ple_norm, w_a_in, a_lb_logits, a_head_gain, w_a_out,
           kv_norm, w_kvf, b_f, w_b_q, w_b_out, w_mlp_up, w_mlp_down, w_ple_gate, w_ple_up,
           final_norm):
    bsz, seq, d = x.shape
    depth = p.shape[0]
    n_a = w_a_in.shape[0]
    t = bsz * seq
    heads = d // HEAD_DIM
    assert depth == n_a + w_b_q.shape[0] and n_a >= 1 and depth > n_a
    p3 = p.reshape(depth, t, p.shape[3])
    mix3, kv3, fin3 = _gain3(mix_norm), _gain3(kv_norm.reshape(1, d)), _gain3(final_norm.reshape(1, d))

    h = x.reshape(t, d)
    u = None
    kv = dcol = drow = None
    out = None
    for layer in range(depth):
        if layer < n_a:
            if u is None:
                proj = norm_matmul(h, mix_norm, w_a_in, layer, out_dtype=F32, **TILES["in_proj"])
            else:
                proj = matmul(u, w_a_in, layer, 4 * d, out_dtype=F32, **TILES["proj"])
            mixed = hgrn2(proj.reshape(bsz, seq, 4 * d), a_lb_logits, a_head_gain, layer,
                          **TILES["hgrn"]).reshape(t, d)
            w_out, j = w_a_out, layer
        else:
            j = layer - n_a
            if u is None:
                raise NotImplementedError("attention layers need a preceding HGRN2 layer")
            q = matmul(u, w_b_q, j, d, out_dtype=BF16, scale=HEAD_DIM ** -0.5, **TILES["proj"])
            mixed = fox_attention(q.reshape(bsz, seq, d), kv, dcol, drow, **TILES["attn"]).reshape(t, d)
            w_out = w_b_out
        h, u = matmul_residual_norm(mixed, w_out, j, h, mlp_norm, layer, **TILES["out_proj"])
        h, u = mlp_residual_norm(u, w_mlp_up, w_mlp_down, layer, h, ple_norm, **TILES["mlp"])
        ple = functools.partial(ple_residual_norm, u, p3, w_ple_gate, w_ple_up, layer, h, **TILES["ple"])
        if layer == depth - 1:
            (out,) = ple([(fin3, 0)], [x.dtype], emit_h=False)
        elif layer == n_a - 1:
            h, u_kv, u = ple([(kv3, 0), (mix3, layer + 1)], [BF16, BF16], emit_h=True)
            kv = matmul(u_kv, w_kvf, 0, 2 * d, out_dtype=BF16, **TILES["proj"]).reshape(bsz, seq, 2 * d)
            w_f = jnp.pad(w_kvf[:, 2 * d:], ((0, 0), (0, LANES - heads)))
            b_fp = jnp.pad(b_f.astype(F32), (0, LANES - heads)).reshape(1, LANES)
            dcol, drow = forget_cumsum(u_kv, w_f, b_fp, bsz=bsz, heads=heads, **TILES["dcum"])
            drow = drow.reshape(bsz * heads, 1, seq)
        else:
            h, u = ple([(mix3, layer + 1)], [BF16], emit_h=True)
    return out.reshape(bsz, seq, d)
```

```python
import functools

import jax
import jax.numpy as jnp
from jax import lax
from jax.experimental import pallas as pl
from jax.experimental.pallas import tpu as pltpu

F32 = jnp.float32
BF16 = jnp.bfloat16
EPS = 1e-6

HEAD_DIM = 128
LANES = 128
V7X_VMEM_LIMIT = 56 * 1024 * 1024
NORM_ROWS = 256

TILES = dict(
    in_proj=dict(tm=2048, tn=512),
    proj=dict(tm=2048, tn=512),
    out_proj=dict(tm=2048, tn=256),
    mlp=dict(tm=1024, tf=512),
    ple=dict(tm=1024, tn=512),
    hgrn=dict(chunk=64, unroll=8),
    dcum=dict(blk=256),
    attn=dict(tq=512),
)


def _params(sem):
    return pltpu.CompilerParams(dimension_semantics=sem, vmem_limit_bytes=V7X_VMEM_LIMIT)


def _once(block_shape, index_map):
    return pl.BlockSpec(block_shape, index_map, pipeline_mode=pl.Buffered(1))


def _sigmoid(x):
    return 1.0 / (1.0 + jnp.exp(-x))


def _norm_rows(h_ref, gains, out_refs):
    rows = h_ref.shape[0]
    rc = min(NORM_ROWS, rows)

    def body(r, carry):
        r0 = pl.multiple_of(r * rc, rc)
        hb = h_ref[pl.ds(r0, rc), :].astype(F32)
        y = hb * lax.rsqrt(jnp.mean(hb * hb, axis=-1, keepdims=True) + EPS)
        for g, o_ref in zip(gains, out_refs):
            o_ref[pl.ds(r0, rc), :] = (y * g).astype(o_ref.dtype)
        return carry

    lax.fori_loop(0, rows // rc, body, 0)


def _dot(a, w_ref_or_val):
    return jnp.dot(a, w_ref_or_val.astype(BF16), preferred_element_type=F32)


def _gain3(g):
    return g.reshape(g.shape[0], 1, g.shape[1])


def _gain_spec(layer, d):
    return pl.BlockSpec((None, 1, d), lambda i, j: (layer, 0, 0))


def _norm_mm_kernel(x_ref, g_ref, w_ref, o_ref, xn_ref):
    @pl.when(pl.program_id(1) == 0)
    def _():
        _norm_rows(x_ref, [g_ref[...]], [xn_ref])

    o_ref[...] = _dot(xn_ref[...], w_ref[...]).astype(o_ref.dtype)


def norm_matmul(x, gains, w, layer, *, tm, tn, out_dtype):
    m, k = x.shape
    n = w.shape[2]
    return pl.pallas_call(
        _norm_mm_kernel,
        grid=(m // tm, n // tn),
        in_specs=[
            _once((tm, k), lambda i, j: (i, 0)),
            _gain_spec(layer, k),
            pl.BlockSpec((None, k, tn), lambda i, j: (layer, 0, j)),
        ],
        out_specs=pl.BlockSpec((tm, tn), lambda i, j: (i, j)),
        out_shape=jax.ShapeDtypeStruct((m, n), out_dtype),
        scratch_shapes=[pltpu.VMEM((tm, k), BF16)],
        compiler_params=_params(("parallel", "arbitrary")),
        name="norm_matmul",
    )(x, _gain3(gains), w)


def _mm_kernel(a_ref, w_ref, o_ref, *, scale):
    acc = _dot(a_ref[...], w_ref[...])
    if scale != 1.0:
        acc = acc * scale
    o_ref[...] = acc.astype(o_ref.dtype)


def matmul(a, w, layer, n, *, tm, tn, out_dtype, scale=1.0):
    m, k = a.shape
    if w.ndim == 3:
        w_spec = pl.BlockSpec((None, k, tn), lambda i, j: (layer, 0, j))
    else:
        w_spec = pl.BlockSpec((k, tn), lambda i, j: (0, j))
    return pl.pallas_call(
        functools.partial(_mm_kernel, scale=scale),
        grid=(m // tm, n // tn),
        in_specs=[_once((tm, k), lambda i, j: (i, 0)), w_spec],
        out_specs=pl.BlockSpec((tm, tn), lambda i, j: (i, j)),
        out_shape=jax.ShapeDtypeStruct((m, n), out_dtype),
        compiler_params=_params(("parallel", "arbitrary")),
        name="matmul",
    )(a, w)


def _mm_res_norm_kernel(a_ref, w_ref, x_ref, g_ref, h_ref, u_ref, *, tn):
    j = pl.program_id(1)
    c0 = pl.multiple_of(j * tn, tn)
    h_ref[:, pl.ds(c0, tn)] = x_ref[...] + _dot(a_ref[...], w_ref[...])

    @pl.when(j == pl.num_programs(1) - 1)
    def _():
        _norm_rows(h_ref, [g_ref[...]], [u_ref])


def matmul_residual_norm(a, w, layer, x, gains, g_layer, *, tm, tn):
    m, k = a.shape
    n = w.shape[2]
    return pl.pallas_call(
        functools.partial(_mm_res_norm_kernel, tn=tn),
        grid=(m // tm, n // tn),
        in_specs=[
            _once((tm, k), lambda i, j: (i, 0)),
            pl.BlockSpec((None, k, tn), lambda i, j: (layer, 0, j)),
            pl.BlockSpec((tm, tn), lambda i, j: (i, j)),
            _gain_spec(g_layer, n),
        ],
        out_specs=[
            _once((tm, n), lambda i, j: (i, 0)),
            _once((tm, n), lambda i, j: (i, 0)),
        ],
        out_shape=[
            jax.ShapeDtypeStruct((m, n), F32),
            jax.ShapeDtypeStruct((m, n), BF16),
        ],
        compiler_params=_params(("parallel", "arbitrary")),
        name="matmul_residual_norm",
    )(a, w, x, _gain3(gains))


def _mlp_kernel(u_ref, wup_ref, wdn_ref, h_ref, g_ref, ho_ref, uo_ref, *, cn):
    f = pl.program_id(1)
    d = ho_ref.shape[1]
    hw = h_ref.shape[1]

    @pl.when(f == 0)
    def _():
        ho_ref[...] = jnp.zeros_like(ho_ref)

    hid = jnp.maximum(_dot(u_ref[...], wup_ref[...]), 0.0)
    hid = (hid * hid).astype(BF16)
    for n in range(d // cn):
        cs = slice(n * cn, (n + 1) * cn)
        ho_ref[:, cs] += _dot(hid, wdn_ref[:, cs])

    @pl.when(f < d // hw)
    def _():
        c0 = pl.multiple_of(f * hw, hw)
        ho_ref[:, pl.ds(c0, hw)] += h_ref[...]

    @pl.when(f == pl.num_programs(1) - 1)
    def _():
        _norm_rows(ho_ref, [g_ref[...]], [uo_ref])


def mlp_residual_norm(u, w_up, w_down, layer, h, gains, *, tm, tf):
    m, d = u.shape
    dff = w_up.shape[2]
    nf = dff // tf
    hw = max(LANES, d // nf)
    n_pieces = d // hw
    return pl.pallas_call(
        functools.partial(_mlp_kernel, cn=512),
        grid=(m // tm, nf),
        in_specs=[
            _once((tm, d), lambda i, f: (i, 0)),
            pl.BlockSpec((None, d, tf), lambda i, f: (layer, 0, f)),
            pl.BlockSpec((None, tf, d), lambda i, f: (layer, f, 0)),
            pl.BlockSpec((tm, hw), lambda i, f: (i, jnp.minimum(f, n_pieces - 1))),
            _gain_spec(layer, d),
        ],
        out_specs=[
            _once((tm, d), lambda i, f: (i, 0)),
            _once((tm, d), lambda i, f: (i, 0)),
        ],
        out_shape=[
            jax.ShapeDtypeStruct((m, d), F32),
            jax.ShapeDtypeStruct((m, d), BF16),
        ],
        compiler_params=_params(("parallel", "arbitrary")),
        name="mlp_residual_norm",
    )(u, w_up, w_down, h, _gain3(gains))


def _ple_kernel(u_ref, p_ref, wg_ref, wu_ref, h_ref, *rest, tn, n_norm, emit_h):
    g_refs = rest[:n_norm]
    out_refs = rest[n_norm:]
    if emit_h:
        ho_ref, norm_refs = out_refs[0], out_refs[1:1 + n_norm]
    else:
        norm_refs, ho_ref = out_refs[:n_norm], out_refs[n_norm]
    j = pl.program_id(1)
    gate = _sigmoid(_dot(u_ref[...], wg_ref[...]))
    up = _dot(p_ref[...].astype(BF16), wu_ref[...])
    c0 = pl.multiple_of(j * tn, tn)
    ho_ref[:, pl.ds(c0, tn)] = h_ref[...] + up * gate

    @pl.when(j == pl.num_programs(1) - 1)
    def _():
        _norm_rows(ho_ref, [g[...] for g in g_refs], list(norm_refs))


def ple_residual_norm(u, p, w_gate, w_up, layer, h, gains, norm_dtypes, *, tm, tn, emit_h):
    m, d = u.shape
    ple_dim = p.shape[2]
    n_norm = len(gains)
    row_spec = _once((tm, d), lambda i, j: (i, 0))
    out_specs = [row_spec] * n_norm
    out_shape = [jax.ShapeDtypeStruct((m, d), dt) for dt in norm_dtypes]
    scratch = []
    if emit_h:
        out_specs = [row_spec] + out_specs
        out_shape = [jax.ShapeDtypeStruct((m, d), F32)] + out_shape
    else:
        scratch = [pltpu.VMEM((tm, d), F32)]
    return pl.pallas_call(
        functools.partial(_ple_kernel, tn=tn, n_norm=n_norm, emit_h=emit_h),
        grid=(m // tm, d // tn),
        in_specs=[
            row_spec,
            pl.BlockSpec((None, tm, ple_dim), lambda i, j: (layer, i, 0)),
            pl.BlockSpec((None, d, tn), lambda i, j: (layer, 0, j)),
            pl.BlockSpec((None, ple_dim, tn), lambda i, j: (layer, 0, j)),
            pl.BlockSpec((tm, tn), lambda i, j: (i, j)),
        ] + [_gain_spec(gl, d) for _, gl in gains],
        out_specs=out_specs,
        out_shape=out_shape,
        scratch_shapes=scratch,
        compiler_params=_params(("parallel", "arbitrary")),
        name="ple_residual_norm",
    )(u, p, w_gate, w_up, h, *[g for g, _ in gains])


def _split3(x):
    hi = x.astype(BF16)
    r1 = x - hi.astype(F32)
    mid = r1.astype(BF16)
    lo = (r1 - mid.astype(F32)).astype(BF16)
    return hi, mid, lo


def _tri_cumsum(tri, x):
    hi, mid, lo = _split3(x)
    dot = lambda t: jnp.dot(tri, t, preferred_element_type=F32)
    return dot(hi) + dot(mid) + dot(lo)


def _hgrn_kernel(q_ref, f_ref, i_ref, g_ref, lbl_ref, hg_ref, o_ref, st_ref, *, chunk, unroll, layer):
    seq = q_ref.shape[0]
    blk = chunk * unroll
    mid = chunk // 2
    lg = lbl_ref[...]
    e = jnp.exp(lg - jnp.max(lg, axis=0, keepdims=True))
    sm = e / jnp.sum(e, axis=0, keepdims=True)
    lb = jnp.sum(sm[:layer + 1], axis=0, keepdims=True)
    head_gain = hg_ref[...]
    row = lax.broadcasted_iota(jnp.int32, (chunk, chunk), 0)
    col = lax.broadcasted_iota(jnp.int32, (chunk, chunk), 1)
    causal = row >= col
    tri = jnp.where(causal, 1.0, 0.0).astype(BF16)
    st_ref[...] = jnp.zeros_like(st_ref)
    nt = (((1,), (1,)), ((), ()))
    tn = (((0,), (0,)), ((), ()))

    def body(c, carry):
        r0 = pl.multiple_of(c * blk, blk)
        rows = pl.ds(r0, blk)
        fg = lb + (1.0 - lb) * _sigmoid(f_ref[rows, :])
        k_all = 1.0 - fg
        logf = jnp.log(fg)
        qv = q_ref[rows, :]
        qc = qv * _sigmoid(qv) * (HEAD_DIM ** -0.5)
        v_all = i_ref[rows, :].astype(BF16)
        parts = []
        for u in range(unroll):
            sl = slice(u * chunk, (u + 1) * chunk)
            b = _tri_cumsum(tri, logf[sl])
            b_mid = b[mid - 1:mid, :]
            b_last = b[chunk - 1:chunk, :]
            q_mid = (qc[sl] * jnp.exp(b - b_mid)).astype(BF16)
            k_mid = (k_all[sl] * jnp.exp(b_mid - b)).astype(BF16)
            q_in = (qc[sl] * jnp.exp(b)).astype(BF16)
            k_end = (k_all[sl] * jnp.exp(b_last - b)).astype(BF16)
            att = lax.dot_general(q_mid, k_mid, nt, preferred_element_type=F32)
            att = jnp.where(causal, att, 0.0).astype(BF16)
            o_intra = jnp.dot(att, v_all[sl], preferred_element_type=F32)
            kv = lax.dot_general(v_all[sl], k_end, tn, preferred_element_type=F32)
            parts.append((q_in, o_intra, kv, jnp.exp(b_last)))
        st = st_ref[...]
        outs = []
        for q_in, o_intra, kv, dec in parts:
            outs.append(o_intra + lax.dot_general(q_in, st.astype(BF16), nt, preferred_element_type=F32))
            st = dec * st + kv
        st_ref[...] = st
        o = jnp.concatenate(outs, axis=0)
        o = o * lax.rsqrt(jnp.mean(o * o, axis=-1, keepdims=True) + EPS) * head_gain
        gv = g_ref[rows, :]
        o_ref[rows, :] = (o * (gv * _sigmoid(gv))).astype(o_ref.dtype)
        return carry

    lax.fori_loop(0, seq // blk, body, 0)


def hgrn2(proj, lb_logits, head_gains, layer, *, chunk, unroll):
    bsz, seq, d4 = proj.shape
    d = d4 // 4
    heads = d // HEAD_DIM
    nl = lb_logits.shape[0]
    col = lambda part: pl.BlockSpec((None, seq, HEAD_DIM), lambda b, h: (b, 0, part * heads + h))
    return pl.pallas_call(
        functools.partial(_hgrn_kernel, chunk=chunk, unroll=unroll, layer=layer),
        grid=(bsz, heads),
        in_specs=[
            col(0), col(1), col(2), col(3),
            pl.BlockSpec((nl, HEAD_DIM), lambda b, h: (0, h)),
            pl.BlockSpec((None, 1, HEAD_DIM), lambda b, h: (layer, 0, 0)),
        ],
        out_specs=pl.BlockSpec((None, seq, HEAD_DIM), lambda b, h: (b, 0, h)),
        out_shape=jax.ShapeDtypeStruct((bsz, seq, d), BF16),
        scratch_shapes=[pltpu.VMEM((HEAD_DIM, HEAD_DIM), F32)],
        compiler_params=_params(("parallel", "parallel")),
        name="hgrn2",
    )(proj, proj, proj, proj, lb_logits, _gain3(head_gains))


def _dcum_kernel(u_ref, wf_ref, bf_ref, dcol_ref, drow_ref, *, blk, heads):
    seq = u_ref.shape[0]
    row = lax.broadcasted_iota(jnp.int32, (blk, blk), 0)
    col = lax.broadcasted_iota(jnp.int32, (blk, blk), 1)
    tri = jnp.where(row >= col, 1.0, 0.0).astype(BF16)
    w_f = wf_ref[...].astype(BF16)
    carry = jnp.zeros((1, wf_ref.shape[1]), F32)
    for n in range(seq // blk):
        rows = slice(n * blk, (n + 1) * blk)
        logit = jnp.dot(u_ref[rows, :], w_f, preferred_element_type=F32) + bf_ref[...]
        logsig = jnp.minimum(logit, 0.0) - jnp.log(1.0 + jnp.exp(-jnp.abs(logit)))
        c = _tri_cumsum(tri, logsig) + carry
        dcol_ref[rows, :] = c
        carry = c[blk - 1:blk, :]
    drow_ref[...] = dcol_ref[...].T[:heads, :]


def forget_cumsum(u, w_f, b_f, *, bsz, heads, blk):
    t, d = u.shape
    seq = t // bsz
    lanes = w_f.shape[1]
    return pl.pallas_call(
        functools.partial(_dcum_kernel, blk=blk, heads=heads),
        grid=(bsz,),
        in_specs=[
            pl.BlockSpec((seq, d), lambda b: (b, 0)),
            pl.BlockSpec((d, lanes), lambda b: (0, 0)),
            pl.BlockSpec((1, lanes), lambda b: (0, 0)),
        ],
        out_specs=[
            pl.BlockSpec((None, seq, lanes), lambda b: (b, 0, 0)),
            pl.BlockSpec((None, heads, seq), lambda b: (b, 0, 0)),
        ],
        out_shape=[
            jax.ShapeDtypeStruct((bsz, seq, lanes), F32),
            jax.ShapeDtypeStruct((bsz, heads, seq), F32),
        ],
        compiler_params=_params(("parallel",)),
        name="forget_cumsum",
    )(u, w_f, b_f)


def _fox_kernel(q_ref, k_ref, v_ref, dcol_ref, drow_ref, o_ref, *, tq):
    seq = q_ref.shape[0]
    head = pl.program_id(1)
    lane = lax.broadcasted_iota(jnp.int32, (tq, dcol_ref.shape[1]), 1)
    row = lax.broadcasted_iota(jnp.int32, (tq, tq), 0)
    col = lax.broadcasted_iota(jnp.int32, (tq, tq), 1)
    causal = row >= col
    nt = (((1,), (1,)), ((), ()))
    for qi in range(seq // tq):
        rows = slice(qi * tq, (qi + 1) * tq)
        q = q_ref[rows, :]
        dq = jnp.sum(jnp.where(lane == head, dcol_ref[rows, :], 0.0), axis=1, keepdims=True)
        m = jnp.full((tq, 1), -jnp.inf, F32)
        l = jnp.zeros((tq, 1), F32)
        acc = jnp.zeros((tq, HEAD_DIM), F32)
        for ki in range(qi + 1):
            cols = slice(ki * tq, (ki + 1) * tq)
            s = lax.dot_general(q, k_ref[cols, :], nt, preferred_element_type=F32)
            s = s + dq - drow_ref[:, cols]
            if ki == qi:
                s = jnp.where(causal, s, -jnp.inf)
            m_new = jnp.maximum(m, jnp.max(s, axis=1, keepdims=True))
            alpha = jnp.exp(m - m_new)
            prob = jnp.exp(s - m_new)
            l = alpha * l + jnp.sum(prob, axis=1, keepdims=True)
            acc = alpha * acc + jnp.dot(prob.astype(BF16), v_ref[cols, :], preferred_element_type=F32)
            m = m_new
        o_ref[rows, :] = (acc / l).astype(o_ref.dtype)


def fox_attention(q, kv, dcol, drow, *, tq):
    bsz, seq, d = q.shape
    heads = d // HEAD_DIM
    return pl.pallas_call(
        functools.partial(_fox_kernel, tq=tq),
        grid=(bsz, heads),
        in_specs=[
            pl.BlockSpec((None, seq, HEAD_DIM), lambda b, h: (b, 0, h)),
            pl.BlockSpec((None, seq, HEAD_DIM), lambda b, h: (b, 0, h)),
            pl.BlockSpec((None, seq, HEAD_DIM), lambda b, h: (b, 0, heads + h)),
            pl.BlockSpec((None, seq, dcol.shape[2]), lambda b, h: (b, 0, 0)),
            pl.BlockSpec((None, 1, seq), lambda b, h: (b * heads + h, 0, 0)),
        ],
        out_specs=pl.BlockSpec((None, seq, HEAD_DIM), lambda b, h: (b, 0, h)),
        out_shape=jax.ShapeDtypeStruct((bsz, seq, d), BF16),
        compiler_params=_params(("parallel", "parallel")),
        name="fox_attention",
    )(q, kv, kv, dcol, drow)


def kernel(x, p, mix_norm, mlp_norm, ple_norm, w_a_in, a_lb_logits, a_head_gain, w_a_out,
           kv_norm, w_kvf, b_f, w_b_q, w_b_out, w_mlp_up, w_mlp_down, w_ple_gate, w_ple_up,
           final_norm):
    bsz, seq, d = x.shape
    depth = p.shape[0]
    n_a = w_a_in.shape[0]
    t = bsz * seq
    heads = d // HEAD_DIM
    assert depth == n_a + w_b_q.shape[0] and n_a >= 1 and depth > n_a
    p3 = p.reshape(depth, t, p.shape[3])
    mix3, kv3, fin3 = _gain3(mix_norm), _gain3(kv_norm.reshape(1, d)), _gain3(final_norm.reshape(1, d))

    h = x.reshape(t, d)
    u = None
    kv = dcol = drow = None
    out = None
    for layer in range(depth):
        if layer < n_a:
            if layer > 0:
                raise NotImplementedError("an HGRN2 layer is only supported as the first layer")
            proj = norm_matmul(h, mix_norm, w_a_in, layer, out_dtype=F32, **TILES["in_proj"])
            mixed = hgrn2(proj.reshape(bsz, seq, 4 * d), a_lb_logits, a_head_gain, layer,
                          **TILES["hgrn"]).reshape(t, d)
            w_out, j = w_a_out, layer
        else:
            j = layer - n_a
            if u is None:
                raise NotImplementedError("attention layers need a preceding HGRN2 layer")
            q = matmul(u, w_b_q, j, d, out_dtype=BF16, scale=HEAD_DIM ** -0.5, **TILES["proj"])
            mixed = fox_attention(q.reshape(bsz, seq, d), kv, dcol, drow, **TILES["attn"]).reshape(t, d)
            w_out = w_b_out
        h, u = matmul_residual_norm(mixed, w_out, j, h, mlp_norm, layer, **TILES["out_proj"])
        h, u = mlp_residual_norm(u, w_mlp_up, w_mlp_down, layer, h, ple_norm, **TILES["mlp"])
        ple = functools.partial(ple_residual_norm, u, p3, w_ple_gate, w_ple_up, layer, h, **TILES["ple"])
        if layer == depth - 1:
            (out,) = ple([(fin3, 0)], [x.dtype], emit_h=False)
        elif layer == n_a - 1:
            h, u_kv, u = ple([(kv3, 0), (mix3, layer + 1)], [BF16, BF16], emit_h=True)
            kv = matmul(u_kv, w_kvf, 0, 2 * d, out_dtype=BF16, **TILES["proj"]).reshape(bsz, seq, 2 * d)
            w_f = jnp.pad(w_kvf[:, 2 * d:], ((0, 0), (0, LANES - heads)))
            b_fp = jnp.pad(b_f.astype(F32), (0, LANES - heads)).reshape(1, LANES)
            dcol, drow = forget_cumsum(u_kv, w_f, b_fp, bsz=bsz, heads=heads, **TILES["dcum"])
            drow = drow.reshape(bsz * heads, 1, seq)
        else:
            h, u = ple([(mix3, layer + 1)], [BF16], emit_h=True)
    return out.reshape(bsz, seq, d)
```

```python
import functools

import jax
import jax.numpy as jnp
from jax import lax
from jax.experimental import pallas as pl
from jax.experimental.pallas import tpu as pltpu

F32 = jnp.float32
BF16 = jnp.bfloat16
EPS = 1e-6

HEAD_DIM = 128
LANES = 128
V7X_VMEM_LIMIT = 56 * 1024 * 1024
NORM_ROWS = 256

TILES = dict(
    in_proj=dict(tm=2048, tn=512),
    kvq=dict(tm=1024, tn=512),
    out_proj=dict(tm=2048, tn=256),
    mlp_ple=dict(tm=1024, tf=512, tn=256),
    hgrn=dict(chunk=64, unroll=8),
    dcum=dict(blk=256),
    attn=dict(tq=512),
)


def _params(sem):
    return pltpu.CompilerParams(dimension_semantics=sem, vmem_limit_bytes=V7X_VMEM_LIMIT)


def _once(block_shape, index_map):
    return pl.BlockSpec(block_shape, index_map, pipeline_mode=pl.Buffered(1))


def _sigmoid(x):
    return 1.0 / (1.0 + jnp.exp(-x))


def _norm_rows(h_ref, gains, out_refs):
    rows = h_ref.shape[0]
    rc = min(NORM_ROWS, rows)

    def body(r, carry):
        r0 = pl.multiple_of(r * rc, rc)
        hb = h_ref[pl.ds(r0, rc), :].astype(F32)
        y = hb * lax.rsqrt(jnp.mean(hb * hb, axis=-1, keepdims=True) + EPS)
        for g, o_ref in zip(gains, out_refs):
            o_ref[pl.ds(r0, rc), :] = (y * g).astype(o_ref.dtype)
        return carry

    lax.fori_loop(0, rows // rc, body, 0)


def _dot(a, w):
    return jnp.dot(a, w.astype(BF16), preferred_element_type=F32)


def _gain3(g):
    return g.reshape(g.shape[0], 1, g.shape[1])


def _gain_spec(layer, d):
    return pl.BlockSpec((None, 1, d), lambda i, j: (layer, 0, 0))


def _norm_mm_kernel(x_ref, g_ref, w_ref, o_ref, xn_ref):
    @pl.when(pl.program_id(1) == 0)
    def _():
        _norm_rows(x_ref, [g_ref[...]], [xn_ref])

    o_ref[...] = _dot(xn_ref[...], w_ref[...]).astype(o_ref.dtype)


def norm_matmul(x, gains, w, layer, *, tm, tn, out_dtype):
    m, k = x.shape
    n = w.shape[2]
    return pl.pallas_call(
        _norm_mm_kernel,
        grid=(m // tm, n // tn),
        in_specs=[
            _once((tm, k), lambda i, j: (i, 0)),
            _gain_spec(layer, k),
            pl.BlockSpec((None, k, tn), lambda i, j: (layer, 0, j)),
        ],
        out_specs=pl.BlockSpec((tm, tn), lambda i, j: (i, j)),
        out_shape=jax.ShapeDtypeStruct((m, n), out_dtype),
        scratch_shapes=[pltpu.VMEM((tm, k), BF16)],
        compiler_params=_params(("parallel", "arbitrary")),
        name="norm_matmul",
    )(x, _gain3(gains), w)


def _kvq_kernel(h_ref, gkv_ref, gq_ref, wkv_ref, wq_ref, wf_ref, o_ref, lg_ref, u_ref, *, n_kv, q_scale):
    j = pl.program_id(1)

    @pl.when(j == 0)
    def _():
        _norm_rows(h_ref, [gkv_ref[...], gq_ref[...]], [u_ref.at[0], u_ref.at[1]])
        lg_ref[...] = _dot(u_ref[0], wf_ref[...])

    @pl.when(j < n_kv)
    def _():
        o_ref[...] = _dot(u_ref[0], wkv_ref[...]).astype(o_ref.dtype)

    @pl.when(j >= n_kv)
    def _():
        o_ref[...] = (_dot(u_ref[1], wq_ref[...]) * q_scale).astype(o_ref.dtype)


def kvq_projection(h, g_kv, g_q, q_layer, w_kvf, w_q, wq_layer, w_f, *, n_kv_cols, q_scale, tm, tn):
    m, d = h.shape
    nq_cols = w_q.shape[2]
    n_kv, n_q = n_kv_cols // tn, nq_cols // tn
    lanes = w_f.shape[1]
    return pl.pallas_call(
        functools.partial(_kvq_kernel, n_kv=n_kv, q_scale=q_scale),
        grid=(m // tm, n_kv + n_q),
        in_specs=[
            pl.BlockSpec((tm, d), lambda i, j: (i, 0)),
            _gain_spec(0, d),
            _gain_spec(q_layer, d),
            pl.BlockSpec((d, tn), lambda i, j: (0, jnp.minimum(j, n_kv - 1))),
            pl.BlockSpec((None, d, tn), lambda i, j: (wq_layer, 0, jnp.clip(j - n_kv, 0, n_q - 1))),
            pl.BlockSpec((d, lanes), lambda i, j: (0, 0)),
        ],
        out_specs=[
            pl.BlockSpec((tm, tn), lambda i, j: (i, j)),
            pl.BlockSpec((tm, lanes), lambda i, j: (i, 0)),
        ],
        out_shape=[
            jax.ShapeDtypeStruct((m, n_kv_cols + nq_cols), BF16),
            jax.ShapeDtypeStruct((m, lanes), F32),
        ],
        scratch_shapes=[pltpu.VMEM((2, tm, d), BF16)],
        compiler_params=_params(("parallel", "arbitrary")),
        name="kvq_projection",
    )(h, g_kv, g_q, w_kvf, w_q, w_f)


def _mm_res_norm_kernel(a_ref, w_ref, x_ref, g_ref, h_ref, u_ref, *, tn):
    j = pl.program_id(1)
    c0 = pl.multiple_of(j * tn, tn)
    h_ref[:, pl.ds(c0, tn)] = x_ref[...] + _dot(a_ref[...], w_ref[...])

    @pl.when(j == pl.num_programs(1) - 1)
    def _():
        _norm_rows(h_ref, [g_ref[...]], [u_ref])


def matmul_residual_norm(a, w, layer, x, gains, g_layer, *, tm, tn):
    m, k = a.shape
    n = w.shape[2]
    return pl.pallas_call(
        functools.partial(_mm_res_norm_kernel, tn=tn),
        grid=(m // tm, n // tn),
        in_specs=[
            _once((tm, k), lambda i, j: (i, 0)),
            pl.BlockSpec((None, k, tn), lambda i, j: (layer, 0, j)),
            pl.BlockSpec((tm, tn), lambda i, j: (i, j)),
            _gain_spec(g_layer, n),
        ],
        out_specs=[
            _once((tm, n), lambda i, j: (i, 0)),
            _once((tm, n), lambda i, j: (i, 0)),
        ],
        out_shape=[
            jax.ShapeDtypeStruct((m, n), F32),
            jax.ShapeDtypeStruct((m, n), BF16),
        ],
        compiler_params=_params(("parallel", "arbitrary")),
        name="matmul_residual_norm",
    )(a, w, x, _gain3(gains))


def _mlp_ple_kernel(u_ref, wup_ref, wdn_ref, h_ref, gp_ref, p_ref, wg_ref, wu_ref, *rest, nf, cn, tn, final):
    if final:
        gf_ref, o_ref, u2_ref = rest
    else:
        o_ref, u2_ref = rest
    s = pl.program_id(1)
    d = o_ref.shape[1]
    hw = h_ref.shape[1]

    @pl.when(s == 0)
    def _():
        o_ref[...] = jnp.zeros_like(o_ref)

    @pl.when(s < nf)
    def _():
        hid = jnp.maximum(_dot(u_ref[...], wup_ref[...]), 0.0)
        hid = (hid * hid).astype(BF16)
        for n in range(d // cn):
            cs = slice(n * cn, (n + 1) * cn)
            o_ref[:, cs] += _dot(hid, wdn_ref[:, cs])

    @pl.when(s < d // hw)
    def _():
        c0 = pl.multiple_of(s * hw, hw)
        o_ref[:, pl.ds(c0, hw)] += h_ref[...]

    @pl.when(s == nf - 1)
    def _():
        _norm_rows(o_ref, [gp_ref[...]], [u2_ref])

    @pl.when(s >= nf)
    def _():
        c0 = pl.multiple_of((s - nf) * tn, tn)
        gate = _sigmoid(_dot(u2_ref[...], wg_ref[...]))
        up = _dot(p_ref[...].astype(BF16), wu_ref[...])
        o_ref[:, pl.ds(c0, tn)] += up * gate

    if final:
        @pl.when(s == pl.num_programs(1) - 1)
        def _():
            _norm_rows(o_ref, [gf_ref[...]], [o_ref])


def mlp_ple(u, w_up, w_down, layer, h, g_ple, p, w_gate, w_ple_up, g_final=None, *, tm, tf, tn):
    m, d = u.shape
    dff = w_up.shape[2]
    ple_dim = p.shape[2]
    nf, n_p = dff // tf, d // tn
    hw = max(LANES, d // nf)
    n_pieces = d // hw
    assert n_pieces <= nf
    final = g_final is not None
    ple_col = lambda i, s: (layer, 0, jnp.clip(s - nf, 0, n_p - 1))
    in_specs = [
        _once((tm, d), lambda i, s: (i, 0)),
        pl.BlockSpec((None, d, tf), lambda i, s: (layer, 0, jnp.minimum(s, nf - 1))),
        pl.BlockSpec((None, tf, d), lambda i, s: (layer, jnp.minimum(s, nf - 1), 0)),
        pl.BlockSpec((tm, hw), lambda i, s: (i, jnp.minimum(s, n_pieces - 1))),
        _gain_spec(layer, d),
        _once((None, tm, ple_dim), lambda i, s: (layer, i, 0)),
        pl.BlockSpec((None, d, tn), ple_col),
        pl.BlockSpec((None, ple_dim, tn), ple_col),
    ]
    args = [u, w_up, w_down, h, _gain3(g_ple), p, w_gate, w_ple_up]
    if final:
        in_specs.append(_gain_spec(0, d))
        args.append(g_final)
    return pl.pallas_call(
        functools.partial(_mlp_ple_kernel, nf=nf, cn=512, tn=tn, final=final),
        grid=(m // tm, nf + n_p),
        in_specs=in_specs,
        out_specs=_once((tm, d), lambda i, s: (i, 0)),
        out_shape=jax.ShapeDtypeStruct((m, d), F32),
        scratch_shapes=[pltpu.VMEM((tm, d), BF16)],
        compiler_params=_params(("parallel", "arbitrary")),
        name="mlp_ple",
    )(*args)


def _split3(x):
    hi = x.astype(BF16)
    r1 = x - hi.astype(F32)
    mid = r1.astype(BF16)
    lo = (r1 - mid.astype(F32)).astype(BF16)
    return hi, mid, lo


def _tri_cumsum(tri, x):
    hi, mid, lo = _split3(x)
    dot = lambda t: jnp.dot(tri, t, preferred_element_type=F32)
    return dot(hi) + dot(mid) + dot(lo)


def _hgrn_kernel(q_ref, f_ref, i_ref, g_ref, lbl_ref, hg_ref, o_ref, st_ref, *, chunk, unroll, layer):
    seq = q_ref.shape[0]
    blk = chunk * unroll
    mid = chunk // 2
    lg = lbl_ref[...]
    e = jnp.exp(lg - jnp.max(lg, axis=0, keepdims=True))
    sm = e / jnp.sum(e, axis=0, keepdims=True)
    lb = jnp.sum(sm[:layer + 1], axis=0, keepdims=True)
    head_gain = hg_ref[...]
    row = lax.broadcasted_iota(jnp.int32, (chunk, chunk), 0)
    col = lax.broadcasted_iota(jnp.int32, (chunk, chunk), 1)
    causal = row >= col
    tri = jnp.where(causal, 1.0, 0.0).astype(BF16)
    st_ref[...] = jnp.zeros_like(st_ref)
    nt = (((1,), (1,)), ((), ()))
    tn = (((0,), (0,)), ((), ()))

    def body(c, carry):
        r0 = pl.multiple_of(c * blk, blk)
        rows = pl.ds(r0, blk)
        fg = lb + (1.0 - lb) * _sigmoid(f_ref[rows, :])
        k_all = 1.0 - fg
        logf = jnp.log(fg)
        qv = q_ref[rows, :]
        qc = qv * _sigmoid(qv) * (HEAD_DIM ** -0.5)
        v_all = i_ref[rows, :].astype(BF16)
        parts = []
        for u in range(unroll):
            sl = slice(u * chunk, (u + 1) * chunk)
            b = _tri_cumsum(tri, logf[sl])
            b_mid = b[mid - 1:mid, :]
            b_last = b[chunk - 1:chunk, :]
            q_mid = (qc[sl] * jnp.exp(b - b_mid)).astype(BF16)
            k_mid = (k_all[sl] * jnp.exp(b_mid - b)).astype(BF16)
            q_in = (qc[sl] * jnp.exp(b)).astype(BF16)
            k_end = (k_all[sl] * jnp.exp(b_last - b)).astype(BF16)
            att = lax.dot_general(q_mid, k_mid, nt, preferred_element_type=F32)
            att = jnp.where(causal, att, 0.0).astype(BF16)
            o_intra = jnp.dot(att, v_all[sl], preferred_element_type=F32)
            kv = lax.dot_general(v_all[sl], k_end, tn, preferred_element_type=F32)
            parts.append((q_in, o_intra, kv, jnp.exp(b_last)))
        st = st_ref[...]
        outs = []
        for q_in, o_intra, kv, dec in parts:
            outs.append(o_intra + lax.dot_general(q_in, st.astype(BF16), nt, preferred_element_type=F32))
            st = dec * st + kv
        st_ref[...] = st
        o = jnp.concatenate(outs, axis=0)
        o = o * lax.rsqrt(jnp.mean(o * o, axis=-1, keepdims=True) + EPS) * head_gain
        gv = g_ref[rows, :]
        o_ref[rows, :] = (o * (gv * _sigmoid(gv))).astype(o_ref.dtype)
        return carry

    lax.fori_loop(0, seq // blk, body, 0)


def hgrn2(proj, lb_logits, head_gains, layer, *, chunk, unroll):
    bsz, seq, d4 = proj.shape
    d = d4 // 4
    heads = d // HEAD_DIM
    nl = lb_logits.shape[0]
    col = lambda part: pl.BlockSpec((None, seq, HEAD_DIM), lambda b, h: (b, 0, part * heads + h))
    return pl.pallas_call(
        functools.partial(_hgrn_kernel, chunk=chunk, unroll=unroll, layer=layer),
        grid=(bsz, heads),
        in_specs=[
            col(0), col(1), col(2), col(3),
            pl.BlockSpec((nl, HEAD_DIM), lambda b, h: (0, h)),
            pl.BlockSpec((None, 1, HEAD_DIM), lambda b, h: (layer, 0, 0)),
        ],
        out_specs=pl.BlockSpec((None, seq, HEAD_DIM), lambda b, h: (b, 0, h)),
        out_shape=jax.ShapeDtypeStruct((bsz, seq, d), BF16),
        scratch_shapes=[pltpu.VMEM((HEAD_DIM, HEAD_DIM), F32)],
        compiler_params=_params(("parallel", "parallel")),
        name="hgrn2",
    )(proj, proj, proj, proj, lb_logits, _gain3(head_gains))


def _dcum_kernel(lg_ref, bf_ref, dcol_ref, drow_ref, *, blk, heads):
    seq = lg_ref.shape[0]
    row = lax.broadcasted_iota(jnp.int32, (blk, blk), 0)
    col = lax.broadcasted_iota(jnp.int32, (blk, blk), 1)
    tri = jnp.where(row >= col, 1.0, 0.0).astype(BF16)
    carry = jnp.zeros((1, lg_ref.shape[1]), F32)
    for n in range(seq // blk):
        rows = slice(n * blk, (n + 1) * blk)
        logit = lg_ref[rows, :] + bf_ref[...]
        logsig = jnp.minimum(logit, 0.0) - jnp.log(1.0 + jnp.exp(-jnp.abs(logit)))
        c = _tri_cumsum(tri, logsig) + carry
        dcol_ref[rows, :] = c
        carry = c[blk - 1:blk, :]
    drow_ref[...] = dcol_ref[...].T[:heads, :]


def forget_cumsum(logits, b_f, *, heads, blk):
    bsz, seq, lanes = logits.shape
    return pl.pallas_call(
        functools.partial(_dcum_kernel, blk=blk, heads=heads),
        grid=(bsz,),
        in_specs=[
            pl.BlockSpec((None, seq, lanes), lambda b: (b, 0, 0)),
            pl.BlockSpec((1, lanes), lambda b: (0, 0)),
        ],
        out_specs=[
            pl.BlockSpec((None, seq, lanes), lambda b: (b, 0, 0)),
            pl.BlockSpec((None, heads, seq), lambda b: (b, 0, 0)),
        ],
        out_shape=[
            jax.ShapeDtypeStruct((bsz, seq, lanes), F32),
            jax.ShapeDtypeStruct((bsz, heads, seq), F32),
        ],
        compiler_params=_params(("parallel",)),
        name="forget_cumsum",
    )(logits, b_f)


def _fox_kernel(q_ref, k_ref, v_ref, dcol_ref, drow_ref, o_ref, *, tq):
    seq = q_ref.shape[0]
    head = pl.program_id(1)
    lane = lax.broadcasted_iota(jnp.int32, (tq, dcol_ref.shape[1]), 1)
    row = lax.broadcasted_iota(jnp.int32, (tq, tq), 0)
    col = lax.broadcasted_iota(jnp.int32, (tq, tq), 1)
    causal = row >= col
    nt = (((1,), (1,)), ((), ()))
    for qi in range(seq // tq):
        rows = slice(qi * tq, (qi + 1) * tq)
        q = q_ref[rows, :]
        dq = jnp.sum(jnp.where(lane == head, dcol_ref[rows, :], 0.0), axis=1, keepdims=True)
        m = jnp.full((tq, 1), -jnp.inf, F32)
        l = jnp.zeros((tq, 1), F32)
        acc = jnp.zeros((tq, HEAD_DIM), F32)
        for ki in range(qi + 1):
            cols = slice(ki * tq, (ki + 1) * tq)
            s = lax.dot_general(q, k_ref[cols, :], nt, preferred_element_type=F32)
            s = s + dq - drow_ref[:, cols]
            if ki == qi:
                s = jnp.where(causal, s, -jnp.inf)
            m_new = jnp.maximum(m, jnp.max(s, axis=1, keepdims=True))
            alpha = jnp.exp(m - m_new)
            prob = jnp.exp(s - m_new)
            l = alpha * l + jnp.sum(prob, axis=1, keepdims=True)
            acc = alpha * acc + jnp.dot(prob.astype(BF16), v_ref[cols, :], preferred_element_type=F32)
            m = m_new
        o_ref[rows, :] = (acc / l).astype(o_ref.dtype)


def fox_attention(kvq, dcol, drow, *, tq):
    bsz, seq, d3 = kvq.shape
    d = d3 // 3
    heads = d // HEAD_DIM
    col = lambda part: pl.BlockSpec((None, seq, HEAD_DIM), lambda b, h: (b, 0, part * heads + h))
    return pl.pallas_call(
        functools.partial(_fox_kernel, tq=tq),
        grid=(bsz, heads),
        in_specs=[
            col(2), col(0), col(1),
            pl.BlockSpec((None, seq, dcol.shape[2]), lambda b, h: (b, 0, 0)),
            pl.BlockSpec((None, 1, seq), lambda b, h: (b * heads + h, 0, 0)),
        ],
        out_specs=pl.BlockSpec((None, seq, HEAD_DIM), lambda b, h: (b, 0, h)),
        out_shape=jax.ShapeDtypeStruct((bsz, seq, d), BF16),
        compiler_params=_params(("parallel", "parallel")),
        name="fox_attention",
    )(kvq, kvq, kvq, dcol, drow)


def kernel(x, p, mix_norm, mlp_norm, ple_norm, w_a_in, a_lb_logits, a_head_gain, w_a_out,
           kv_norm, w_kvf, b_f, w_b_q, w_b_out, w_mlp_up, w_mlp_down, w_ple_gate, w_ple_up,
           final_norm):
    bsz, seq, d = x.shape
    depth = p.shape[0]
    t = bsz * seq
    heads = d // HEAD_DIM
    assert depth == 2 and w_a_in.shape[0] == 1 and w_b_q.shape[0] == 1
    p3 = p.reshape(depth, t, p.shape[3])
    mix3 = _gain3(mix_norm)
    kv3 = kv_norm.reshape(1, 1, d)
    fin3 = final_norm.reshape(1, 1, d)
    tail = functools.partial(mlp_ple, **TILES["mlp_ple"])

    h = x.reshape(t, d)
    proj = norm_matmul(h, mix_norm, w_a_in, 0, out_dtype=F32, **TILES["in_proj"])
    mixed = hgrn2(proj.reshape(bsz, seq, 4 * d), a_lb_logits, a_head_gain, 0, **TILES["hgrn"])
    h, u = matmul_residual_norm(mixed.reshape(t, d), w_a_out, 0, h, mlp_norm, 0, **TILES["out_proj"])
    h = tail(u, w_mlp_up, w_mlp_down, 0, h, ple_norm, p3, w_ple_gate, w_ple_up)

    w_f = jnp.pad(w_kvf[:, 2 * d:], ((0, 0), (0, LANES - heads)))
    b_fp = jnp.pad(b_f.astype(F32), (0, LANES - heads)).reshape(1, LANES)
    kvq, logits = kvq_projection(h, kv3, mix3, 1, w_kvf, w_b_q, 0, w_f, n_kv_cols=2 * d,
                                 q_scale=HEAD_DIM ** -0.5, **TILES["kvq"])
    dcol, drow = forget_cumsum(logits.reshape(bsz, seq, LANES), b_fp, heads=heads, **TILES["dcum"])
    mixed = fox_attention(kvq.reshape(bsz, seq, 3 * d), dcol, drow.reshape(bsz * heads, 1, seq),
                          **TILES["attn"])
    h, u = matmul_residual_norm(mixed.reshape(t, d), w_b_out, 0, h, mlp_norm, 1, **TILES["out_proj"])
    out = tail(u, w_mlp_up, w_mlp_down, 1, h, ple_norm, p3, w_ple_gate, w_ple_up, g_final=fin3)
    return out.reshape(bsz, seq, d)
```

```python
import functools

import jax
import jax.numpy as jnp
from jax import lax
from jax.experimental import pallas as pl
from jax.experimental.pallas import tpu as pltpu

F32 = jnp.float32
BF16 = jnp.bfloat16
EPS = 1e-6

LOG2E = 1.4426950408889634
HEAD_DIM = 128
LANES = 128
V7X_VMEM_LIMIT = 56 * 1024 * 1024
NORM_ROWS = 256

TILES = dict(
    in_proj=dict(tm=2048, tn=512),
    kvq=dict(tm=1024, tn=512),
    out_proj=dict(tm=2048, tn=512),
    mlp_ple=dict(tm=1024, tf=512, tn=256),
    hgrn=dict(chunk=64, unroll=8),
    dcum=dict(blk=256),
    attn=dict(tq=512),
)


def _params(sem):
    return pltpu.CompilerParams(dimension_semantics=sem, vmem_limit_bytes=V7X_VMEM_LIMIT)


def _once(block_shape, index_map):
    return pl.BlockSpec(block_shape, index_map, pipeline_mode=pl.Buffered(1))


def _sigmoid(x):
    return 1.0 / (1.0 + jnp.exp(-x))


def _norm_rows(h_ref, gains, out_refs):
    rows = h_ref.shape[0]
    rc = min(NORM_ROWS, rows)

    def body(r, carry):
        r0 = pl.multiple_of(r * rc, rc)
        hb = h_ref[pl.ds(r0, rc), :].astype(F32)
        y = hb * lax.rsqrt(jnp.mean(hb * hb, axis=-1, keepdims=True) + EPS)
        for g, o_ref in zip(gains, out_refs):
            o_ref[pl.ds(r0, rc), :] = (y * g).astype(o_ref.dtype)
        return carry

    lax.fori_loop(0, rows // rc, body, 0)


def _dot(a, w):
    return jnp.dot(a, w.astype(BF16), preferred_element_type=F32)


def _gain3(g):
    return g.reshape(g.shape[0], 1, g.shape[1])


def _gain_spec(layer, d):
    return pl.BlockSpec((None, 1, d), lambda i, j: (layer, 0, 0))


def _norm_mm_kernel(x_ref, g_ref, w_ref, o_ref, xn_ref):
    @pl.when(pl.program_id(1) == 0)
    def _():
        _norm_rows(x_ref, [g_ref[...]], [xn_ref])

    o_ref[...] = _dot(xn_ref[...], w_ref[...]).astype(o_ref.dtype)


def norm_matmul(x, gains, w, layer, *, tm, tn, out_dtype):
    m, k = x.shape
    n = w.shape[2]
    return pl.pallas_call(
        _norm_mm_kernel,
        grid=(m // tm, n // tn),
        in_specs=[
            _once((tm, k), lambda i, j: (i, 0)),
            _gain_spec(layer, k),
            pl.BlockSpec((None, k, tn), lambda i, j: (layer, 0, j)),
        ],
        out_specs=pl.BlockSpec((tm, tn), lambda i, j: (i, j)),
        out_shape=jax.ShapeDtypeStruct((m, n), out_dtype),
        scratch_shapes=[pltpu.VMEM((tm, k), BF16)],
        compiler_params=_params(("parallel", "arbitrary")),
        name="norm_matmul",
    )(x, _gain3(gains), w)


def _kvq_kernel(h_ref, gkv_ref, gq_ref, wkv_ref, wq_ref, wf_ref, o_ref, lg_ref, u_ref, *, n_kv, q_scale):
    j = pl.program_id(1)

    nt = (((1,), (1,)), ((), ()))

    @pl.when(j == 0)
    def _():
        _norm_rows(h_ref, [gkv_ref[...], gq_ref[...]], [u_ref.at[0], u_ref.at[1]])
        lg_ref[...] = lax.dot_general(u_ref[0], wf_ref[...].astype(BF16), nt, preferred_element_type=F32)

    @pl.when(j < n_kv)
    def _():
        acc = lax.dot_general(u_ref[0], wkv_ref[...].astype(BF16), nt, preferred_element_type=F32)
        o_ref[...] = acc.astype(o_ref.dtype)

    @pl.when(j >= n_kv)
    def _():
        o_ref[...] = (_dot(u_ref[1], wq_ref[...]) * q_scale).astype(o_ref.dtype)


def kvq_projection(h, g_kv, g_q, q_layer, w_kvf_t, w_q, wq_layer, w_f_t, *, n_kv_cols, q_scale, tm, tn):
    m, d = h.shape
    nq_cols = w_q.shape[2]
    n_kv, n_q = n_kv_cols // tn, nq_cols // tn
    lanes = w_f_t.shape[0]
    return pl.pallas_call(
        functools.partial(_kvq_kernel, n_kv=n_kv, q_scale=q_scale),
        grid=(m // tm, n_kv + n_q),
        in_specs=[
            pl.BlockSpec((tm, d), lambda i, j: (i, 0)),
            _gain_spec(0, d),
            _gain_spec(q_layer, d),
            pl.BlockSpec((tn, d), lambda i, j: (jnp.minimum(j, n_kv - 1), 0)),
            pl.BlockSpec((None, d, tn), lambda i, j: (wq_layer, 0, jnp.clip(j - n_kv, 0, n_q - 1))),
            pl.BlockSpec((lanes, d), lambda i, j: (0, 0)),
        ],
        out_specs=[
            pl.BlockSpec((tm, tn), lambda i, j: (i, j)),
            pl.BlockSpec((tm, lanes), lambda i, j: (i, 0)),
        ],
        out_shape=[
            jax.ShapeDtypeStruct((m, n_kv_cols + nq_cols), BF16),
            jax.ShapeDtypeStruct((m, lanes), F32),
        ],
        scratch_shapes=[pltpu.VMEM((2, tm, d), BF16)],
        compiler_params=_params(("parallel", "arbitrary")),
        name="kvq_projection",
    )(h, g_kv, g_q, w_kvf_t, w_q, w_f_t)


def _mm_res_kernel(a_ref, w_ref, x_ref, g_ref, h_ref, hg_ref, ss_ref):
    h = x_ref[...] + _dot(a_ref[...], w_ref[...])
    h_ref[...] = h
    hg_ref[...] = (h * g_ref[...]).astype(hg_ref.dtype)
    sq = h * h
    part = sq[:, :LANES]
    for n in range(1, h.shape[1] // LANES):
        part = part + sq[:, n * LANES:(n + 1) * LANES]
    ss_ref[...] = part


def matmul_residual(a, w, layer, x, gains, g_layer, *, tm, tn):
    m, k = a.shape
    n = w.shape[2]
    return pl.pallas_call(
        _mm_res_kernel,
        grid=(m // tm, n // tn),
        in_specs=[
            pl.BlockSpec((tm, k), lambda i, j: (i, 0)),
            pl.BlockSpec((None, k, tn), lambda i, j: (layer, 0, j)),
            pl.BlockSpec((tm, tn), lambda i, j: (i, j)),
            pl.BlockSpec((None, 1, tn), lambda i, j: (g_layer, 0, j)),
        ],
        out_specs=[
            pl.BlockSpec((tm, tn), lambda i, j: (i, j)),
            pl.BlockSpec((tm, tn), lambda i, j: (i, j)),
            pl.BlockSpec((tm, LANES), lambda i, j: (i, j)),
        ],
        out_shape=[
            jax.ShapeDtypeStruct((m, n), F32),
            jax.ShapeDtypeStruct((m, n), BF16),
            jax.ShapeDtypeStruct((m, (n // tn) * LANES), F32),
        ],
        compiler_params=_params(("parallel", "arbitrary")),
        name="matmul_residual",
    )(a, w, x, _gain3(gains))


def _mlp_ple_kernel(u_ref, ss_ref, wup_ref, wdn_ref, h_ref, gp_ref, p_ref, wg_ref, wu_ref, *rest,
                    nf, cn, tn, final):
    if final:
        gf_ref, o_ref, u2_ref, r_ref = rest
    else:
        o_ref, u2_ref, r_ref = rest
    s = pl.program_id(1)
    d = o_ref.shape[1]
    hw = h_ref.shape[1]

    @pl.when(s == 0)
    def _():
        o_ref[...] = jnp.zeros_like(o_ref)
        r_ref[...] = lax.rsqrt(jnp.sum(ss_ref[...], axis=1, keepdims=True) * (1.0 / d) + EPS)

    @pl.when(s < nf)
    def _():
        hid = jnp.maximum(_dot(u_ref[...], wup_ref[...]) * r_ref[...], 0.0)
        hid = (hid * hid).astype(BF16)
        for n in range(d // cn):
            cs = slice(n * cn, (n + 1) * cn)
            o_ref[:, cs] += _dot(hid, wdn_ref[:, cs])

    @pl.when(s < d // hw)
    def _():
        c0 = pl.multiple_of(s * hw, hw)
        o_ref[:, pl.ds(c0, hw)] += h_ref[...]

    @pl.when(s == nf - 1)
    def _():
        _norm_rows(o_ref, [gp_ref[...]], [u2_ref])

    @pl.when(s >= nf)
    def _():
        c0 = pl.multiple_of((s - nf) * tn, tn)
        gate = _sigmoid(_dot(u2_ref[...], wg_ref[...]))
        up = _dot(p_ref[...].astype(BF16), wu_ref[...])
        o_ref[:, pl.ds(c0, tn)] += up * gate

    if final:
        @pl.when(s == pl.num_programs(1) - 1)
        def _():
            _norm_rows(o_ref, [gf_ref[...]], [o_ref])


def mlp_ple(u, ss, w_up, w_down, layer, h, g_ple, p, w_gate, w_ple_up, g_final=None, *, tm, tf, tn):
    m, d = u.shape
    dff = w_up.shape[2]
    ple_dim = p.shape[2]
    nf, n_p = dff // tf, d // tn
    hw = max(LANES, d // nf)
    n_pieces = d // hw
    assert n_pieces <= nf
    final = g_final is not None
    ple_col = lambda i, s: (layer, 0, jnp.clip(s - nf, 0, n_p - 1))
    in_specs = [
        _once((tm, d), lambda i, s: (i, 0)),
        pl.BlockSpec((tm, ss.shape[1]), lambda i, s: (i, 0)),
        pl.BlockSpec((None, d, tf), lambda i, s: (layer, 0, jnp.minimum(s, nf - 1))),
        pl.BlockSpec((None, tf, d), lambda i, s: (layer, jnp.minimum(s, nf - 1), 0)),
        pl.BlockSpec((tm, hw), lambda i, s: (i, jnp.minimum(s, n_pieces - 1))),
        _gain_spec(layer, d),
        _once((None, tm, ple_dim), lambda i, s: (layer, i, 0)),
        pl.BlockSpec((None, d, tn), ple_col),
        pl.BlockSpec((None, ple_dim, tn), ple_col),
    ]
    args = [u, ss, w_up, w_down, h, _gain3(g_ple), p, w_gate, w_ple_up]
    if final:
        in_specs.append(_gain_spec(0, d))
        args.append(g_final)
    return pl.pallas_call(
        functools.partial(_mlp_ple_kernel, nf=nf, cn=512, tn=tn, final=final),
        grid=(m // tm, nf + n_p),
        in_specs=in_specs,
        out_specs=_once((tm, d), lambda i, s: (i, 0)),
        out_shape=jax.ShapeDtypeStruct((m, d), F32),
        scratch_shapes=[pltpu.VMEM((tm, d), BF16), pltpu.VMEM((tm, 1), F32)],
        compiler_params=_params(("parallel", "arbitrary")),
        name="mlp_ple",
    )(*args)


def _tri_cumsum(tri, x, terms=3):
    total = None
    rest = x
    for n in range(terms):
        piece = rest.astype(BF16)
        if n + 1 < terms:
            rest = rest - piece.astype(F32)
        part = jnp.dot(tri, piece, preferred_element_type=F32)
        total = part if total is None else total + part
    return total


def _hgrn_kernel(q_ref, f_ref, i_ref, g_ref, lbl_ref, hg_ref, o_ref, st_ref, *, chunk, unroll, layer):
    seq = q_ref.shape[0]
    blk = chunk * unroll
    mid = chunk // 2
    lg = lbl_ref[...]
    e = jnp.exp(lg - jnp.max(lg, axis=0, keepdims=True))
    sm = e / jnp.sum(e, axis=0, keepdims=True)
    lb = jnp.sum(sm[:layer + 1], axis=0, keepdims=True)
    head_gain = hg_ref[...]
    row = lax.broadcasted_iota(jnp.int32, (chunk, chunk), 0)
    col = lax.broadcasted_iota(jnp.int32, (chunk, chunk), 1)
    causal = row >= col
    tri = jnp.where(causal, 1.0, 0.0).astype(BF16)
    st_ref[...] = jnp.zeros_like(st_ref)
    nt = (((1,), (1,)), ((), ()))
    tn = (((0,), (0,)), ((), ()))

    def body(c, carry):
        r0 = pl.multiple_of(c * blk, blk)
        rows = pl.ds(r0, blk)
        fg = lb + (1.0 - lb) * _sigmoid(f_ref[rows, :])
        k_all = 1.0 - fg
        log2f = jnp.log(fg) * LOG2E
        qv = q_ref[rows, :]
        qc = qv * _sigmoid(qv) * (HEAD_DIM ** -0.5)
        v_all = i_ref[rows, :].astype(BF16)
        parts = []
        for u in range(unroll):
            sl = slice(u * chunk, (u + 1) * chunk)
            b = _tri_cumsum(tri, log2f[sl], terms=3)
            b_mid = b[mid - 1:mid, :]
            b_last = b[chunk - 1:chunk, :]
            q_mid = qc[sl] * jnp.exp2(b - b_mid)
            k_mid = k_all[sl] * jnp.exp2(b_mid - b)
            q_in = (q_mid * jnp.exp2(b_mid)).astype(BF16)
            k_end = (k_mid * jnp.exp2(b_last - b_mid)).astype(BF16)
            att = lax.dot_general(q_mid.astype(BF16), k_mid.astype(BF16), nt, preferred_element_type=F32)
            att = jnp.where(causal, att, 0.0).astype(BF16)
            o_intra = jnp.dot(att, v_all[sl], preferred_element_type=F32)
            kv = lax.dot_general(v_all[sl], k_end, tn, preferred_element_type=F32)
            parts.append((q_in, o_intra, kv, jnp.exp2(b_last)))
        st = st_ref[...]
        outs = []
        for q_in, o_intra, kv, dec in parts:
            outs.append(o_intra + lax.dot_general(q_in, st.astype(BF16), nt, preferred_element_type=F32))
            st = dec * st + kv
        st_ref[...] = st
        o = jnp.concatenate(outs, axis=0)
        o = o * lax.rsqrt(jnp.mean(o * o, axis=-1, keepdims=True) + EPS) * head_gain
        gv = g_ref[rows, :]
        o_ref[rows, :] = (o * (gv * _sigmoid(gv))).astype(o_ref.dtype)
        return carry

    lax.fori_loop(0, seq // blk, body, 0)


def hgrn2(proj, lb_logits, head_gains, layer, *, chunk, unroll):
    bsz, seq, d4 = proj.shape
    d = d4 // 4
    heads = d // HEAD_DIM
    nl = lb_logits.shape[0]
    col = lambda part: pl.BlockSpec((None, seq, HEAD_DIM), lambda b, h: (b, 0, part * heads + h))
    return pl.pallas_call(
        functools.partial(_hgrn_kernel, chunk=chunk, unroll=unroll, layer=layer),
        grid=(bsz, heads),
        in_specs=[
            col(0), col(1), col(2), col(3),
            pl.BlockSpec((nl, HEAD_DIM), lambda b, h: (0, h)),
            pl.BlockSpec((None, 1, HEAD_DIM), lambda b, h: (layer, 0, 0)),
        ],
        out_specs=pl.BlockSpec((None, seq, HEAD_DIM), lambda b, h: (b, 0, h)),
        out_shape=jax.ShapeDtypeStruct((bsz, seq, d), BF16),
        scratch_shapes=[pltpu.VMEM((HEAD_DIM, HEAD_DIM), F32)],
        compiler_params=_params(("parallel", "parallel")),
        name="hgrn2",
    )(proj, proj, proj, proj, lb_logits, _gain3(head_gains))


def _dcum_kernel(lg_ref, bf_ref, dcol_ref, drow_ref, *, blk, heads):
    seq = lg_ref.shape[0]
    row = lax.broadcasted_iota(jnp.int32, (blk, blk), 0)
    col = lax.broadcasted_iota(jnp.int32, (blk, blk), 1)
    tri = jnp.where(row >= col, 1.0, 0.0).astype(BF16)
    carry = jnp.zeros((1, lg_ref.shape[1]), F32)
    for n in range(seq // blk):
        rows = slice(n * blk, (n + 1) * blk)
        logit = lg_ref[rows, :] + bf_ref[...]
        logsig = jnp.minimum(logit, 0.0) - jnp.log(1.0 + jnp.exp(-jnp.abs(logit)))
        c = _tri_cumsum(tri, logsig) + carry
        dcol_ref[rows, :] = c
        carry = c[blk - 1:blk, :]
    drow_ref[...] = dcol_ref[...].T[:heads, :]


def forget_cumsum(logits, b_f, *, heads, blk):
    bsz, seq, lanes = logits.shape
    return pl.pallas_call(
        functools.partial(_dcum_kernel, blk=blk, heads=heads),
        grid=(bsz,),
        in_specs=[
            pl.BlockSpec((None, seq, lanes), lambda b: (b, 0, 0)),
            pl.BlockSpec((1, lanes), lambda b: (0, 0)),
        ],
        out_specs=[
            pl.BlockSpec((None, seq, lanes), lambda b: (b, 0, 0)),
            pl.BlockSpec((None, heads, seq), lambda b: (b, 0, 0)),
        ],
        out_shape=[
            jax.ShapeDtypeStruct((bsz, seq, lanes), F32),
            jax.ShapeDtypeStruct((bsz, heads, seq), F32),
        ],
        compiler_params=_params(("parallel",)),
        name="forget_cumsum",
    )(logits, b_f)


def _fox_kernel(q_ref, k_ref, v_ref, dcol_ref, drow_ref, o_ref, *, tq):
    seq = q_ref.shape[0]
    head = pl.program_id(1)
    lane = lax.broadcasted_iota(jnp.int32, (tq, dcol_ref.shape[1]), 1)
    row = lax.broadcasted_iota(jnp.int32, (tq, tq), 0)
    col = lax.broadcasted_iota(jnp.int32, (tq, tq), 1)
    causal = row >= col
    nt = (((1,), (1,)), ((), ()))
    for qi in range(seq // tq):
        rows = slice(qi * tq, (qi + 1) * tq)
        q = q_ref[rows, :]
        dq = jnp.sum(jnp.where(lane == head, dcol_ref[rows, :], 0.0), axis=1, keepdims=True) * LOG2E
        m = jnp.full((tq, 1), -jnp.inf, F32)
        l = jnp.zeros((tq, 1), F32)
        acc = jnp.zeros((tq, HEAD_DIM), F32)
        for ki in range(qi + 1):
            cols = slice(ki * tq, (ki + 1) * tq)
            t = lax.dot_general(q, k_ref[cols, :], nt, preferred_element_type=F32)
            t = t - drow_ref[:, cols] * LOG2E
            if ki == qi:
                t = jnp.where(causal, t, -jnp.inf)
            m_new = jnp.maximum(m, jnp.max(t, axis=1, keepdims=True) + dq)
            alpha = jnp.exp2(m - m_new)
            prob = jnp.exp2(t + (dq - m_new))
            l = alpha * l + jnp.sum(prob, axis=1, keepdims=True)
            acc = alpha * acc + jnp.dot(prob.astype(BF16), v_ref[cols, :], preferred_element_type=F32)
            m = m_new
        o_ref[rows, :] = (acc / l).astype(o_ref.dtype)


def fox_attention(kvq, dcol, drow, *, tq):
    bsz, seq, d3 = kvq.shape
    d = d3 // 3
    heads = d // HEAD_DIM
    col = lambda part: pl.BlockSpec((None, seq, HEAD_DIM), lambda b, h: (b, 0, part * heads + h))
    return pl.pallas_call(
        functools.partial(_fox_kernel, tq=tq),
        grid=(bsz, heads),
        in_specs=[
            col(2), col(0), col(1),
            pl.BlockSpec((None, seq, dcol.shape[2]), lambda b, h: (b, 0, 0)),
            pl.BlockSpec((None, 1, seq), lambda b, h: (b * heads + h, 0, 0)),
        ],
        out_specs=pl.BlockSpec((None, seq, HEAD_DIM), lambda b, h: (b, 0, h)),
        out_shape=jax.ShapeDtypeStruct((bsz, seq, d), BF16),
        compiler_params=_params(("parallel", "parallel")),
        name="fox_attention",
    )(kvq, kvq, kvq, dcol, drow)


def kernel(x, p, mix_norm, mlp_norm, ple_norm, w_a_in, a_lb_logits, a_head_gain, w_a_out,
           kv_norm, w_kvf, b_f, w_b_q, w_b_out, w_mlp_up, w_mlp_down, w_ple_gate, w_ple_up,
           final_norm):
    bsz, seq, d = x.shape
    depth = p.shape[0]
    t = bsz * seq
    heads = d // HEAD_DIM
    assert depth == 2 and w_a_in.shape[0] == 1 and w_b_q.shape[0] == 1
    p3 = p.reshape(depth, t, p.shape[3])
    mix3 = _gain3(mix_norm)
    kv3 = kv_norm.reshape(1, 1, d)
    fin3 = final_norm.reshape(1, 1, d)
    tail = functools.partial(mlp_ple, **TILES["mlp_ple"])

    h = x.reshape(t, d)
    proj = norm_matmul(h, mix_norm, w_a_in, 0, out_dtype=F32, **TILES["in_proj"])
    mixed = hgrn2(proj.reshape(bsz, seq, 4 * d), a_lb_logits, a_head_gain, 0, **TILES["hgrn"])
    h, u, ss = matmul_residual(mixed.reshape(t, d), w_a_out, 0, h, mlp_norm, 0, **TILES["out_proj"])
    h = tail(u, ss, w_mlp_up, w_mlp_down, 0, h, ple_norm, p3, w_ple_gate, w_ple_up)

    w_kvf_t = w_kvf.T
    w_f_t = jnp.pad(w_kvf_t[2 * d:], ((0, LANES - heads), (0, 0)))
    b_fp = jnp.pad(b_f.astype(F32), (0, LANES - heads)).reshape(1, LANES)
    kvq, logits = kvq_projection(h, kv3, mix3, 1, w_kvf_t, w_b_q, 0, w_f_t, n_kv_cols=2 * d,
                                 q_scale=HEAD_DIM ** -0.5 * LOG2E, **TILES["kvq"])
    dcol, drow = forget_cumsum(logits.reshape(bsz, seq, LANES), b_fp, heads=heads, **TILES["dcum"])
    mixed = fox_attention(kvq.reshape(bsz, seq, 3 * d), dcol, drow.reshape(bsz * heads, 1, seq),
                          **TILES["attn"])
    h, u, ss = matmul_residual(mixed.reshape(t, d), w_b_out, 0, h, mlp_norm, 1, **TILES["out_proj"])
    out = tail(u, ss, w_mlp_up, w_mlp_down, 1, h, ple_norm, p3, w_ple_gate, w_ple_up, g_final=fin3)
    return out.reshape(bsz, seq, d)
```

```python
import functools

import jax
import jax.numpy as jnp
from jax import lax
from jax.experimental import pallas as pl
from jax.experimental.pallas import tpu as pltpu

F32 = jnp.float32
BF16 = jnp.bfloat16
EPS = 1e-6

LOG2E = 1.4426950408889634
HEAD_DIM = 128
LANES = 128
V7X_VMEM_LIMIT = 58 * 1024 * 1024
NORM_ROWS = 256

TILES = dict(
    in_proj=dict(tm=2048, tn=512),
    kvq=dict(tm=1024, tn=512),
    out_proj=dict(tm=2048, tn=512),
    mlp_ple=dict(tm=1024, tf=512, tn=256),
    hgrn=dict(chunk=64, unroll=8),
    dcum=dict(blk=256),
    attn=dict(tq=512),
)


def _params(sem):
    return pltpu.CompilerParams(dimension_semantics=sem, vmem_limit_bytes=V7X_VMEM_LIMIT)


def _once(block_shape, index_map):
    return pl.BlockSpec(block_shape, index_map, pipeline_mode=pl.Buffered(1))


def _sigmoid(x):
    return 1.0 / (1.0 + jnp.exp(-x))


def _norm_rows(h_ref, gains, out_refs):
    rows = h_ref.shape[0]
    rc = min(NORM_ROWS, rows)

    def body(r, carry):
        r0 = pl.multiple_of(r * rc, rc)
        hb = h_ref[pl.ds(r0, rc), :].astype(F32)
        y = hb * lax.rsqrt(jnp.mean(hb * hb, axis=-1, keepdims=True) + EPS)
        for g, o_ref in zip(gains, out_refs):
            o_ref[pl.ds(r0, rc), :] = (y * g).astype(o_ref.dtype)
        return carry

    lax.fori_loop(0, rows // rc, body, 0)


def _dot(a, w):
    return jnp.dot(a, w.astype(BF16), preferred_element_type=F32)


def _gain3(g):
    return g.reshape(g.shape[0], 1, g.shape[1])


def _gain_spec(layer, d):
    return pl.BlockSpec((None, 1, d), lambda i, j: (layer, 0, 0))


def _norm_mm_kernel(x_ref, g_ref, w_ref, o_ref, xn_ref):
    @pl.when(pl.program_id(1) == 0)
    def _():
        _norm_rows(x_ref, [g_ref[...]], [xn_ref])

    o_ref[...] = _dot(xn_ref[...], w_ref[...]).astype(o_ref.dtype)


def norm_matmul(x, gains, w, layer, *, tm, tn, out_dtype):
    m, k = x.shape
    n = w.shape[2]
    return pl.pallas_call(
        _norm_mm_kernel,
        grid=(m // tm, n // tn),
        in_specs=[
            _once((tm, k), lambda i, j: (i, 0)),
            _gain_spec(layer, k),
            pl.BlockSpec((None, k, tn), lambda i, j: (layer, 0, j)),
        ],
        out_specs=pl.BlockSpec((tm, tn), lambda i, j: (i, j)),
        out_shape=jax.ShapeDtypeStruct((m, n), out_dtype),
        scratch_shapes=[pltpu.VMEM((tm, k), BF16)],
        compiler_params=_params(("parallel", "arbitrary")),
        name="norm_matmul",
    )(x, _gain3(gains), w)


def _kvq_kernel(ukv_ref, uq_ref, ss_ref, wkv_ref, wq_ref, wf_ref, o_ref, lg_ref, r_ref, *, n_kv, q_scale):
    j = pl.program_id(1)
    d = ukv_ref.shape[1]
    nt = (((1,), (1,)), ((), ()))

    @pl.when(j == 0)
    def _():
        r = lax.rsqrt(jnp.sum(ss_ref[...], axis=1, keepdims=True) * (1.0 / d) + EPS)
        r_ref[...] = r
        lg = lax.dot_general(ukv_ref[...], wf_ref[...].astype(BF16), nt, preferred_element_type=F32)
        lg_ref[...] = lg * r

    @pl.when(j < n_kv)
    def _():
        acc = lax.dot_general(ukv_ref[...], wkv_ref[...].astype(BF16), nt, preferred_element_type=F32)
        o_ref[...] = (acc * r_ref[...]).astype(o_ref.dtype)

    @pl.when(j >= n_kv)
    def _():
        o_ref[...] = (_dot(uq_ref[...], wq_ref[...]) * (r_ref[...] * q_scale)).astype(o_ref.dtype)


def kvq_projection(u_kv, u_q, ss, w_kvf_t, w_q, wq_layer, w_f_t, *, n_kv_cols, q_scale, tm, tn):
    m, d = u_kv.shape
    nq_cols = w_q.shape[2]
    n_kv, n_q = n_kv_cols // tn, nq_cols // tn
    lanes = w_f_t.shape[0]
    return pl.pallas_call(
        functools.partial(_kvq_kernel, n_kv=n_kv, q_scale=q_scale),
        grid=(m // tm, n_kv + n_q),
        in_specs=[
            pl.BlockSpec((tm, d), lambda i, j: (i, 0)),
            pl.BlockSpec((tm, d), lambda i, j: (i, 0)),
            pl.BlockSpec((tm, ss.shape[1]), lambda i, j: (i, 0)),
            pl.BlockSpec((tn, d), lambda i, j: (jnp.minimum(j, n_kv - 1), 0)),
            pl.BlockSpec((None, d, tn), lambda i, j: (wq_layer, 0, jnp.clip(j - n_kv, 0, n_q - 1))),
            pl.BlockSpec((lanes, d), lambda i, j: (0, 0)),
        ],
        out_specs=[
            pl.BlockSpec((tm, tn), lambda i, j: (i, j)),
            pl.BlockSpec((tm, lanes), lambda i, j: (i, 0)),
        ],
        out_shape=[
            jax.ShapeDtypeStruct((m, n_kv_cols + nq_cols), BF16),
            jax.ShapeDtypeStruct((m, lanes), F32),
        ],
        scratch_shapes=[pltpu.VMEM((tm, 1), F32)],
        compiler_params=_params(("parallel", "arbitrary")),
        name="kvq_projection",
    )(u_kv, u_q, ss, w_kvf_t, w_q, w_f_t)


def _mm_res_kernel(a_ref, w_ref, x_ref, g_ref, h_ref, hg_ref, ss_ref):
    h = x_ref[...] + _dot(a_ref[...], w_ref[...])
    h_ref[...] = h
    hg_ref[...] = (h * g_ref[...]).astype(hg_ref.dtype)
    sq = h * h
    part = sq[:, :LANES]
    for n in range(1, h.shape[1] // LANES):
        part = part + sq[:, n * LANES:(n + 1) * LANES]
    ss_ref[...] = part


def matmul_residual(a, w, layer, x, gains, g_layer, *, tm, tn):
    m, k = a.shape
    n = w.shape[2]
    return pl.pallas_call(
        _mm_res_kernel,
        grid=(m // tm, n // tn),
        in_specs=[
            pl.BlockSpec((tm, k), lambda i, j: (i, 0)),
            pl.BlockSpec((None, k, tn), lambda i, j: (layer, 0, j)),
            pl.BlockSpec((tm, tn), lambda i, j: (i, j)),
            pl.BlockSpec((None, 1, tn), lambda i, j: (g_layer, 0, j)),
        ],
        out_specs=[
            pl.BlockSpec((tm, tn), lambda i, j: (i, j)),
            pl.BlockSpec((tm, tn), lambda i, j: (i, j)),
            pl.BlockSpec((tm, LANES), lambda i, j: (i, j)),
        ],
        out_shape=[
            jax.ShapeDtypeStruct((m, n), F32),
            jax.ShapeDtypeStruct((m, n), BF16),
            jax.ShapeDtypeStruct((m, (n // tn) * LANES), F32),
        ],
        compiler_params=_params(("parallel", "arbitrary")),
        name="matmul_residual",
    )(a, w, x, _gain3(gains))


def _mlp_ple_kernel(u_ref, ss_ref, wup_ref, wdn_ref, h_ref, gp_ref, p_ref, wg_ref, wu_ref, *rest,
                    nf, cn, tn, final, n_next):
    gn_refs, rest = rest[:n_next], rest[n_next:]
    if final:
        gf_ref, rest = rest[0], rest[1:]
    o_ref, hg_refs, rest = rest[0], rest[1:1 + n_next], rest[1 + n_next:]
    if n_next:
        sso_ref, u2_ref, r_ref, ssa_ref = rest
    else:
        u2_ref, r_ref = rest
    s = pl.program_id(1)
    d = o_ref.shape[1]
    hw = h_ref.shape[1]

    @pl.when(s == 0)
    def _():
        o_ref[...] = jnp.zeros_like(o_ref)
        r_ref[...] = lax.rsqrt(jnp.sum(ss_ref[...], axis=1, keepdims=True) * (1.0 / d) + EPS)
        if n_next:
            ssa_ref[...] = jnp.zeros_like(ssa_ref)

    @pl.when(s < nf)
    def _():
        hid = jnp.maximum(_dot(u_ref[...], wup_ref[...]) * r_ref[...], 0.0)
        hid = (hid * hid).astype(BF16)
        for n in range(d // cn):
            cs = slice(n * cn, (n + 1) * cn)
            o_ref[:, cs] += _dot(hid, wdn_ref[:, cs])

    @pl.when(s < d // hw)
    def _():
        c0 = pl.multiple_of(s * hw, hw)
        o_ref[:, pl.ds(c0, hw)] += h_ref[...]

    @pl.when(s == nf - 1)
    def _():
        _norm_rows(o_ref, [gp_ref[...]], [u2_ref])

    @pl.when(s >= nf)
    def _():
        c0 = pl.multiple_of((s - nf) * tn, tn)
        gate = _sigmoid(_dot(u2_ref[...], wg_ref[...]))
        up = _dot(p_ref[...].astype(BF16), wu_ref[...])
        hb = o_ref[:, pl.ds(c0, tn)] + up * gate
        o_ref[:, pl.ds(c0, tn)] = hb
        for g_ref, hg_ref in zip(gn_refs, hg_refs):
            hg_ref[...] = (hb * g_ref[...]).astype(hg_ref.dtype)
        if n_next:
            sq = hb * hb
            part = sq[:, :LANES]
            for n in range(1, tn // LANES):
                part = part + sq[:, n * LANES:(n + 1) * LANES]
            ssa_ref[...] += part

    @pl.when(s == pl.num_programs(1) - 1)
    def _():
        if n_next:
            sso_ref[...] = ssa_ref[...]
        if final:
            _norm_rows(o_ref, [gf_ref[...]], [o_ref])


def mlp_ple(u, ss, w_up, w_down, layer, h, g_ple, p, w_gate, w_ple_up, g_final=None, next_gains=(),
            *, tm, tf, tn):
    m, d = u.shape
    dff = w_up.shape[2]
    ple_dim = p.shape[2]
    nf, n_p = dff // tf, d // tn
    hw = max(LANES, d // nf)
    n_pieces = d // hw
    assert n_pieces <= nf
    final = g_final is not None
    n_next = len(next_gains)
    ple_col = lambda i, s: (layer, 0, jnp.clip(s - nf, 0, n_p - 1))
    in_specs = [
        pl.BlockSpec((tm, d), lambda i, s: (i, 0)),
        _once((tm, ss.shape[1]), lambda i, s: (i, 0)),
        pl.BlockSpec((None, d, tf), lambda i, s: (layer, 0, jnp.minimum(s, nf - 1))),
        pl.BlockSpec((None, tf, d), lambda i, s: (layer, jnp.minimum(s, nf - 1), 0)),
        pl.BlockSpec((tm, hw), lambda i, s: (i, jnp.minimum(s, n_pieces - 1))),
        _gain_spec(layer, d),
        _once((None, tm, ple_dim), lambda i, s: (layer, i, 0)),
        pl.BlockSpec((None, d, tn), ple_col),
        pl.BlockSpec((None, ple_dim, tn), ple_col),
    ]
    args = [u, ss, w_up, w_down, h, _gain3(g_ple), p, w_gate, w_ple_up]
    for g3, g_layer in next_gains:
        in_specs.append(pl.BlockSpec((None, 1, tn), lambda i, s, gl=g_layer: (gl, 0, jnp.clip(s - nf, 0, n_p - 1))))
        args.append(g3)
    if final:
        in_specs.append(_gain_spec(0, d))
        args.append(g_final)
    piece = lambda i, s: (i, jnp.clip(s - nf, 0, n_p - 1))
    out_specs = [pl.BlockSpec((tm, d), lambda i, s: (i, 0))] + [pl.BlockSpec((tm, tn), piece)] * n_next
    out_shape = [jax.ShapeDtypeStruct((m, d), F32)] + [jax.ShapeDtypeStruct((m, d), BF16)] * n_next
    scratch = [pltpu.VMEM((tm, d), BF16), pltpu.VMEM((tm, 1), F32)]
    if n_next:
        out_specs.append(pl.BlockSpec((tm, LANES), lambda i, s: (i, 0)))
        out_shape.append(jax.ShapeDtypeStruct((m, LANES), F32))
        scratch.append(pltpu.VMEM((tm, LANES), F32))
    outs = pl.pallas_call(
        functools.partial(_mlp_ple_kernel, nf=nf, cn=512, tn=tn, final=final, n_next=n_next),
        grid=(m // tm, nf + n_p),
        in_specs=in_specs,
        out_specs=out_specs,
        out_shape=out_shape,
        scratch_shapes=scratch,
        compiler_params=_params(("parallel", "arbitrary")),
        name="mlp_ple",
    )(*args)
    return outs if n_next else outs[0]


def _tri_cumsum(tri, x, terms=3):
    total = None
    rest = x
    for n in range(terms):
        piece = rest.astype(BF16)
        if n + 1 < terms:
            rest = rest - piece.astype(F32)
        part = jnp.dot(tri, piece, preferred_element_type=F32)
        total = part if total is None else total + part
    return total


def _hgrn_kernel(q_ref, f_ref, i_ref, g_ref, lbl_ref, hg_ref, o_ref, st_ref, *, chunk, unroll, layer):
    seq = q_ref.shape[0]
    blk = chunk * unroll
    mid = chunk // 2
    lg = lbl_ref[...]
    e = jnp.exp(lg - jnp.max(lg, axis=0, keepdims=True))
    sm = e / jnp.sum(e, axis=0, keepdims=True)
    lb = jnp.sum(sm[:layer + 1], axis=0, keepdims=True)
    head_gain = hg_ref[...]
    row = lax.broadcasted_iota(jnp.int32, (chunk, chunk), 0)
    col = lax.broadcasted_iota(jnp.int32, (chunk, chunk), 1)
    causal = row >= col
    tri = jnp.where(causal, 1.0, 0.0).astype(BF16)
    st_ref[...] = jnp.zeros_like(st_ref)
    nt = (((1,), (1,)), ((), ()))
    tn = (((0,), (0,)), ((), ()))

    def body(c, carry):
        r0 = pl.multiple_of(c * blk, blk)
        rows = pl.ds(r0, blk)
        fg = lb + (1.0 - lb) * _sigmoid(f_ref[rows, :])
        k_all = 1.0 - fg
        log2f = jnp.log(fg) * LOG2E
        qv = q_ref[rows, :]
        qc = qv * _sigmoid(qv) * (HEAD_DIM ** -0.5)
        v_all = i_ref[rows, :].astype(BF16)
        parts = []
        for u in range(unroll):
            sl = slice(u * chunk, (u + 1) * chunk)
            b = _tri_cumsum(tri, log2f[sl], terms=3)
            b_mid = b[mid - 1:mid, :]
            b_last = b[chunk - 1:chunk, :]
            q_mid = qc[sl] * jnp.exp2(b - b_mid)
            k_mid = k_all[sl] * jnp.exp2(b_mid - b)
            q_in = (q_mid * jnp.exp2(b_mid)).astype(BF16)
            k_end = (k_mid * jnp.exp2(b_last - b_mid)).astype(BF16)
            att = lax.dot_general(q_mid.astype(BF16), k_mid.astype(BF16), nt, preferred_element_type=F32)
            att = jnp.where(causal, att, 0.0).astype(BF16)
            o_intra = jnp.dot(att, v_all[sl], preferred_element_type=F32)
            kv = lax.dot_general(v_all[sl], k_end, tn, preferred_element_type=F32)
            parts.append((q_in, o_intra, kv, jnp.exp2(b_last)))
        st = st_ref[...]
        outs = []
        for q_in, o_intra, kv, dec in parts:
            outs.append(o_intra + lax.dot_general(q_in, st.astype(BF16), nt, preferred_element_type=F32))
            st = dec * st + kv
        st_ref[...] = st
        o = jnp.concatenate(outs, axis=0)
        o = o * lax.rsqrt(jnp.mean(o * o, axis=-1, keepdims=True) + EPS) * head_gain
        gv = g_ref[rows, :]
        o_ref[rows, :] = (o * (gv * _sigmoid(gv))).astype(o_ref.dtype)
        return carry

    lax.fori_loop(0, seq // blk, body, 0)


def hgrn2(proj, lb_logits, head_gains, layer, *, chunk, unroll):
    bsz, seq, d4 = proj.shape
    d = d4 // 4
    heads = d // HEAD_DIM
    nl = lb_logits.shape[0]
    col = lambda part: pl.BlockSpec((None, seq, HEAD_DIM), lambda b, h: (b, 0, part * heads + h))
    return pl.pallas_call(
        functools.partial(_hgrn_kernel, chunk=chunk, unroll=unroll, layer=layer),
        grid=(bsz, heads),
        in_specs=[
            col(0), col(1), col(2), col(3),
            pl.BlockSpec((nl, HEAD_DIM), lambda b, h: (0, h)),
            pl.BlockSpec((None, 1, HEAD_DIM), lambda b, h: (layer, 0, 0)),
        ],
        out_specs=pl.BlockSpec((None, seq, HEAD_DIM), lambda b, h: (b, 0, h)),
        out_shape=jax.ShapeDtypeStruct((bsz, seq, d), BF16),
        scratch_shapes=[pltpu.VMEM((HEAD_DIM, HEAD_DIM), F32)],
        compiler_params=_params(("parallel", "parallel")),
        name="hgrn2",
    )(proj, proj, proj, proj, lb_logits, _gain3(head_gains))


def _dcum_kernel(lg_ref, bf_ref, dcol_ref, drow_ref, *, blk, heads):
    seq = lg_ref.shape[0]
    row = lax.broadcasted_iota(jnp.int32, (blk, blk), 0)
    col = lax.broadcasted_iota(jnp.int32, (blk, blk), 1)
    tri = jnp.where(row >= col, 1.0, 0.0).astype(BF16)
    carry = jnp.zeros((1, lg_ref.shape[1]), F32)
    for n in range(seq // blk):
        rows = slice(n * blk, (n + 1) * blk)
        logit = lg_ref[rows, :] + bf_ref[...]
        logsig = jnp.minimum(logit, 0.0) - jnp.log(1.0 + jnp.exp(-jnp.abs(logit)))
        c = _tri_cumsum(tri, logsig) + carry
        dcol_ref[rows, :] = c
        carry = c[blk - 1:blk, :]
    drow_ref[...] = dcol_ref[...].T[:heads, :]


def forget_cumsum(logits, b_f, *, heads, blk):
    bsz, seq, lanes = logits.shape
    return pl.pallas_call(
        functools.partial(_dcum_kernel, blk=blk, heads=heads),
        grid=(bsz,),
        in_specs=[
            pl.BlockSpec((None, seq, lanes), lambda b: (b, 0, 0)),
            pl.BlockSpec((1, lanes), lambda b: (0, 0)),
        ],
        out_specs=[
            pl.BlockSpec((None, seq, lanes), lambda b: (b, 0, 0)),
            pl.BlockSpec((None, heads, seq), lambda b: (b, 0, 0)),
        ],
        out_shape=[
            jax.ShapeDtypeStruct((bsz, seq, lanes), F32),
            jax.ShapeDtypeStruct((bsz, heads, seq), F32),
        ],
        compiler_params=_params(("parallel",)),
        name="forget_cumsum",
    )(logits, b_f)


def _fox_kernel(q_ref, k_ref, v_ref, dcol_ref, drow_ref, o_ref, *, tq):
    seq = q_ref.shape[0]
    head = pl.program_id(1)
    lane = lax.broadcasted_iota(jnp.int32, (tq, dcol_ref.shape[1]), 1)
    row = lax.broadcasted_iota(jnp.int32, (tq, tq), 0)
    col = lax.broadcasted_iota(jnp.int32, (tq, tq), 1)
    causal = row >= col
    nt = (((1,), (1,)), ((), ()))
    for qi in range(seq // tq):
        rows = slice(qi * tq, (qi + 1) * tq)
        q = q_ref[rows, :]
        dq = jnp.sum(jnp.where(lane == head, dcol_ref[rows, :], 0.0), axis=1, keepdims=True) * LOG2E
        m = jnp.full((tq, 1), -jnp.inf, F32)
        l = jnp.zeros((tq, 1), F32)
        acc = jnp.zeros((tq, HEAD_DIM), F32)
        for ki in range(qi + 1):
            cols = slice(ki * tq, (ki + 1) * tq)
            t = lax.dot_general(q, k_ref[cols, :], nt, preferred_element_type=F32)
            t = t - drow_ref[:, cols] * LOG2E
            if ki == qi:
                t = jnp.where(causal, t, -jnp.inf)
            m_new = jnp.maximum(m, jnp.max(t, axis=1, keepdims=True) + dq)
            alpha = jnp.exp2(m - m_new)
            prob = jnp.exp2(t + (dq - m_new))
            l = alpha * l + jnp.sum(prob, axis=1, keepdims=True)
            acc = alpha * acc + jnp.dot(prob.astype(BF16), v_ref[cols, :], preferred_element_type=F32)
            m = m_new
        o_ref[rows, :] = (acc / l).astype(o_ref.dtype)


def fox_attention(kvq, dcol, drow, *, tq):
    bsz, seq, d3 = kvq.shape
    d = d3 // 3
    heads = d // HEAD_DIM
    col = lambda part: pl.BlockSpec((None, seq, HEAD_DIM), lambda b, h: (b, 0, part * heads + h))
    return pl.pallas_call(
        functools.partial(_fox_kernel, tq=tq),
        grid=(bsz, heads),
        in_specs=[
            col(2), col(0), col(1),
            pl.BlockSpec((None, seq, dcol.shape[2]), lambda b, h: (b, 0, 0)),
            pl.BlockSpec((None, 1, seq), lambda b, h: (b * heads + h, 0, 0)),
        ],
        out_specs=pl.BlockSpec((None, seq, HEAD_DIM), lambda b, h: (b, 0, h)),
        out_shape=jax.ShapeDtypeStruct((bsz, seq, d), BF16),
        compiler_params=_params(("parallel", "parallel")),
        name="fox_attention",
    )(kvq, kvq, kvq, dcol, drow)


def kernel(x, p, mix_norm, mlp_norm, ple_norm, w_a_in, a_lb_logits, a_head_gain, w_a_out,
           kv_norm, w_kvf, b_f, w_b_q, w_b_out, w_mlp_up, w_mlp_down, w_ple_gate, w_ple_up,
           final_norm):
    bsz, seq, d = x.shape
    depth = p.shape[0]
    t = bsz * seq
    heads = d // HEAD_DIM
    assert depth == 2 and w_a_in.shape[0] == 1 and w_b_q.shape[0] == 1
    p3 = p.reshape(depth, t, p.shape[3])
    mix3 = _gain3(mix_norm)
    kv3 = kv_norm.reshape(1, 1, d)
    fin3 = final_norm.reshape(1, 1, d)
    tail = functools.partial(mlp_ple, **TILES["mlp_ple"])

    h = x.reshape(t, d)
    proj = norm_matmul(h, mix_norm, w_a_in, 0, out_dtype=F32, **TILES["in_proj"])
    mixed = hgrn2(proj.reshape(bsz, seq, 4 * d), a_lb_logits, a_head_gain, 0, **TILES["hgrn"])
    h, u, ss = matmul_residual(mixed.reshape(t, d), w_a_out, 0, h, mlp_norm, 0, **TILES["out_proj"])
    h, u_kv, u_q, ss = tail(u, ss, w_mlp_up, w_mlp_down, 0, h, ple_norm, p3, w_ple_gate, w_ple_up,
                            next_gains=[(kv3, 0), (mix3, 1)])

    w_kvf_t = w_kvf.T
    w_f_t = jnp.pad(w_kvf_t[2 * d:], ((0, LANES - heads), (0, 0)))
    b_fp = jnp.pad(b_f.astype(F32), (0, LANES - heads)).reshape(1, LANES)
    kvq, logits = kvq_projection(u_kv, u_q, ss, w_kvf_t, w_b_q, 0, w_f_t, n_kv_cols=2 * d,
                                 q_scale=HEAD_DIM ** -0.5 * LOG2E, **TILES["kvq"])
    dcol, drow = forget_cumsum(logits.reshape(bsz, seq, LANES), b_fp, heads=heads, **TILES["dcum"])
    mixed = fox_attention(kvq.reshape(bsz, seq, 3 * d), dcol, drow.reshape(bsz * heads, 1, seq),
                          **TILES["attn"])
    h, u, ss = matmul_residual(mixed.reshape(t, d), w_b_out, 0, h, mlp_norm, 1, **TILES["out_proj"])
    out = tail(u, ss, w_mlp_up, w_mlp_down, 1, h, ple_norm, p3, w_ple_gate, w_ple_up, g_final=fin3)
    return out.reshape(bsz, seq, d)
```

```python
import functools

import jax
import jax.numpy as jnp
from jax import lax
from jax.experimental import pallas as pl
from jax.experimental.pallas import tpu as pltpu

F32 = jnp.float32
BF16 = jnp.bfloat16
EPS = 1e-6

LOG2E = 1.4426950408889634
HEAD_DIM = 128
LANES = 128
V7X_VMEM_LIMIT = 56 * 1024 * 1024
NORM_ROWS = 256

TILES = dict(
    in_proj=dict(tm=2048, tn=512),
    kvq=dict(tm=1024, tn=512),
    out_proj=dict(tm=2048, tn=512),
    mlp_ple=dict(tm=1024, tf=512, tn=256),
    hgrn=dict(chunk=64, unroll=8, group=4),
    dcum=dict(blk=256),
    attn=dict(tq=512, group=2),
)


def _params(sem):
    return pltpu.CompilerParams(dimension_semantics=sem, vmem_limit_bytes=V7X_VMEM_LIMIT)


def _once(block_shape, index_map):
    return pl.BlockSpec(block_shape, index_map, pipeline_mode=pl.Buffered(1))


def _sigmoid(x):
    return 1.0 / (1.0 + jnp.exp(-x))


def _norm_rows(h_ref, gains, out_refs):
    rows = h_ref.shape[0]
    rc = min(NORM_ROWS, rows)

    def body(r, carry):
        r0 = pl.multiple_of(r * rc, rc)
        hb = h_ref[pl.ds(r0, rc), :].astype(F32)
        y = hb * lax.rsqrt(jnp.mean(hb * hb, axis=-1, keepdims=True) + EPS)
        for g, o_ref in zip(gains, out_refs):
            o_ref[pl.ds(r0, rc), :] = (y * g).astype(o_ref.dtype)
        return carry

    lax.fori_loop(0, rows // rc, body, 0)


def _dot(a, w):
    return jnp.dot(a, w.astype(BF16), preferred_element_type=F32)


def _gain3(g):
    return g.reshape(g.shape[0], 1, g.shape[1])


def _gain_spec(layer, d):
    return pl.BlockSpec((None, 1, d), lambda i, j: (layer, 0, 0))


def _norm_mm_kernel(x_ref, g_ref, w_ref, o_ref, xn_ref):
    @pl.when(pl.program_id(1) == 0)
    def _():
        _norm_rows(x_ref, [g_ref[...]], [xn_ref])

    o_ref[...] = _dot(xn_ref[...], w_ref[...]).astype(o_ref.dtype)


def norm_matmul(x, gains, w, layer, *, tm, tn, out_dtype):
    m, k = x.shape
    n = w.shape[2]
    return pl.pallas_call(
        _norm_mm_kernel,
        grid=(m // tm, n // tn),
        in_specs=[
            _once((tm, k), lambda i, j: (i, 0)),
            _gain_spec(layer, k),
            pl.BlockSpec((None, k, tn), lambda i, j: (layer, 0, j)),
        ],
        out_specs=pl.BlockSpec((tm, tn), lambda i, j: (i, j)),
        out_shape=jax.ShapeDtypeStruct((m, n), out_dtype),
        scratch_shapes=[pltpu.VMEM((tm, k), BF16)],
        compiler_params=_params(("parallel", "arbitrary")),
        name="norm_matmul",
    )(x, _gain3(gains), w)


def _kvq_kernel(h_ref, gkv_ref, gq_ref, wkv_ref, wq_ref, wf_ref, o_ref, lg_ref, u_ref, *, n_kv, q_scale):
    j = pl.program_id(1)

    nt = (((1,), (1,)), ((), ()))

    @pl.when(j == 0)
    def _():
        _norm_rows(h_ref, [gkv_ref[...], gq_ref[...]], [u_ref.at[0], u_ref.at[1]])
        lg_ref[...] = lax.dot_general(u_ref[0], wf_ref[...].astype(BF16), nt, preferred_element_type=F32)

    @pl.when(j < n_kv)
    def _():
        acc = lax.dot_general(u_ref[0], wkv_ref[...].astype(BF16), nt, preferred_element_type=F32)
        o_ref[...] = acc.astype(o_ref.dtype)

    @pl.when(j >= n_kv)
    def _():
        o_ref[...] = (_dot(u_ref[1], wq_ref[...]) * q_scale).astype(o_ref.dtype)


def kvq_projection(h, g_kv, g_q, q_layer, w_kvf_t, w_q, wq_layer, w_f_t, *, n_kv_cols, q_scale, tm, tn):
    m, d = h.shape
    nq_cols = w_q.shape[2]
    n_kv, n_q = n_kv_cols // tn, nq_cols // tn
    lanes = w_f_t.shape[0]
    return pl.pallas_call(
        functools.partial(_kvq_kernel, n_kv=n_kv, q_scale=q_scale),
        grid=(m // tm, n_kv + n_q),
        in_specs=[
            pl.BlockSpec((tm, d), lambda i, j: (i, 0)),
            _gain_spec(0, d),
            _gain_spec(q_layer, d),
            pl.BlockSpec((tn, d), lambda i, j: (jnp.minimum(j, n_kv - 1), 0)),
            pl.BlockSpec((None, d, tn), lambda i, j: (wq_layer, 0, jnp.clip(j - n_kv, 0, n_q - 1))),
            pl.BlockSpec((lanes, d), lambda i, j: (0, 0)),
        ],
        out_specs=[
            pl.BlockSpec((tm, tn), lambda i, j: (i, j)),
            pl.BlockSpec((tm, lanes), lambda i, j: (i, 0)),
        ],
        out_shape=[
            jax.ShapeDtypeStruct((m, n_kv_cols + nq_cols), BF16),
            jax.ShapeDtypeStruct((m, lanes), F32),
        ],
        scratch_shapes=[pltpu.VMEM((2, tm, d), BF16)],
        compiler_params=_params(("parallel", "arbitrary")),
        name="kvq_projection",
    )(h, g_kv, g_q, w_kvf_t, w_q, w_f_t)


def _mm_res_kernel(a_ref, w_ref, x_ref, g_ref, h_ref, hg_ref, ss_ref):
    h = x_ref[...] + _dot(a_ref[...], w_ref[...])
    h_ref[...] = h
    hg_ref[...] = (h * g_ref[...]).astype(hg_ref.dtype)
    sq = h * h
    part = sq[:, :LANES]
    for n in range(1, h.shape[1] // LANES):
        part = part + sq[:, n * LANES:(n + 1) * LANES]
    ss_ref[...] = part


def matmul_residual(a, w, layer, x, gains, g_layer, *, tm, tn):
    m, k = a.shape
    n = w.shape[2]
    return pl.pallas_call(
        _mm_res_kernel,
        grid=(m // tm, n // tn),
        in_specs=[
            pl.BlockSpec((tm, k), lambda i, j: (i, 0)),
            pl.BlockSpec((None, k, tn), lambda i, j: (layer, 0, j)),
            pl.BlockSpec((tm, tn), lambda i, j: (i, j)),
            pl.BlockSpec((None, 1, tn), lambda i, j: (g_layer, 0, j)),
        ],
        out_specs=[
            pl.BlockSpec((tm, tn), lambda i, j: (i, j)),
            pl.BlockSpec((tm, tn), lambda i, j: (i, j)),
            pl.BlockSpec((tm, LANES), lambda i, j: (i, j)),
        ],
        out_shape=[
            jax.ShapeDtypeStruct((m, n), F32),
            jax.ShapeDtypeStruct((m, n), BF16),
            jax.ShapeDtypeStruct((m, (n // tn) * LANES), F32),
        ],
        compiler_params=_params(("parallel", "arbitrary")),
        name="matmul_residual",
    )(a, w, x, _gain3(gains))


def _mlp_ple_kernel(u_ref, ss_ref, wup_ref, wdn_ref, h_ref, gp_ref, p_ref, wg_ref, wu_ref, *rest,
                    nf, cn, tn, final):
    if final:
        gf_ref, o_ref, u2_ref, r_ref = rest
    else:
        o_ref, u2_ref, r_ref = rest
    s = pl.program_id(1)
    d = o_ref.shape[1]
    hw = h_ref.shape[1]

    @pl.when(s == 0)
    def _():
        o_ref[...] = jnp.zeros_like(o_ref)
        r_ref[...] = lax.rsqrt(jnp.sum(ss_ref[...], axis=1, keepdims=True) * (1.0 / d) + EPS)

    @pl.when(s < nf)
    def _():
        hid = jnp.maximum(_dot(u_ref[...], wup_ref[...]) * r_ref[...], 0.0)
        hid = (hid * hid).astype(BF16)
        for n in range(d // cn):
            cs = slice(n * cn, (n + 1) * cn)
            o_ref[:, cs] += _dot(hid, wdn_ref[:, cs])

    @pl.when(s < d // hw)
    def _():
        c0 = pl.multiple_of(s * hw, hw)
        o_ref[:, pl.ds(c0, hw)] += h_ref[...]

    @pl.when(s == nf - 1)
    def _():
        _norm_rows(o_ref, [gp_ref[...]], [u2_ref])

    @pl.when(s >= nf)
    def _():
        c0 = pl.multiple_of((s - nf) * tn, tn)
        gate = _sigmoid(_dot(u2_ref[...], wg_ref[...]))
        up = _dot(p_ref[...].astype(BF16), wu_ref[...])
        o_ref[:, pl.ds(c0, tn)] += up * gate

    if final:
        @pl.when(s == pl.num_programs(1) - 1)
        def _():
            _norm_rows(o_ref, [gf_ref[...]], [o_ref])


def mlp_ple(u, ss, w_up, w_down, layer, h, g_ple, p, w_gate, w_ple_up, g_final=None, *, tm, tf, tn):
    m, d = u.shape
    dff = w_up.shape[2]
    ple_dim = p.shape[2]
    nf, n_p = dff // tf, d // tn
    hw = max(LANES, d // nf)
    n_pieces = d // hw
    assert n_pieces <= nf
    final = g_final is not None
    ple_col = lambda i, s: (layer, 0, jnp.clip(s - nf, 0, n_p - 1))
    in_specs = [
        _once((tm, d), lambda i, s: (i, 0)),
        pl.BlockSpec((tm, ss.shape[1]), lambda i, s: (i, 0)),
        pl.BlockSpec((None, d, tf), lambda i, s: (layer, 0, jnp.minimum(s, nf - 1))),
        pl.BlockSpec((None, tf, d), lambda i, s: (layer, jnp.minimum(s, nf - 1), 0)),
        pl.BlockSpec((tm, hw), lambda i, s: (i, jnp.minimum(s, n_pieces - 1))),
        _gain_spec(layer, d),
        _once((None, tm, ple_dim), lambda i, s: (layer, i, 0)),
        pl.BlockSpec((None, d, tn), ple_col),
        pl.BlockSpec((None, ple_dim, tn), ple_col),
    ]
    args = [u, ss, w_up, w_down, h, _gain3(g_ple), p, w_gate, w_ple_up]
    if final:
        in_specs.append(_gain_spec(0, d))
        args.append(g_final)
    return pl.pallas_call(
        functools.partial(_mlp_ple_kernel, nf=nf, cn=512, tn=tn, final=final),
        grid=(m // tm, nf + n_p),
        in_specs=in_specs,
        out_specs=_once((tm, d), lambda i, s: (i, 0)),
        out_shape=jax.ShapeDtypeStruct((m, d), F32),
        scratch_shapes=[pltpu.VMEM((tm, d), BF16), pltpu.VMEM((tm, 1), F32)],
        compiler_params=_params(("parallel", "arbitrary")),
        name="mlp_ple",
    )(*args)


def _tri_cumsum(tri, x, terms=3):
    total = None
    rest = x
    for n in range(terms):
        piece = rest.astype(BF16)
        if n + 1 < terms:
            rest = rest - piece.astype(F32)
        part = jnp.dot(tri, piece, preferred_element_type=F32)
        total = part if total is None else total + part
    return total


def _hgrn_block(qv, fv, iv, gv, lb, head_gain, st, causal, tri, *, chunk):
    mid = chunk // 2
    nt = (((1,), (1,)), ((), ()))
    tn = (((0,), (0,)), ((), ()))
    fg = lb + (1.0 - lb) * _sigmoid(fv)
    k_all = 1.0 - fg
    log2f = jnp.log(fg) * LOG2E
    qc = qv * _sigmoid(qv) * (HEAD_DIM ** -0.5)
    v_all = iv.astype(BF16)
    parts = []
    for u in range(qv.shape[0] // chunk):
        sl = slice(u * chunk, (u + 1) * chunk)
        b = _tri_cumsum(tri, log2f[sl], terms=3)
        b_mid = b[mid - 1:mid, :]
        b_last = b[chunk - 1:chunk, :]
        q_mid = qc[sl] * jnp.exp2(b - b_mid)
        k_mid = k_all[sl] * jnp.exp2(b_mid - b)
        q_in = (q_mid * jnp.exp2(b_mid)).astype(BF16)
        k_end = (k_mid * jnp.exp2(b_last - b_mid)).astype(BF16)
        att = lax.dot_general(q_mid.astype(BF16), k_mid.astype(BF16), nt, preferred_element_type=F32)
        att = jnp.where(causal, att, 0.0).astype(BF16)
        o_intra = jnp.dot(att, v_all[sl], preferred_element_type=F32)
        kv = lax.dot_general(v_all[sl], k_end, tn, preferred_element_type=F32)
        parts.append((q_in, o_intra, kv, jnp.exp2(b_last)))
    outs = []
    for q_in, o_intra, kv, dec in parts:
        outs.append(o_intra + lax.dot_general(q_in, st.astype(BF16), nt, preferred_element_type=F32))
        st = dec * st + kv
    o = jnp.concatenate(outs, axis=0)
    o = o * lax.rsqrt(jnp.mean(o * o, axis=-1, keepdims=True) + EPS) * head_gain
    return o * (gv * _sigmoid(gv)), st


def _hgrn_kernel(q_ref, f_ref, i_ref, g_ref, lbl_ref, hg_ref, o_ref, st_ref, *, chunk, unroll, layer):
    seq = q_ref.shape[0]
    blk = chunk * unroll
    n_heads = o_ref.shape[1] // HEAD_DIM
    lg = lbl_ref[...]
    e = jnp.exp(lg - jnp.max(lg, axis=0, keepdims=True))
    sm = e / jnp.sum(e, axis=0, keepdims=True)
    lb = jnp.sum(sm[:layer + 1], axis=0, keepdims=True)
    head_gain = hg_ref[...]
    row = lax.broadcasted_iota(jnp.int32, (chunk, chunk), 0)
    col = lax.broadcasted_iota(jnp.int32, (chunk, chunk), 1)
    causal = row >= col
    tri = jnp.where(causal, 1.0, 0.0).astype(BF16)

    st_ref[...] = jnp.zeros_like(st_ref)

    def body(c, carry):
        rows = pl.ds(pl.multiple_of(c * blk, blk), blk)
        for hd in range(n_heads):
            cols = slice(hd * HEAD_DIM, (hd + 1) * HEAD_DIM)
            o, st = _hgrn_block(q_ref[rows, cols], f_ref[rows, cols], i_ref[rows, cols], g_ref[rows, cols],
                                lb[:, cols], head_gain, st_ref[hd], causal, tri, chunk=chunk)
            st_ref[hd] = st
            o_ref[rows, cols] = o.astype(o_ref.dtype)
        return carry

    lax.fori_loop(0, seq // blk, body, 0)


def hgrn2(proj, lb_logits, head_gains, layer, *, chunk, unroll, group):
    bsz, seq, d4 = proj.shape
    d = d4 // 4
    gw = group * HEAD_DIM
    ng = d // gw
    nl = lb_logits.shape[0]
    col = lambda part: pl.BlockSpec((None, seq, gw), lambda b, g: (b, 0, part * ng + g))
    return pl.pallas_call(
        functools.partial(_hgrn_kernel, chunk=chunk, unroll=unroll, layer=layer),
        grid=(bsz, ng),
        in_specs=[
            col(0), col(1), col(2), col(3),
            pl.BlockSpec((nl, gw), lambda b, g: (0, g)),
            pl.BlockSpec((None, 1, HEAD_DIM), lambda b, g: (layer, 0, 0)),
        ],
        out_specs=pl.BlockSpec((None, seq, gw), lambda b, g: (b, 0, g)),
        out_shape=jax.ShapeDtypeStruct((bsz, seq, d), BF16),
        scratch_shapes=[pltpu.VMEM((group, HEAD_DIM, HEAD_DIM), F32)],
        compiler_params=_params(("parallel", "parallel")),
        name="hgrn2",
    )(proj, proj, proj, proj, lb_logits, _gain3(head_gains))


def _dcum_kernel(lg_ref, bf_ref, dcol_ref, drow_ref, *, blk, heads):
    seq = lg_ref.shape[0]
    row = lax.broadcasted_iota(jnp.int32, (blk, blk), 0)
    col = lax.broadcasted_iota(jnp.int32, (blk, blk), 1)
    tri = jnp.where(row >= col, 1.0, 0.0).astype(BF16)
    carry = jnp.zeros((1, lg_ref.shape[1]), F32)
    for n in range(seq // blk):
        rows = slice(n * blk, (n + 1) * blk)
        logit = lg_ref[rows, :] + bf_ref[...]
        logsig = jnp.minimum(logit, 0.0) - jnp.log(1.0 + jnp.exp(-jnp.abs(logit)))
        c = _tri_cumsum(tri, logsig) + carry
        dcol_ref[rows, :] = c
        carry = c[blk - 1:blk, :]
    drow_ref[...] = dcol_ref[...].T[:heads, :]


def forget_cumsum(logits, b_f, *, heads, blk):
    bsz, seq, lanes = logits.shape
    return pl.pallas_call(
        functools.partial(_dcum_kernel, blk=blk, heads=heads),
        grid=(bsz,),
        in_specs=[
            pl.BlockSpec((None, seq, lanes), lambda b: (b, 0, 0)),
            pl.BlockSpec((1, lanes), lambda b: (0, 0)),
        ],
        out_specs=[
            pl.BlockSpec((None, seq, lanes), lambda b: (b, 0, 0)),
            pl.BlockSpec((None, heads, seq), lambda b: (b, 0, 0)),
        ],
        out_shape=[
            jax.ShapeDtypeStruct((bsz, seq, lanes), F32),
            jax.ShapeDtypeStruct((bsz, heads, seq), F32),
        ],
        compiler_params=_params(("parallel",)),
        name="forget_cumsum",
    )(logits, b_f)


def _fox_kernel(q_ref, k_ref, v_ref, dcol_ref, drow_ref, o_ref, *, tq):
    seq = q_ref.shape[0]
    n_heads = o_ref.shape[1] // HEAD_DIM
    lane = lax.broadcasted_iota(jnp.int32, (tq, dcol_ref.shape[1]), 1)
    row = lax.broadcasted_iota(jnp.int32, (tq, tq), 0)
    col = lax.broadcasted_iota(jnp.int32, (tq, tq), 1)
    causal = row >= col
    nt = (((1,), (1,)), ((), ()))
    for hd in range(n_heads):
        head = pl.program_id(1) * n_heads + hd
        hc = slice(hd * HEAD_DIM, (hd + 1) * HEAD_DIM)
        for qi in range(seq // tq):
            rows = slice(qi * tq, (qi + 1) * tq)
            q = q_ref[rows, hc]
            dq = jnp.sum(jnp.where(lane == head, dcol_ref[rows, :], 0.0), axis=1, keepdims=True) * LOG2E
            m = jnp.full((tq, 1), -jnp.inf, F32)
            l = jnp.zeros((tq, 1), F32)
            acc = jnp.zeros((tq, HEAD_DIM), F32)
            for ki in range(qi + 1):
                cols = slice(ki * tq, (ki + 1) * tq)
                t = lax.dot_general(q, k_ref[cols, hc], nt, preferred_element_type=F32)
                t = t - drow_ref[hd:hd + 1, cols] * LOG2E
                if ki == qi:
                    t = jnp.where(causal, t, -jnp.inf)
                m_new = jnp.maximum(m, jnp.max(t, axis=1, keepdims=True) + dq)
                alpha = jnp.exp2(m - m_new)
                prob = jnp.exp2(t + (dq - m_new))
                l = alpha * l + jnp.sum(prob, axis=1, keepdims=True)
                acc = alpha * acc + jnp.dot(prob.astype(BF16), v_ref[cols, hc], preferred_element_type=F32)
                m = m_new
            o_ref[rows, hc] = (acc / l).astype(o_ref.dtype)


def fox_attention(kvq, dcol, drow, *, tq, group):
    bsz, seq, d3 = kvq.shape
    d = d3 // 3
    gw = group * HEAD_DIM
    ng = d // gw
    col = lambda part: pl.BlockSpec((None, seq, gw), lambda b, g: (b, 0, part * ng + g))
    return pl.pallas_call(
        functools.partial(_fox_kernel, tq=tq),
        grid=(bsz, ng),
        in_specs=[
            col(2), col(0), col(1),
            pl.BlockSpec((None, seq, dcol.shape[2]), lambda b, g: (b, 0, 0)),
            pl.BlockSpec((None, group, seq), lambda b, g: (b * ng + g, 0, 0)),
        ],
        out_specs=pl.BlockSpec((None, seq, gw), lambda b, g: (b, 0, g)),
        out_shape=jax.ShapeDtypeStruct((bsz, seq, d), BF16),
        compiler_params=_params(("parallel", "parallel")),
        name="fox_attention",
    )(kvq, kvq, kvq, dcol, drow.reshape(bsz * ng, group, seq))


def kernel(x, p, mix_norm, mlp_norm, ple_norm, w_a_in, a_lb_logits, a_head_gain, w_a_out,
           kv_norm, w_kvf, b_f, w_b_q, w_b_out, w_mlp_up, w_mlp_down, w_ple_gate, w_ple_up,
           final_norm):
    bsz, seq, d = x.shape
    depth = p.shape[0]
    t = bsz * seq
    heads = d // HEAD_DIM
    assert depth == 2 and w_a_in.shape[0] == 1 and w_b_q.shape[0] == 1
    p3 = p.reshape(depth, t, p.shape[3])
    mix3 = _gain3(mix_norm)
    kv3 = kv_norm.reshape(1, 1, d)
    fin3 = final_norm.reshape(1, 1, d)
    tail = functools.partial(mlp_ple, **TILES["mlp_ple"])

    h = x.reshape(t, d)
    proj = norm_matmul(h, mix_norm, w_a_in, 0, out_dtype=F32, **TILES["in_proj"])
    mixed = hgrn2(proj.reshape(bsz, seq, 4 * d), a_lb_logits, a_head_gain, 0, **TILES["hgrn"])
    h, u, ss = matmul_residual(mixed.reshape(t, d), w_a_out, 0, h, mlp_norm, 0, **TILES["out_proj"])
    h = tail(u, ss, w_mlp_up, w_mlp_down, 0, h, ple_norm, p3, w_ple_gate, w_ple_up)

    w_kvf_t = w_kvf.T
    w_f_t = jnp.pad(w_kvf_t[2 * d:], ((0, LANES - heads), (0, 0)))
    b_fp = jnp.pad(b_f.astype(F32), (0, LANES - heads)).reshape(1, LANES)
    kvq, logits = kvq_projection(h, kv3, mix3, 1, w_kvf_t, w_b_q, 0, w_f_t, n_kv_cols=2 * d,
                                 q_scale=HEAD_DIM ** -0.5 * LOG2E, **TILES["kvq"])
    dcol, drow = forget_cumsum(logits.reshape(bsz, seq, LANES), b_fp, heads=heads, **TILES["dcum"])
    mixed = fox_attention(kvq.reshape(bsz, seq, 3 * d), dcol, drow, **TILES["attn"])
    h, u, ss = matmul_residual(mixed.reshape(t, d), w_b_out, 0, h, mlp_norm, 1, **TILES["out_proj"])
    out = tail(u, ss, w_mlp_up, w_mlp_down, 1, h, ple_norm, p3, w_ple_gate, w_ple_up, g_final=fin3)
    return out.reshape(bsz, seq, d)
```

```python
import functools

import jax
import jax.numpy as jnp
from jax import lax
from jax.experimental import pallas as pl
from jax.experimental.pallas import tpu as pltpu

F32 = jnp.float32
BF16 = jnp.bfloat16
EPS = 1e-6

LOG2E = 1.4426950408889634
HEAD_DIM = 128
LANES = 128
V7X_VMEM_LIMIT = 56 * 1024 * 1024
NORM_ROWS = 256

TILES = dict(
    in_proj=dict(tm=2048, tn=512),
    kvq=dict(tm=1024, tn=512),
    out_proj=dict(tm=2048, tn=512),
    mlp_ple=dict(tm=1024, tf=1024, tn=512),
    hgrn=dict(chunk=64, unroll=8, group=2),
    dcum=dict(blk=256),
    attn=dict(tq=512, group=2),
)


def _params(sem):
    return pltpu.CompilerParams(dimension_semantics=sem, vmem_limit_bytes=V7X_VMEM_LIMIT)


def _once(block_shape, index_map):
    return pl.BlockSpec(block_shape, index_map, pipeline_mode=pl.Buffered(1))


def _sigmoid(x):
    return 1.0 / (1.0 + jnp.exp(-x))


def _norm_rows(h_ref, gains, out_refs):
    rows = h_ref.shape[0]
    rc = min(NORM_ROWS, rows)

    def body(r, carry):
        r0 = pl.multiple_of(r * rc, rc)
        hb = h_ref[pl.ds(r0, rc), :].astype(F32)
        y = hb * lax.rsqrt(jnp.mean(hb * hb, axis=-1, keepdims=True) + EPS)
        for g, o_ref in zip(gains, out_refs):
            o_ref[pl.ds(r0, rc), :] = (y * g).astype(o_ref.dtype)
        return carry

    lax.fori_loop(0, rows // rc, body, 0)


def _dot(a, w):
    return jnp.dot(a, w.astype(BF16), preferred_element_type=F32)


def _gain3(g):
    return g.reshape(g.shape[0], 1, g.shape[1])


def _gain_spec(layer, d):
    return pl.BlockSpec((None, 1, d), lambda i, j: (layer, 0, 0))


def _cast_plan(weights, n_steps, step_of):
    in_specs, out_specs, out_shapes = [], [], []
    for w, layer in weights:
        _, r, c = w.shape
        rb = r // n_steps
        in_specs.append(pl.BlockSpec((None, rb, c), lambda *idx, l=layer: (l, step_of(*idx), 0)))
        out_specs.append(pl.BlockSpec((None, rb, c), lambda *idx: (0, step_of(*idx), 0)))
        out_shapes.append(jax.ShapeDtypeStruct((1, r, c), BF16))
    return in_specs, out_specs, out_shapes


def _cast_slabs(src_refs, dst_refs):
    for src_ref, dst_ref in zip(src_refs, dst_refs):
        dst_ref[...] = src_ref[...].astype(dst_ref.dtype)


def _norm_mm_kernel(x_ref, g_ref, w_ref, o_ref, xn_ref):
    @pl.when(pl.program_id(1) == 0)
    def _():
        _norm_rows(x_ref, [g_ref[...]], [xn_ref])

    o_ref[...] = _dot(xn_ref[...], w_ref[...]).astype(o_ref.dtype)


def norm_matmul(x, gains, w, layer, *, tm, tn, out_dtype):
    m, k = x.shape
    n = w.shape[2]
    return pl.pallas_call(
        _norm_mm_kernel,
        grid=(m // tm, n // tn),
        in_specs=[
            _once((tm, k), lambda i, j: (i, 0)),
            _gain_spec(layer, k),
            pl.BlockSpec((None, k, tn), lambda i, j: (layer, 0, j)),
        ],
        out_specs=pl.BlockSpec((tm, tn), lambda i, j: (i, j)),
        out_shape=jax.ShapeDtypeStruct((m, n), out_dtype),
        scratch_shapes=[pltpu.VMEM((tm, k), BF16)],
        compiler_params=_params(("parallel", "arbitrary")),
        name="norm_matmul",
    )(x, _gain3(gains), w)


def _kvq_kernel(h_ref, gkv_ref, gq_ref, wkv_ref, wq_ref, wf_ref, o_ref, lg_ref, u_ref, *, n_kv, q_scale):
    j = pl.program_id(1)

    nt = (((1,), (1,)), ((), ()))

    @pl.when(j == 0)
    def _():
        _norm_rows(h_ref, [gkv_ref[...], gq_ref[...]], [u_ref.at[0], u_ref.at[1]])
        lg_ref[...] = lax.dot_general(u_ref[0], wf_ref[...].astype(BF16), nt, preferred_element_type=F32)

    @pl.when(j < n_kv)
    def _():
        acc = lax.dot_general(u_ref[0], wkv_ref[...].astype(BF16), nt, preferred_element_type=F32)
        o_ref[...] = acc.astype(o_ref.dtype)

    @pl.when(j >= n_kv)
    def _():
        o_ref[...] = (_dot(u_ref[1], wq_ref[...]) * q_scale).astype(o_ref.dtype)


def kvq_projection(h, g_kv, g_q, q_layer, w_kvf_t, w_q, wq_layer, w_f_t, *, n_kv_cols, q_scale, tm, tn):
    m, d = h.shape
    nq_cols = w_q.shape[2]
    n_kv, n_q = n_kv_cols // tn, nq_cols // tn
    lanes = w_f_t.shape[0]
    return pl.pallas_call(
        functools.partial(_kvq_kernel, n_kv=n_kv, q_scale=q_scale),
        grid=(m // tm, n_kv + n_q),
        in_specs=[
            pl.BlockSpec((tm, d), lambda i, j: (i, 0)),
            _gain_spec(0, d),
            _gain_spec(q_layer, d),
            pl.BlockSpec((tn, d), lambda i, j: (jnp.minimum(j, n_kv - 1), 0)),
            pl.BlockSpec((None, d, tn), lambda i, j: (wq_layer, 0, jnp.clip(j - n_kv, 0, n_q - 1))),
            pl.BlockSpec((lanes, d), lambda i, j: (0, 0)),
        ],
        out_specs=[
            pl.BlockSpec((tm, tn), lambda i, j: (i, j)),
            pl.BlockSpec((tm, lanes), lambda i, j: (i, 0)),
        ],
        out_shape=[
            jax.ShapeDtypeStruct((m, n_kv_cols + nq_cols), BF16),
            jax.ShapeDtypeStruct((m, lanes), F32),
        ],
        scratch_shapes=[pltpu.VMEM((2, tm, d), BF16)],
        compiler_params=_params(("parallel", "arbitrary")),
        name="kvq_projection",
    )(h, g_kv, g_q, w_kvf_t, w_q, w_f_t)


def _mm_res_kernel(a_ref, w_ref, x_ref, g_ref, h_ref, hg_ref, ss_ref):
    h = x_ref[...] + _dot(a_ref[...], w_ref[...])
    h_ref[...] = h
    hg_ref[...] = (h * g_ref[...]).astype(hg_ref.dtype)
    sq = h * h
    part = sq[:, :LANES]
    for n in range(1, h.shape[1] // LANES):
        part = part + sq[:, n * LANES:(n + 1) * LANES]
    ss_ref[...] = part


def matmul_residual(a, w, layer, x, gains, g_layer, *, tm, tn):
    m, k = a.shape
    n = w.shape[2]
    return pl.pallas_call(
        _mm_res_kernel,
        grid=(m // tm, n // tn),
        in_specs=[
            pl.BlockSpec((tm, k), lambda i, j: (i, 0)),
            pl.BlockSpec((None, k, tn), lambda i, j: (layer, 0, j)),
            pl.BlockSpec((tm, tn), lambda i, j: (i, j)),
            pl.BlockSpec((None, 1, tn), lambda i, j: (g_layer, 0, j)),
        ],
        out_specs=[
            pl.BlockSpec((tm, tn), lambda i, j: (i, j)),
            pl.BlockSpec((tm, tn), lambda i, j: (i, j)),
            pl.BlockSpec((tm, LANES), lambda i, j: (i, j)),
        ],
        out_shape=[
            jax.ShapeDtypeStruct((m, n), F32),
            jax.ShapeDtypeStruct((m, n), BF16),
            jax.ShapeDtypeStruct((m, (n // tn) * LANES), F32),
        ],
        compiler_params=_params(("parallel", "arbitrary")),
        name="matmul_residual",
    )(a, w, x, _gain3(gains))


def _mlp_ple_kernel(u_ref, ss_ref, wup_ref, wdn_ref, h_ref, gp_ref, p_ref, wg_ref, wu_ref, *rest,
                    nf, cn, tn, final):
    if final:
        gf_ref, o_ref, u2_ref, r_ref = rest
    else:
        o_ref, u2_ref, r_ref = rest
    s = pl.program_id(1)
    d = o_ref.shape[1]
    hw = h_ref.shape[1]

    @pl.when(s == 0)
    def _():
        o_ref[...] = jnp.zeros_like(o_ref)
        r_ref[...] = lax.rsqrt(jnp.sum(ss_ref[...], axis=1, keepdims=True) * (1.0 / d) + EPS)

    @pl.when(s < nf)
    def _():
        hid = jnp.maximum(_dot(u_ref[...], wup_ref[...]) * r_ref[...], 0.0)
        hid = (hid * hid).astype(BF16)
        for n in range(d // cn):
            cs = slice(n * cn, (n + 1) * cn)
            o_ref[:, cs] += _dot(hid, wdn_ref[:, cs])

    @pl.when(s < d // hw)
    def _():
        c0 = pl.multiple_of(s * hw, hw)
        o_ref[:, pl.ds(c0, hw)] += h_ref[...]

    @pl.when(s == nf - 1)
    def _():
        _norm_rows(o_ref, [gp_ref[...]], [u2_ref])

    @pl.when(s >= nf)
    def _():
        c0 = pl.multiple_of((s - nf) * tn, tn)
        gate = _sigmoid(_dot(u2_ref[...], wg_ref[...]))
        up = _dot(p_ref[...].astype(BF16), wu_ref[...])
        o_ref[:, pl.ds(c0, tn)] += up * gate

    if final:
        @pl.when(s == pl.num_programs(1) - 1)
        def _():
            _norm_rows(o_ref, [gf_ref[...]], [o_ref])


def mlp_ple(u, ss, w_up, w_down, w_layer, layer, h, g_ple, p, w_gate, w_ple_up, g_final=None, *, tm, tf, tn):
    m, d = u.shape
    dff = w_up.shape[2]
    ple_dim = p.shape[2]
    nf, n_p = dff // tf, d // tn
    hw = max(LANES, d // nf)
    n_pieces = d // hw
    assert n_pieces <= nf
    final = g_final is not None
    ple_col = lambda i, s: (layer, 0, jnp.clip(s - nf, 0, n_p - 1))
    in_specs = [
        _once((tm, d), lambda i, s: (i, 0)),
        pl.BlockSpec((tm, ss.shape[1]), lambda i, s: (i, 0)),
        pl.BlockSpec((None, d, tf), lambda i, s: (w_layer, 0, jnp.minimum(s, nf - 1))),
        pl.BlockSpec((None, tf, d), lambda i, s: (w_layer, jnp.minimum(s, nf - 1), 0)),
        pl.BlockSpec((tm, hw), lambda i, s: (i, jnp.minimum(s, n_pieces - 1))),
        _gain_spec(layer, d),
        _once((None, tm, ple_dim), lambda i, s: (layer, i, 0)),
        pl.BlockSpec((None, d, tn), ple_col),
        pl.BlockSpec((None, ple_dim, tn), ple_col),
    ]
    args = [u, ss, w_up, w_down, h, _gain3(g_ple), p, w_gate, w_ple_up]
    if final:
        in_specs.append(_gain_spec(0, d))
        args.append(g_final)
    return pl.pallas_call(
        functools.partial(_mlp_ple_kernel, nf=nf, cn=512, tn=tn, final=final),
        grid=(m // tm, nf + n_p),
        in_specs=in_specs,
        out_specs=_once((tm, d), lambda i, s: (i, 0)),
        out_shape=jax.ShapeDtypeStruct((m, d), F32),
        scratch_shapes=[pltpu.VMEM((tm, d), BF16), pltpu.VMEM((tm, 1), F32)],
        compiler_params=_params(("parallel", "arbitrary")),
        name="mlp_ple",
    )(*args)


def _tri_cumsum(tri, x, terms=3):
    total = None
    rest = x
    for n in range(terms):
        piece = rest.astype(BF16)
        if n + 1 < terms:
            rest = rest - piece.astype(F32)
        part = jnp.dot(tri, piece, preferred_element_type=F32)
        total = part if total is None else total + part
    return total


def _hgrn_block(qv, fv, iv, gv, lb, head_gain, st, causal, tri, *, chunk):
    mid = chunk // 2
    nt = (((1,), (1,)), ((), ()))
    tn = (((0,), (0,)), ((), ()))
    fg = lb + (1.0 - lb) * _sigmoid(fv)
    k_all = 1.0 - fg
    log2f = jnp.log(fg) * LOG2E
    qc = qv * _sigmoid(qv) * (HEAD_DIM ** -0.5)
    v_all = iv.astype(BF16)
    parts = []
    for u in range(qv.shape[0] // chunk):
        sl = slice(u * chunk, (u + 1) * chunk)
        b = _tri_cumsum(tri, log2f[sl], terms=3)
        b_mid = b[mid - 1:mid, :]
        b_last = b[chunk - 1:chunk, :]
        q_mid = qc[sl] * jnp.exp2(b - b_mid)
        k_mid = k_all[sl] * jnp.exp2(b_mid - b)
        q_in = (q_mid * jnp.exp2(b_mid)).astype(BF16)
        k_end = (k_mid * jnp.exp2(b_last - b_mid)).astype(BF16)
        att = lax.dot_general(q_mid.astype(BF16), k_mid.astype(BF16), nt, preferred_element_type=F32)
        att = jnp.where(causal, att, 0.0).astype(BF16)
        o_intra = jnp.dot(att, v_all[sl], preferred_element_type=F32)
        kv = lax.dot_general(v_all[sl], k_end, tn, preferred_element_type=F32)
        parts.append((q_in, o_intra, kv, jnp.exp2(b_last)))
    outs = []
    for q_in, o_intra, kv, dec in parts:
        outs.append(o_intra + lax.dot_general(q_in, st.astype(BF16), nt, preferred_element_type=F32))
        st = dec * st + kv
    o = jnp.concatenate(outs, axis=0)
    o = o * lax.rsqrt(jnp.mean(o * o, axis=-1, keepdims=True) + EPS) * head_gain
    return o * (gv * _sigmoid(gv)), st


def _hgrn_kernel(q_ref, f_ref, i_ref, g_ref, lbl_ref, hg_ref, *rest, chunk, unroll, layer, n_cast):
    o_ref, st_ref = rest[n_cast], rest[-1]
    _cast_slabs(rest[:n_cast], rest[n_cast + 1:-1])
    seq = q_ref.shape[0]
    blk = chunk * unroll
    n_heads = o_ref.shape[1] // HEAD_DIM
    lg = lbl_ref[...]
    e = jnp.exp(lg - jnp.max(lg, axis=0, keepdims=True))
    sm = e / jnp.sum(e, axis=0, keepdims=True)
    lb = jnp.sum(sm[:layer + 1], axis=0, keepdims=True)
    head_gain = hg_ref[...]
    row = lax.broadcasted_iota(jnp.int32, (chunk, chunk), 0)
    col = lax.broadcasted_iota(jnp.int32, (chunk, chunk), 1)
    causal = row >= col
    tri = jnp.where(causal, 1.0, 0.0).astype(BF16)

    st_ref[...] = jnp.zeros_like(st_ref)

    def body(c, carry):
        rows = pl.ds(pl.multiple_of(c * blk, blk), blk)
        for hd in range(n_heads):
            cols = slice(hd * HEAD_DIM, (hd + 1) * HEAD_DIM)
            o, st = _hgrn_block(q_ref[rows, cols], f_ref[rows, cols], i_ref[rows, cols], g_ref[rows, cols],
                                lb[:, cols], head_gain, st_ref[hd], causal, tri, chunk=chunk)
            st_ref[hd] = st
            o_ref[rows, cols] = o.astype(o_ref.dtype)
        return carry

    lax.fori_loop(0, seq // blk, body, 0)


def hgrn2(proj, lb_logits, head_gains, layer, cast=(), *, chunk, unroll, group):
    bsz, seq, d4 = proj.shape
    d = d4 // 4
    gw = group * HEAD_DIM
    ng = d // gw
    nl = lb_logits.shape[0]
    col = lambda part: pl.BlockSpec((None, seq, gw), lambda b, g: (b, 0, part * ng + g))
    c_in, c_out, c_shape = _cast_plan(cast, bsz * ng, lambda b, g: b * ng + g)
    outs = pl.pallas_call(
        functools.partial(_hgrn_kernel, chunk=chunk, unroll=unroll, layer=layer, n_cast=len(cast)),
        grid=(bsz, ng),
        in_specs=[
            col(0), col(1), col(2), col(3),
            pl.BlockSpec((nl, gw), lambda b, g: (0, g)),
            pl.BlockSpec((None, 1, HEAD_DIM), lambda b, g: (layer, 0, 0)),
        ] + c_in,
        out_specs=[pl.BlockSpec((None, seq, gw), lambda b, g: (b, 0, g))] + c_out,
        out_shape=[jax.ShapeDtypeStruct((bsz, seq, d), BF16)] + c_shape,
        scratch_shapes=[pltpu.VMEM((group, HEAD_DIM, HEAD_DIM), F32)],
        compiler_params=_params(("parallel", "parallel")),
        name="hgrn2",
    )(proj, proj, proj, proj, lb_logits, _gain3(head_gains), *[w for w, _ in cast])
    return outs


def _dcum_kernel(lg_ref, bf_ref, dcol_ref, drow_ref, *, blk, heads):
    seq = lg_ref.shape[0]
    row = lax.broadcasted_iota(jnp.int32, (blk, blk), 0)
    col = lax.broadcasted_iota(jnp.int32, (blk, blk), 1)
    tri = jnp.where(row >= col, 1.0, 0.0).astype(BF16)
    carry = jnp.zeros((1, lg_ref.shape[1]), F32)
    for n in range(seq // blk):
        rows = slice(n * blk, (n + 1) * blk)
        logit = lg_ref[rows, :] + bf_ref[...]
        logsig = jnp.minimum(logit, 0.0) - jnp.log(1.0 + jnp.exp(-jnp.abs(logit)))
        c = _tri_cumsum(tri, logsig) + carry
        dcol_ref[rows, :] = c
        carry = c[blk - 1:blk, :]
    drow_ref[...] = dcol_ref[...].T[:heads, :]


def forget_cumsum(logits, b_f, *, heads, blk):
    bsz, seq, lanes = logits.shape
    return pl.pallas_call(
        functools.partial(_dcum_kernel, blk=blk, heads=heads),
        grid=(bsz,),
        in_specs=[
            pl.BlockSpec((None, seq, lanes), lambda b: (b, 0, 0)),
            pl.BlockSpec((1, lanes), lambda b: (0, 0)),
        ],
        out_specs=[
            pl.BlockSpec((None, seq, lanes), lambda b: (b, 0, 0)),
            pl.BlockSpec((None, heads, seq), lambda b: (b, 0, 0)),
        ],
        out_shape=[
            jax.ShapeDtypeStruct((bsz, seq, lanes), F32),
            jax.ShapeDtypeStruct((bsz, heads, seq), F32),
        ],
        compiler_params=_params(("parallel",)),
        name="forget_cumsum",
    )(logits, b_f)


def _fox_kernel(q_ref, k_ref, v_ref, dcol_ref, drow_ref, *rest, tq, n_cast):
    o_ref = rest[n_cast]
    _cast_slabs(rest[:n_cast], rest[n_cast + 1:])
    seq = q_ref.shape[0]
    n_heads = o_ref.shape[1] // HEAD_DIM
    lane = lax.broadcasted_iota(jnp.int32, (tq, dcol_ref.shape[1]), 1)
    row = lax.broadcasted_iota(jnp.int32, (tq, tq), 0)
    col = lax.broadcasted_iota(jnp.int32, (tq, tq), 1)
    causal = row >= col
    nt = (((1,), (1,)), ((), ()))
    for hd in range(n_heads):
        head = pl.program_id(1) * n_heads + hd
        hc = slice(hd * HEAD_DIM, (hd + 1) * HEAD_DIM)
        for qi in range(seq // tq):
            rows = slice(qi * tq, (qi + 1) * tq)
            q = q_ref[rows, hc]
            dq = jnp.sum(jnp.where(lane == head, dcol_ref[rows, :], 0.0), axis=1, keepdims=True) * LOG2E
            m = jnp.full((tq, 1), -jnp.inf, F32)
            l = jnp.zeros((tq, 1), F32)
            acc = jnp.zeros((tq, HEAD_DIM), F32)
            for ki in range(qi + 1):
                cols = slice(ki * tq, (ki + 1) * tq)
                t = lax.dot_general(q, k_ref[cols, hc], nt, preferred_element_type=F32)
                t = t - drow_ref[hd:hd + 1, cols] * LOG2E
                if ki == qi:
                    t = jnp.where(causal, t, -jnp.inf)
                m_new = jnp.maximum(m, jnp.max(t, axis=1, keepdims=True) + dq)
                alpha = jnp.exp2(m - m_new)
                prob = jnp.exp2(t + (dq - m_new))
                l = alpha * l + jnp.sum(prob, axis=1, keepdims=True)
                acc = alpha * acc + jnp.dot(prob.astype(BF16), v_ref[cols, hc], preferred_element_type=F32)
                m = m_new
            o_ref[rows, hc] = (acc / l).astype(o_ref.dtype)


def fox_attention(kvq, dcol, drow, cast=(), *, tq, group):
    bsz, seq, d3 = kvq.shape
    d = d3 // 3
    gw = group * HEAD_DIM
    ng = d // gw
    col = lambda part: pl.BlockSpec((None, seq, gw), lambda b, g: (b, 0, part * ng + g))
    c_in, c_out, c_shape = _cast_plan(cast, bsz * ng, lambda b, g: b * ng + g)
    return pl.pallas_call(
        functools.partial(_fox_kernel, tq=tq, n_cast=len(cast)),
        grid=(bsz, ng),
        in_specs=[
            col(2), col(0), col(1),
            pl.BlockSpec((None, seq, dcol.shape[2]), lambda b, g: (b, 0, 0)),
            pl.BlockSpec((None, group, seq), lambda b, g: (b * ng + g, 0, 0)),
        ] + c_in,
        out_specs=[pl.BlockSpec((None, seq, gw), lambda b, g: (b, 0, g))] + c_out,
        out_shape=[jax.ShapeDtypeStruct((bsz, seq, d), BF16)] + c_shape,
        compiler_params=_params(("parallel", "parallel")),
        name="fox_attention",
    )(kvq, kvq, kvq, dcol, drow.reshape(bsz * ng, group, seq), *[w for w, _ in cast])


def kernel(x, p, mix_norm, mlp_norm, ple_norm, w_a_in, a_lb_logits, a_head_gain, w_a_out,
           kv_norm, w_kvf, b_f, w_b_q, w_b_out, w_mlp_up, w_mlp_down, w_ple_gate, w_ple_up,
           final_norm):
    bsz, seq, d = x.shape
    depth = p.shape[0]
    t = bsz * seq
    heads = d // HEAD_DIM
    assert depth == 2 and w_a_in.shape[0] == 1 and w_b_q.shape[0] == 1
    p3 = p.reshape(depth, t, p.shape[3])
    mix3 = _gain3(mix_norm)
    kv3 = kv_norm.reshape(1, 1, d)
    fin3 = final_norm.reshape(1, 1, d)
    tail = functools.partial(mlp_ple, **TILES["mlp_ple"])

    h = x.reshape(t, d)
    proj = norm_matmul(h, mix_norm, w_a_in, 0, out_dtype=F32, **TILES["in_proj"])
    mixed, w_up_b, w_dn_b = hgrn2(proj.reshape(bsz, seq, 4 * d), a_lb_logits, a_head_gain, 0,
                                  cast=[(w_mlp_up, 0), (w_mlp_down, 0)], **TILES["hgrn"])
    h, u, ss = matmul_residual(mixed.reshape(t, d), w_a_out, 0, h, mlp_norm, 0, **TILES["out_proj"])
    h = tail(u, ss, w_up_b, w_dn_b, 0, 0, h, ple_norm, p3, w_ple_gate, w_ple_up)

    w_kvf_t = w_kvf.T
    w_f_t = jnp.pad(w_kvf_t[2 * d:], ((0, LANES - heads), (0, 0)))
    b_fp = jnp.pad(b_f.astype(F32), (0, LANES - heads)).reshape(1, LANES)
    kvq, logits = kvq_projection(h, kv3, mix3, 1, w_kvf_t, w_b_q, 0, w_f_t, n_kv_cols=2 * d,
                                 q_scale=HEAD_DIM ** -0.5 * LOG2E, **TILES["kvq"])
    dcol, drow = forget_cumsum(logits.reshape(bsz, seq, LANES), b_fp, heads=heads, **TILES["dcum"])
    mixed, w_up_b, w_dn_b = fox_attention(kvq.reshape(bsz, seq, 3 * d), dcol, drow,
                                          cast=[(w_mlp_up, 1), (w_mlp_down, 1)], **TILES["attn"])
    h, u, ss = matmul_residual(mixed.reshape(t, d), w_b_out, 0, h, mlp_norm, 1, **TILES["out_proj"])
    out = tail(u, ss, w_up_b, w_dn_b, 0, 1, h, ple_norm, p3, w_ple_gate, w_ple_up, g_final=fin3)
    return out.reshape(bsz, seq, d)
```

```python
import functools

import jax
import jax.numpy as jnp
from jax import lax
from jax.experimental import pallas as pl
from jax.experimental.pallas import tpu as pltpu

F32 = jnp.float32
BF16 = jnp.bfloat16
EPS = 1e-6

LOG2E = 1.4426950408889634
HEAD_DIM = 128
LANES = 128
V7X_VMEM_LIMIT = 56 * 1024 * 1024
NORM_ROWS = 256

TILES = dict(
    in_proj=dict(tm=2048, tn=512),
    kvq=dict(tm=1024, tn=512),
    out_proj=dict(tm=2048, tn=512),
    mlp_ple=dict(tm=1024, tf=1024, tn=512),
    hgrn=dict(chunk=64, unroll=8, group=2),
    dcum=dict(blk=256),
    attn=dict(tq=512, group=2),
)


def _params(sem):
    return pltpu.CompilerParams(dimension_semantics=sem, vmem_limit_bytes=V7X_VMEM_LIMIT)


def _once(block_shape, index_map):
    return pl.BlockSpec(block_shape, index_map, pipeline_mode=pl.Buffered(1))


def _sigmoid(x):
    return 1.0 / (1.0 + jnp.exp(-x))


def _norm_rows(h_ref, gains, out_refs):
    rows = h_ref.shape[0]
    rc = min(NORM_ROWS, rows)

    def body(r, carry):
        r0 = pl.multiple_of(r * rc, rc)
        hb = h_ref[pl.ds(r0, rc), :].astype(F32)
        y = hb * lax.rsqrt(jnp.mean(hb * hb, axis=-1, keepdims=True) + EPS)
        for g, o_ref in zip(gains, out_refs):
            o_ref[pl.ds(r0, rc), :] = (y * g).astype(o_ref.dtype)
        return carry

    lax.fori_loop(0, rows // rc, body, 0)


def _dot(a, w):
    return jnp.dot(a, w.astype(BF16), preferred_element_type=F32)


def _gain3(g):
    return g.reshape(g.shape[0], 1, g.shape[1])


def _gain_spec(layer, d):
    return pl.BlockSpec((None, 1, d), lambda i, j: (layer, 0, 0))


def _snake(i, j, n):
    return jnp.where(i % 2 == 0, j, n - 1 - j)


def _cast_plan(weights, n_steps, step_of):
    in_specs, out_specs, out_shapes = [], [], []
    for w, layer in weights:
        _, r, c = w.shape
        rb = r // n_steps
        in_specs.append(pl.BlockSpec((None, rb, c), lambda *idx, l=layer: (l, step_of(*idx), 0)))
        out_specs.append(pl.BlockSpec((None, rb, c), lambda *idx: (0, step_of(*idx), 0)))
        out_shapes.append(jax.ShapeDtypeStruct((1, r, c), BF16))
    return in_specs, out_specs, out_shapes


def _cast_slabs(src_refs, dst_refs):
    for src_ref, dst_ref in zip(src_refs, dst_refs):
        dst_ref[...] = src_ref[...].astype(dst_ref.dtype)


def _norm_mm_kernel(x_ref, g_ref, w_ref, o_ref, xn_ref):
    @pl.when(pl.program_id(1) == 0)
    def _():
        _norm_rows(x_ref, [g_ref[...]], [xn_ref])

    o_ref[...] = _dot(xn_ref[...], w_ref[...]).astype(o_ref.dtype)


def norm_matmul(x, gains, w, layer, *, tm, tn, out_dtype):
    m, k = x.shape
    n = w.shape[2]
    return pl.pallas_call(
        _norm_mm_kernel,
        grid=(m // tm, n // tn),
        in_specs=[
            _once((tm, k), lambda i, j: (i, 0)),
            _gain_spec(layer, k),
            pl.BlockSpec((None, k, tn), lambda i, j: (layer, 0, _snake(i, j, n // tn))),
        ],
        out_specs=pl.BlockSpec((tm, tn), lambda i, j: (i, _snake(i, j, n // tn))),
        out_shape=jax.ShapeDtypeStruct((m, n), out_dtype),
        scratch_shapes=[pltpu.VMEM((tm, k), BF16)],
        compiler_params=_params(("parallel", "arbitrary")),
        name="norm_matmul",
    )(x, _gain3(gains), w)


def _kvq_kernel(h_ref, gkv_ref, gq_ref, wkv_ref, wq_ref, wf_ref, o_ref, lg_ref, u_ref, *, n_kv, q_scale):
    col = _snake(pl.program_id(0), pl.program_id(1), pl.num_programs(1))
    nt = (((1,), (1,)), ((), ()))

    @pl.when(pl.program_id(1) == 0)
    def _():
        _norm_rows(h_ref, [gkv_ref[...], gq_ref[...]], [u_ref.at[0], u_ref.at[1]])
        lg_ref[...] = lax.dot_general(u_ref[0], wf_ref[...].astype(BF16), nt, preferred_element_type=F32)

    @pl.when(col < n_kv)
    def _():
        acc = lax.dot_general(u_ref[0], wkv_ref[...].astype(BF16), nt, preferred_element_type=F32)
        o_ref[...] = acc.astype(o_ref.dtype)

    @pl.when(col >= n_kv)
    def _():
        o_ref[...] = (_dot(u_ref[1], wq_ref[...]) * q_scale).astype(o_ref.dtype)


def kvq_projection(h, g_kv, g_q, q_layer, w_kvf_t, w_q, wq_layer, w_f_t, *, n_kv_cols, q_scale, tm, tn):
    m, d = h.shape
    nq_cols = w_q.shape[2]
    n_kv, n_q = n_kv_cols // tn, nq_cols // tn
    col = lambda i, j: _snake(i, j, n_kv + n_q)
    lanes = w_f_t.shape[0]
    return pl.pallas_call(
        functools.partial(_kvq_kernel, n_kv=n_kv, q_scale=q_scale),
        grid=(m // tm, n_kv + n_q),
        in_specs=[
            pl.BlockSpec((tm, d), lambda i, j: (i, 0)),
            _gain_spec(0, d),
            _gain_spec(q_layer, d),
            pl.BlockSpec((tn, d), lambda i, j: (jnp.minimum(col(i, j), n_kv - 1), 0)),
            pl.BlockSpec((None, d, tn), lambda i, j: (wq_layer, 0, jnp.clip(col(i, j) - n_kv, 0, n_q - 1))),
            pl.BlockSpec((lanes, d), lambda i, j: (0, 0)),
        ],
        out_specs=[
            pl.BlockSpec((tm, tn), lambda i, j: (i, col(i, j))),
            pl.BlockSpec((tm, lanes), lambda i, j: (i, 0)),
        ],
        out_shape=[
            jax.ShapeDtypeStruct((m, n_kv_cols + nq_cols), BF16),
            jax.ShapeDtypeStruct((m, lanes), F32),
        ],
        scratch_shapes=[pltpu.VMEM((2, tm, d), BF16)],
        compiler_params=_params(("parallel", "arbitrary")),
        name="kvq_projection",
    )(h, g_kv, g_q, w_kvf_t, w_q, w_f_t)


def _mm_res_kernel(a_ref, w_ref, x_ref, g_ref, h_ref, hg_ref, ss_ref):
    h = x_ref[...] + _dot(a_ref[...], w_ref[...])
    h_ref[...] = h
    hg_ref[...] = (h * g_ref[...]).astype(hg_ref.dtype)
    sq = h * h
    part = sq[:, :LANES]
    for n in range(1, h.shape[1] // LANES):
        part = part + sq[:, n * LANES:(n + 1) * LANES]
    ss_ref[...] = part


def matmul_residual(a, w, layer, x, gains, g_layer, *, tm, tn):
    m, k = a.shape
    n = w.shape[2]
    col = lambda i, j: _snake(i, j, n // tn)
    return pl.pallas_call(
        _mm_res_kernel,
        grid=(m // tm, n // tn),
        in_specs=[
            pl.BlockSpec((tm, k), lambda i, j: (i, 0)),
            pl.BlockSpec((None, k, tn), lambda i, j: (layer, 0, col(i, j))),
            pl.BlockSpec((tm, tn), lambda i, j: (i, col(i, j))),
            pl.BlockSpec((None, 1, tn), lambda i, j: (g_layer, 0, col(i, j))),
        ],
        out_specs=[
            pl.BlockSpec((tm, tn), lambda i, j: (i, col(i, j))),
            pl.BlockSpec((tm, tn), lambda i, j: (i, col(i, j))),
            pl.BlockSpec((tm, LANES), lambda i, j: (i, col(i, j))),
        ],
        out_shape=[
            jax.ShapeDtypeStruct((m, n), F32),
            jax.ShapeDtypeStruct((m, n), BF16),
            jax.ShapeDtypeStruct((m, (n // tn) * LANES), F32),
        ],
        compiler_params=_params(("parallel", "arbitrary")),
        name="matmul_residual",
    )(a, w, x, _gain3(gains))


def _mlp_ple_kernel(u_ref, ss_ref, wup_ref, wdn_ref, h_ref, gp_ref, p_ref, wg_ref, wu_ref, *rest,
                    nf, cn, tn, final):
    if final:
        gf_ref, o_ref, u2_ref, r_ref = rest
    else:
        o_ref, u2_ref, r_ref = rest
    s = pl.program_id(1)
    d = o_ref.shape[1]
    hw = h_ref.shape[1]

    @pl.when(s == 0)
    def _():
        o_ref[...] = jnp.zeros_like(o_ref)
        r_ref[...] = lax.rsqrt(jnp.sum(ss_ref[...], axis=1, keepdims=True) * (1.0 / d) + EPS)

    @pl.when(s < nf)
    def _():
        hid = jnp.maximum(_dot(u_ref[...], wup_ref[...]) * r_ref[...], 0.0)
        hid = (hid * hid).astype(BF16)
        for n in range(d // cn):
            cs = slice(n * cn, (n + 1) * cn)
            o_ref[:, cs] += _dot(hid, wdn_ref[:, cs])

    @pl.when(s < d // hw)
    def _():
        c0 = pl.multiple_of(s * hw, hw)
        o_ref[:, pl.ds(c0, hw)] += h_ref[...]

    @pl.when(s == nf - 1)
    def _():
        _norm_rows(o_ref, [gp_ref[...]], [u2_ref])

    @pl.when(s >= nf)
    def _():
        c0 = pl.multiple_of(_snake(pl.program_id(0), s - nf, d // tn) * tn, tn)
        gate = _sigmoid(_dot(u2_ref[...], wg_ref[...]))
        up = _dot(p_ref[...].astype(BF16), wu_ref[...])
        o_ref[:, pl.ds(c0, tn)] += up * gate

    if final:
        @pl.when(s == pl.num_programs(1) - 1)
        def _():
            _norm_rows(o_ref, [gf_ref[...]], [o_ref])


def mlp_ple(u, ss, w_up, w_down, w_layer, layer, h, g_ple, p, w_gate, w_ple_up, g_final=None, *, tm, tf, tn):
    m, d = u.shape
    dff = w_up.shape[2]
    ple_dim = p.shape[2]
    nf, n_p = dff // tf, d // tn
    hw = max(LANES, d // nf)
    n_pieces = d // hw
    assert n_pieces <= nf
    final = g_final is not None
    slab = lambda i, s: _snake(i, jnp.minimum(s, nf - 1), nf)
    ple_col = lambda i, s: (layer, 0, _snake(i, jnp.clip(s - nf, 0, n_p - 1), n_p))
    in_specs = [
        _once((tm, d), lambda i, s: (i, 0)),
        pl.BlockSpec((tm, ss.shape[1]), lambda i, s: (i, 0)),
        pl.BlockSpec((None, d, tf), lambda i, s: (w_layer, 0, slab(i, s))),
        pl.BlockSpec((None, tf, d), lambda i, s: (w_layer, slab(i, s), 0)),
        pl.BlockSpec((tm, hw), lambda i, s: (i, jnp.minimum(s, n_pieces - 1))),
        _gain_spec(layer, d),
        _once((None, tm, ple_dim), lambda i, s: (layer, i, 0)),
        pl.BlockSpec((None, d, tn), ple_col),
        pl.BlockSpec((None, ple_dim, tn), ple_col),
    ]
    args = [u, ss, w_up, w_down, h, _gain3(g_ple), p, w_gate, w_ple_up]
    if final:
        in_specs.append(_gain_spec(0, d))
        args.append(g_final)
    return pl.pallas_call(
        functools.partial(_mlp_ple_kernel, nf=nf, cn=512, tn=tn, final=final),
        grid=(m // tm, nf + n_p),
        in_specs=in_specs,
        out_specs=_once((tm, d), lambda i, s: (i, 0)),
        out_shape=jax.ShapeDtypeStruct((m, d), F32),
        scratch_shapes=[pltpu.VMEM((tm, d), BF16), pltpu.VMEM((tm, 1), F32)],
        compiler_params=_params(("parallel", "arbitrary")),
        name="mlp_ple",
    )(*args)


def _tri_cumsum(tri, x, terms=3):
    total = None
    rest = x
    for n in range(terms):
        piece = rest.astype(BF16)
        if n + 1 < terms:
            rest = rest - piece.astype(F32)
        part = jnp.dot(tri, piece, preferred_element_type=F32)
        total = part if total is None else total + part
    return total


def _hgrn_block(qv, fv, iv, gv, lb, head_gain, st, causal, tri, *, chunk):
    mid = chunk // 2
    nt = (((1,), (1,)), ((), ()))
    tn = (((0,), (0,)), ((), ()))
    fg = lb + (1.0 - lb) * _sigmoid(fv)
    k_all = 1.0 - fg
    log2f = jnp.log(fg) * LOG2E
    qc = qv * _sigmoid(qv) * (HEAD_DIM ** -0.5)
    v_all = iv.astype(BF16)
    parts = []
    for u in range(qv.shape[0] // chunk):
        sl = slice(u * chunk, (u + 1) * chunk)
        b = _tri_cumsum(tri, log2f[sl], terms=3)
        b_mid = b[mid - 1:mid, :]
        b_last = b[chunk - 1:chunk, :]
        q_mid = qc[sl] * jnp.exp2(b - b_mid)
        k_mid = k_all[sl] * jnp.exp2(b_mid - b)
        q_in = (q_mid * jnp.exp2(b_mid)).astype(BF16)
        k_end = (k_mid * jnp.exp2(b_last - b_mid)).astype(BF16)
        att = lax.dot_general(q_mid.astype(BF16), k_mid.astype(BF16), nt, preferred_element_type=F32)
        att = jnp.where(causal, att, 0.0).astype(BF16)
        o_intra = jnp.dot(att, v_all[sl], preferred_element_type=F32)
        kv = lax.dot_general(v_all[sl], k_end, tn, preferred_element_type=F32)
        parts.append((q_in, o_intra, kv, jnp.exp2(b_last)))
    outs = []
    for q_in, o_intra, kv, dec in parts:
        outs.append(o_intra + lax.dot_general(q_in, st.astype(BF16), nt, preferred_element_type=F32))
        st = dec * st + kv
    o = jnp.concatenate(outs, axis=0)
    o = o * lax.rsqrt(jnp.mean(o * o, axis=-1, keepdims=True) + EPS) * head_gain
    return o * (gv * _sigmoid(gv)), st


def _hgrn_kernel(q_ref, f_ref, i_ref, g_ref, lbl_ref, hg_ref, *rest, chunk, unroll, layer, n_cast):
    o_ref, st_ref = rest[n_cast], rest[-1]
    _cast_slabs(rest[:n_cast], rest[n_cast + 1:-1])
    seq = q_ref.shape[0]
    blk = chunk * unroll
    n_heads = o_ref.shape[1] // HEAD_DIM
    lg = lbl_ref[...]
    e = jnp.exp(lg - jnp.max(lg, axis=0, keepdims=True))
    sm = e / jnp.sum(e, axis=0, keepdims=True)
    lb = jnp.sum(sm[:layer + 1], axis=0, keepdims=True)
    head_gain = hg_ref[...]
    row = lax.broadcasted_iota(jnp.int32, (chunk, chunk), 0)
    col = lax.broadcasted_iota(jnp.int32, (chunk, chunk), 1)
    causal = row >= col
    tri = jnp.where(causal, 1.0, 0.0).astype(BF16)

    st_ref[...] = jnp.zeros_like(st_ref)

    def body(c, carry):
        rows = pl.ds(pl.multiple_of(c * blk, blk), blk)
        for hd in range(n_heads):
            cols = slice(hd * HEAD_DIM, (hd + 1) * HEAD_DIM)
            o, st = _hgrn_block(q_ref[rows, cols], f_ref[rows, cols], i_ref[rows, cols], g_ref[rows, cols],
                                lb[:, cols], head_gain, st_ref[hd], causal, tri, chunk=chunk)
            st_ref[hd] = st
            o_ref[rows, cols] = o.astype(o_ref.dtype)
        return carry

    lax.fori_loop(0, seq // blk, body, 0)


def hgrn2(proj, lb_logits, head_gains, layer, cast=(), *, chunk, unroll, group):
    bsz, seq, d4 = proj.shape
    d = d4 // 4
    gw = group * HEAD_DIM
    ng = d // gw
    nl = lb_logits.shape[0]
    col = lambda part: pl.BlockSpec((None, seq, gw), lambda b, g: (b, 0, part * ng + g))
    c_in, c_out, c_shape = _cast_plan(cast, bsz * ng, lambda b, g: b * ng + g)
    outs = pl.pallas_call(
        functools.partial(_hgrn_kernel, chunk=chunk, unroll=unroll, layer=layer, n_cast=len(cast)),
        grid=(bsz, ng),
        in_specs=[
            col(0), col(1), col(2), col(3),
            pl.BlockSpec((nl, gw), lambda b, g: (0, g)),
            pl.BlockSpec((None, 1, HEAD_DIM), lambda b, g: (layer, 0, 0)),
        ] + c_in,
        out_specs=[pl.BlockSpec((None, seq, gw), lambda b, g: (b, 0, g))] + c_out,
        out_shape=[jax.ShapeDtypeStruct((bsz, seq, d), BF16)] + c_shape,
        scratch_shapes=[pltpu.VMEM((group, HEAD_DIM, HEAD_DIM), F32)],
        compiler_params=_params(("parallel", "parallel")),
        name="hgrn2",
    )(proj, proj, proj, proj, lb_logits, _gain3(head_gains), *[w for w, _ in cast])
    return outs


def _dcum_kernel(lg_ref, bf_ref, dcol_ref, drow_ref, *, blk, heads):
    seq = lg_ref.shape[0]
    row = lax.broadcasted_iota(jnp.int32, (blk, blk), 0)
    col = lax.broadcasted_iota(jnp.int32, (blk, blk), 1)
    tri = jnp.where(row >= col, 1.0, 0.0).astype(BF16)
    carry = jnp.zeros((1, lg_ref.shape[1]), F32)
    for n in range(seq // blk):
        rows = slice(n * blk, (n + 1) * blk)
        logit = lg_ref[rows, :] + bf_ref[...]
        logsig = jnp.minimum(logit, 0.0) - jnp.log(1.0 + jnp.exp(-jnp.abs(logit)))
        c = _tri_cumsum(tri, logsig) + carry
        dcol_ref[rows, :] = c
        carry = c[blk - 1:blk, :]
    drow_ref[...] = dcol_ref[...].T[:heads, :]


def forget_cumsum(logits, b_f, *, heads, blk):
    bsz, seq, lanes = logits.shape
    return pl.pallas_call(
        functools.partial(_dcum_kernel, blk=blk, heads=heads),
        grid=(bsz,),
        in_specs=[
            pl.BlockSpec((None, seq, lanes), lambda b: (b, 0, 0)),
            pl.BlockSpec((1, lanes), lambda b: (0, 0)),
        ],
        out_specs=[
            pl.BlockSpec((None, seq, lanes), lambda b: (b, 0, 0)),
            pl.BlockSpec((None, heads, seq), lambda b: (b, 0, 0)),
        ],
        out_shape=[
            jax.ShapeDtypeStruct((bsz, seq, lanes), F32),
            jax.ShapeDtypeStruct((bsz, heads, seq), F32),
        ],
        compiler_params=_params(("parallel",)),
        name="forget_cumsum",
    )(logits, b_f)


def _fox_kernel(q_ref, k_ref, v_ref, dcol_ref, drow_ref, *rest, tq, n_cast):
    o_ref = rest[n_cast]
    _cast_slabs(rest[:n_cast], rest[n_cast + 1:])
    seq = q_ref.shape[0]
    n_heads = o_ref.shape[1] // HEAD_DIM
    lane = lax.broadcasted_iota(jnp.int32, (tq, dcol_ref.shape[1]), 1)
    row = lax.broadcasted_iota(jnp.int32, (tq, tq), 0)
    col = lax.broadcasted_iota(jnp.int32, (tq, tq), 1)
    causal = row >= col
    nt = (((1,), (1,)), ((), ()))
    for hd in range(n_heads):
        head = pl.program_id(1) * n_heads + hd
        hc = slice(hd * HEAD_DIM, (hd + 1) * HEAD_DIM)
        for qi in range(seq // tq):
            rows = slice(qi * tq, (qi + 1) * tq)
            q = q_ref[rows, hc]
            dq = jnp.sum(jnp.where(lane == head, dcol_ref[rows, :], 0.0), axis=1, keepdims=True) * LOG2E
            m = jnp.full((tq, 1), -jnp.inf, F32)
            l = jnp.zeros((tq, 1), F32)
            acc = jnp.zeros((tq, HEAD_DIM), F32)
            for ki in range(qi + 1):
                cols = slice(ki * tq, (ki + 1) * tq)
                t = lax.dot_general(q, k_ref[cols, hc], nt, preferred_element_type=F32)
                t = t - drow_ref[hd:hd + 1, cols] * LOG2E
                if ki == qi:
                    t = jnp.where(causal, t, -jnp.inf)
                m_new = jnp.maximum(m, jnp.max(t, axis=1, keepdims=True) + dq)
                alpha = jnp.exp2(m - m_new)
                prob = jnp.exp2(t + (dq - m_new))
                l = alpha * l + jnp.sum(prob, axis=1, keepdims=True)
                acc = alpha * acc + jnp.dot(prob.astype(BF16), v_ref[cols, hc], preferred_element_type=F32)
                m = m_new
            o_ref[rows, hc] = (acc / l).astype(o_ref.dtype)


def fox_attention(kvq, dcol, drow, cast=(), *, tq, group):
    bsz, seq, d3 = kvq.shape
    d = d3 // 3
    gw = group * HEAD_DIM
    ng = d // gw
    col = lambda part: pl.BlockSpec((None, seq, gw), lambda b, g: (b, 0, part * ng + g))
    c_in, c_out, c_shape = _cast_plan(cast, bsz * ng, lambda b, g: b * ng + g)
    return pl.pallas_call(
        functools.partial(_fox_kernel, tq=tq, n_cast=len(cast)),
        grid=(bsz, ng),
        in_specs=[
            col(2), col(0), col(1),
            pl.BlockSpec((None, seq, dcol.shape[2]), lambda b, g: (b, 0, 0)),
            pl.BlockSpec((None, group, seq), lambda b, g: (b * ng + g, 0, 0)),
        ] + c_in,
        out_specs=[pl.BlockSpec((None, seq, gw), lambda b, g: (b, 0, g))] + c_out,
        out_shape=[jax.ShapeDtypeStruct((bsz, seq, d), BF16)] + c_shape,
        compiler_params=_params(("parallel", "parallel")),
        name="fox_attention",
    )(kvq, kvq, kvq, dcol, drow.reshape(bsz * ng, group, seq), *[w for w, _ in cast])


def kernel(x, p, mix_norm, mlp_norm, ple_norm, w_a_in, a_lb_logits, a_head_gain, w_a_out,
           kv_norm, w_kvf, b_f, w_b_q, w_b_out, w_mlp_up, w_mlp_down, w_ple_gate, w_ple_up,
           final_norm):
    bsz, seq, d = x.shape
    depth = p.shape[0]
    t = bsz * seq
    heads = d // HEAD_DIM
    assert depth == 2 and w_a_in.shape[0] == 1 and w_b_q.shape[0] == 1
    p3 = p.reshape(depth, t, p.shape[3])
    mix3 = _gain3(mix_norm)
    kv3 = kv_norm.reshape(1, 1, d)
    fin3 = final_norm.reshape(1, 1, d)
    tail = functools.partial(mlp_ple, **TILES["mlp_ple"])

    h = x.reshape(t, d)
    proj = norm_matmul(h, mix_norm, w_a_in, 0, out_dtype=F32, **TILES["in_proj"])
    mixed, w_up_b, w_dn_b = hgrn2(proj.reshape(bsz, seq, 4 * d), a_lb_logits, a_head_gain, 0,
                                  cast=[(w_mlp_up, 0), (w_mlp_down, 0)], **TILES["hgrn"])
    h, u, ss = matmul_residual(mixed.reshape(t, d), w_a_out, 0, h, mlp_norm, 0, **TILES["out_proj"])
    h = tail(u, ss, w_up_b, w_dn_b, 0, 0, h, ple_norm, p3, w_ple_gate, w_ple_up)

    w_kvf_t = w_kvf.T
    w_f_t = jnp.pad(w_kvf_t[2 * d:], ((0, LANES - heads), (0, 0)))
    b_fp = jnp.pad(b_f.astype(F32), (0, LANES - heads)).reshape(1, LANES)
    kvq, logits = kvq_projection(h, kv3, mix3, 1, w_kvf_t, w_b_q, 0, w_f_t, n_kv_cols=2 * d,
                                 q_scale=HEAD_DIM ** -0.5 * LOG2E, **TILES["kvq"])
    dcol, drow = forget_cumsum(logits.reshape(bsz, seq, LANES), b_fp, heads=heads, **TILES["dcum"])
    mixed, w_up_b, w_dn_b = fox_attention(kvq.reshape(bsz, seq, 3 * d), dcol, drow,
                                          cast=[(w_mlp_up, 1), (w_mlp_down, 1)], **TILES["attn"])
    h, u, ss = matmul_residual(mixed.reshape(t, d), w_b_out, 0, h, mlp_norm, 1, **TILES["out_proj"])
    out = tail(u, ss, w_up_b, w_dn_b, 0, 1, h, ple_norm, p3, w_ple_gate, w_ple_up, g_final=fin3)
    return out.reshape(bsz, seq, d)
```

```python
import functools

import jax
import jax.numpy as jnp
from jax import lax
from jax.experimental import pallas as pl
from jax.experimental.pallas import tpu as pltpu

F32 = jnp.float32
BF16 = jnp.bfloat16
EPS = 1e-6

LOG2E = 1.4426950408889634
HEAD_DIM = 128
LANES = 128
V7X_VMEM_LIMIT = 56 * 1024 * 1024
NORM_ROWS = 256

TILES = dict(
    in_proj=dict(tm=2048, tn=512),
    kvq=dict(tm=1024, tn=512),
    out_proj=dict(tm=2048, tn=512),
    mlp_ple=dict(tm=1024, tf=1024, tn=512),
    hgrn=dict(chunk=64, unroll=8, group=2),
    dcum=dict(blk=256),
    attn=dict(tq=512, group=2),
)


def _params(sem):
    return pltpu.CompilerParams(dimension_semantics=sem, vmem_limit_bytes=V7X_VMEM_LIMIT)


def _once(block_shape, index_map):
    return pl.BlockSpec(block_shape, index_map, pipeline_mode=pl.Buffered(1))


def _sigmoid(x):
    return 1.0 / (1.0 + jnp.exp(-x))


def _norm_rows(h_ref, gains, out_refs):
    rows = h_ref.shape[0]
    rc = min(NORM_ROWS, rows)

    def body(r, carry):
        r0 = pl.multiple_of(r * rc, rc)
        hb = h_ref[pl.ds(r0, rc), :].astype(F32)
        y = hb * lax.rsqrt(jnp.mean(hb * hb, axis=-1, keepdims=True) + EPS)
        for g, o_ref in zip(gains, out_refs):
            o_ref[pl.ds(r0, rc), :] = (y * g).astype(o_ref.dtype)
        return carry

    lax.fori_loop(0, rows // rc, body, 0)


def _dot(a, w):
    return jnp.dot(a, w.astype(BF16), preferred_element_type=F32)


def _gain3(g):
    return g.reshape(g.shape[0], 1, g.shape[1])


def _gain_spec(layer, d):
    return pl.BlockSpec((None, 1, d), lambda i, j: (layer, 0, 0))


def _snake(i, j, n):
    return jnp.where(i % 2 == 0, j, n - 1 - j)


def _cast_plan(weights, n_steps, step_of):
    in_specs, out_specs, out_shapes = [], [], []
    for w, layer in weights:
        _, r, c = w.shape
        rb = r // n_steps
        in_specs.append(pl.BlockSpec((None, rb, c), lambda *idx, l=layer: (l, step_of(*idx), 0)))
        out_specs.append(pl.BlockSpec((None, rb, c), lambda *idx: (0, step_of(*idx), 0)))
        out_shapes.append(jax.ShapeDtypeStruct((1, r, c), BF16))
    return in_specs, out_specs, out_shapes


def _cast_slabs(src_refs, dst_refs):
    for src_ref, dst_ref in zip(src_refs, dst_refs):
        dst_ref[...] = src_ref[...].astype(dst_ref.dtype)


def _norm_mm_kernel(x_ref, g_ref, w_ref, o_ref, xn_ref):
    @pl.when(pl.program_id(1) == 0)
    def _():
        _norm_rows(x_ref, [g_ref[...]], [xn_ref])

    o_ref[...] = _dot(xn_ref[...], w_ref[...]).astype(o_ref.dtype)


def norm_matmul(x, gains, w, layer, *, tm, tn, out_dtype):
    m, k = x.shape
    n = w.shape[2]
    return pl.pallas_call(
        _norm_mm_kernel,
        grid=(m // tm, n // tn),
        in_specs=[
            _once((tm, k), lambda i, j: (i, 0)),
            _gain_spec(layer, k),
            pl.BlockSpec((None, k, tn), lambda i, j: (layer, 0, _snake(i, j, n // tn))),
        ],
        out_specs=pl.BlockSpec((tm, tn), lambda i, j: (i, _snake(i, j, n // tn))),
        out_shape=jax.ShapeDtypeStruct((m, n), out_dtype),
        scratch_shapes=[pltpu.VMEM((tm, k), BF16)],
        compiler_params=_params(("parallel", "arbitrary")),
        name="norm_matmul",
    )(x, _gain3(gains), w)


def _kvq_kernel(h_ref, gkv_ref, gq_ref, wkv_ref, wq_ref, wf_ref, o_ref, lg_ref, u_ref, *, n_kv, q_scale):
    col = _snake(pl.program_id(0), pl.program_id(1), pl.num_programs(1))
    nt = (((1,), (1,)), ((), ()))

    @pl.when(pl.program_id(1) == 0)
    def _():
        _norm_rows(h_ref, [gkv_ref[...], gq_ref[...]], [u_ref.at[0], u_ref.at[1]])
        lg_ref[...] = lax.dot_general(u_ref[0], wf_ref[...].astype(BF16), nt, preferred_element_type=F32)

    @pl.when(col < n_kv)
    def _():
        acc = lax.dot_general(u_ref[0], wkv_ref[...].astype(BF16), nt, preferred_element_type=F32)
        o_ref[...] = acc.astype(o_ref.dtype)

    @pl.when(col >= n_kv)
    def _():
        o_ref[...] = (_dot(u_ref[1], wq_ref[...]) * q_scale).astype(o_ref.dtype)


def kvq_projection(h, g_kv, g_q, q_layer, w_kvf_t, w_q, wq_layer, w_f_t, *, n_kv_cols, q_scale, tm, tn):
    m, d = h.shape
    nq_cols = w_q.shape[2]
    n_kv, n_q = n_kv_cols // tn, nq_cols // tn
    col = lambda i, j: _snake(i, j, n_kv + n_q)
    lanes = w_f_t.shape[0]
    return pl.pallas_call(
        functools.partial(_kvq_kernel, n_kv=n_kv, q_scale=q_scale),
        grid=(m // tm, n_kv + n_q),
        in_specs=[
            pl.BlockSpec((tm, d), lambda i, j: (i, 0)),
            _gain_spec(0, d),
            _gain_spec(q_layer, d),
            pl.BlockSpec((tn, d), lambda i, j: (jnp.minimum(col(i, j), n_kv - 1), 0)),
            pl.BlockSpec((None, d, tn), lambda i, j: (wq_layer, 0, jnp.clip(col(i, j) - n_kv, 0, n_q - 1))),
            pl.BlockSpec((lanes, d), lambda i, j: (0, 0)),
        ],
        out_specs=[
            pl.BlockSpec((tm, tn), lambda i, j: (i, col(i, j))),
            pl.BlockSpec((tm, lanes), lambda i, j: (i, 0)),
        ],
        out_shape=[
            jax.ShapeDtypeStruct((m, n_kv_cols + nq_cols), BF16),
            jax.ShapeDtypeStruct((m, lanes), F32),
        ],
        scratch_shapes=[pltpu.VMEM((2, tm, d), BF16)],
        compiler_params=_params(("parallel", "arbitrary")),
        name="kvq_projection",
    )(h, g_kv, g_q, w_kvf_t, w_q, w_f_t)


def _mm_res_kernel(a_ref, w_ref, x_ref, g_ref, h_ref, hg_ref, ss_ref):
    h = x_ref[...] + _dot(a_ref[...], w_ref[...])
    h_ref[...] = h
    hg_ref[...] = (h * g_ref[...]).astype(hg_ref.dtype)
    sq = h * h
    part = sq[:, :LANES]
    for n in range(1, h.shape[1] // LANES):
        part = part + sq[:, n * LANES:(n + 1) * LANES]
    ss_ref[...] = part


def matmul_residual(a, w, layer, x, gains, g_layer, *, tm, tn):
    m, k = a.shape
    n = w.shape[2]
    col = lambda i, j: _snake(i, j, n // tn)
    return pl.pallas_call(
        _mm_res_kernel,
        grid=(m // tm, n // tn),
        in_specs=[
            pl.BlockSpec((tm, k), lambda i, j: (i, 0)),
            pl.BlockSpec((None, k, tn), lambda i, j: (layer, 0, col(i, j))),
            pl.BlockSpec((tm, tn), lambda i, j: (i, col(i, j))),
            pl.BlockSpec((None, 1, tn), lambda i, j: (g_layer, 0, col(i, j))),
        ],
        out_specs=[
            pl.BlockSpec((tm, tn), lambda i, j: (i, col(i, j))),
            pl.BlockSpec((tm, tn), lambda i, j: (i, col(i, j))),
            pl.BlockSpec((tm, LANES), lambda i, j: (i, col(i, j))),
        ],
        out_shape=[
            jax.ShapeDtypeStruct((m, n), F32),
            jax.ShapeDtypeStruct((m, n), BF16),
            jax.ShapeDtypeStruct((m, (n // tn) * LANES), F32),
        ],
        compiler_params=_params(("parallel", "arbitrary")),
        name="matmul_residual",
    )(a, w, x, _gain3(gains))


def _mlp_ple_kernel(u_ref, ss_ref, wup_ref, wdn_ref, h_ref, gp_ref, p_ref, wg_ref, wu_ref, *rest,
                    nf, cn, tn, final):
    if final:
        gf_ref, o_ref, u2_ref, r_ref = rest
    else:
        o_ref, u2_ref, r_ref = rest
    s = pl.program_id(1)
    d = o_ref.shape[1]
    hw = h_ref.shape[1]

    def mlp_slab(first):
        hid = jnp.maximum(_dot(u_ref[...], wup_ref[...]) * r_ref[...], 0.0)
        hid = (hid * hid).astype(BF16)
        for n in range(d // cn):
            cs = slice(n * cn, (n + 1) * cn)
            part = _dot(hid, wdn_ref[:, cs])
            o_ref[:, cs] = part if first else o_ref[:, cs] + part

    @pl.when(s == 0)
    def _():
        r_ref[...] = lax.rsqrt(jnp.sum(ss_ref[...], axis=1, keepdims=True) * (1.0 / d) + EPS)
        mlp_slab(first=True)

    @pl.when(jnp.logical_and(s > 0, s < nf))
    def _():
        mlp_slab(first=False)

    @pl.when(s < d // hw)
    def _():
        c0 = pl.multiple_of(s * hw, hw)
        o_ref[:, pl.ds(c0, hw)] += h_ref[...]

    @pl.when(s == nf - 1)
    def _():
        _norm_rows(o_ref, [gp_ref[...]], [u2_ref])

    @pl.when(s >= nf)
    def _():
        c0 = pl.multiple_of(_snake(pl.program_id(0), s - nf, d // tn) * tn, tn)
        p_b = p_ref[...].astype(BF16)
        half = tn // 2
        for n in range(2):
            cs = slice(n * half, (n + 1) * half)
            gate = _sigmoid(_dot(u2_ref[...], wg_ref[:, cs]))
            up = _dot(p_b, wu_ref[:, cs])
            o_ref[:, pl.ds(c0 + n * half, half)] += up * gate

    if final:
        @pl.when(s == pl.num_programs(1) - 1)
        def _():
            _norm_rows(o_ref, [gf_ref[...]], [o_ref])


def mlp_ple(u, ss, w_up, w_down, w_layer, layer, h, g_ple, p, w_gate, w_ple_up, g_final=None, *, tm, tf, tn):
    m, d = u.shape
    dff = w_up.shape[2]
    ple_dim = p.shape[2]
    nf, n_p = dff // tf, d // tn
    hw = max(LANES, d // nf)
    n_pieces = d // hw
    assert n_pieces <= nf
    final = g_final is not None
    slab = lambda i, s: _snake(i, jnp.minimum(s, nf - 1), nf)
    ple_col = lambda i, s: (layer, 0, _snake(i, jnp.clip(s - nf, 0, n_p - 1), n_p))
    in_specs = [
        _once((tm, d), lambda i, s: (i, 0)),
        pl.BlockSpec((tm, ss.shape[1]), lambda i, s: (i, 0)),
        pl.BlockSpec((None, d, tf), lambda i, s: (w_layer, 0, slab(i, s))),
        pl.BlockSpec((None, tf, d), lambda i, s: (w_layer, slab(i, s), 0)),
        pl.BlockSpec((tm, hw), lambda i, s: (i, jnp.minimum(s, n_pieces - 1))),
        _gain_spec(layer, d),
        _once((None, tm, ple_dim), lambda i, s: (layer, i, 0)),
        pl.BlockSpec((None, d, tn), ple_col),
        pl.BlockSpec((None, ple_dim, tn), ple_col),
    ]
    args = [u, ss, w_up, w_down, h, _gain3(g_ple), p, w_gate, w_ple_up]
    if final:
        in_specs.append(_gain_spec(0, d))
        args.append(g_final)
    return pl.pallas_call(
        functools.partial(_mlp_ple_kernel, nf=nf, cn=512, tn=tn, final=final),
        grid=(m // tm, nf + n_p),
        in_specs=in_specs,
        out_specs=_once((tm, d), lambda i, s: (i, 0)),
        out_shape=jax.ShapeDtypeStruct((m, d), F32),
        scratch_shapes=[pltpu.VMEM((tm, d), BF16), pltpu.VMEM((tm, 1), F32)],
        compiler_params=_params(("parallel", "arbitrary")),
        name="mlp_ple",
    )(*args)


def _tri_cumsum(tri, x, terms=3):
    total = None
    rest = x
    for n in range(terms):
        piece = rest.astype(BF16)
        if n + 1 < terms:
            rest = rest - piece.astype(F32)
        part = jnp.dot(tri, piece, preferred_element_type=F32)
        total = part if total is None else total + part
    return total


def _hgrn_block(qv, fv, iv, gv, lb, head_gain, st, causal, tri, *, chunk):
    mid = chunk // 2
    nt = (((1,), (1,)), ((), ()))
    tn = (((0,), (0,)), ((), ()))
    fg = lb + (1.0 - lb) * _sigmoid(fv)
    k_all = 1.0 - fg
    log2f = jnp.log(fg) * LOG2E
    qc = qv * _sigmoid(qv) * (HEAD_DIM ** -0.5)
    v_all = iv.astype(BF16)
    parts = []
    for u in range(qv.shape[0] // chunk):
        sl = slice(u * chunk, (u + 1) * chunk)
        b = _tri_cumsum(tri, log2f[sl], terms=3)
        b_mid = b[mid - 1:mid, :]
        b_last = b[chunk - 1:chunk, :]
        q_mid = qc[sl] * jnp.exp2(b - b_mid)
        k_mid = k_all[sl] * jnp.exp2(b_mid - b)
        q_in = (q_mid * jnp.exp2(b_mid)).astype(BF16)
        k_end = (k_mid * jnp.exp2(b_last - b_mid)).astype(BF16)
        att = lax.dot_general(q_mid.astype(BF16), k_mid.astype(BF16), nt, preferred_element_type=F32)
        att = jnp.where(causal, att, 0.0).astype(BF16)
        o_intra = jnp.dot(att, v_all[sl], preferred_element_type=F32)
        kv = lax.dot_general(v_all[sl], k_end, tn, preferred_element_type=F32)
        parts.append((q_in, o_intra, kv, jnp.exp2(b_last)))
    outs = []
    for q_in, o_intra, kv, dec in parts:
        outs.append(o_intra + lax.dot_general(q_in, st.astype(BF16), nt, preferred_element_type=F32))
        st = dec * st + kv
    o = jnp.concatenate(outs, axis=0)
    o = o * lax.rsqrt(jnp.mean(o * o, axis=-1, keepdims=True) + EPS) * head_gain
    return o * (gv * _sigmoid(gv)), st


def _hgrn_kernel(q_ref, f_ref, i_ref, g_ref, lbl_ref, hg_ref, *rest, chunk, unroll, layer, n_cast):
    o_ref, st_ref = rest[n_cast], rest[-1]
    _cast_slabs(rest[:n_cast], rest[n_cast + 1:-1])
    seq = q_ref.shape[0]
    blk = chunk * unroll
    n_heads = o_ref.shape[1] // HEAD_DIM
    lg = lbl_ref[...]
    e = jnp.exp(lg - jnp.max(lg, axis=0, keepdims=True))
    sm = e / jnp.sum(e, axis=0, keepdims=True)
    lb = jnp.sum(sm[:layer + 1], axis=0, keepdims=True)
    head_gain = hg_ref[...]
    row = lax.broadcasted_iota(jnp.int32, (chunk, chunk), 0)
    col = lax.broadcasted_iota(jnp.int32, (chunk, chunk), 1)
    causal = row >= col
    tri = jnp.where(causal, 1.0, 0.0).astype(BF16)

    st_ref[...] = jnp.zeros_like(st_ref)

    def body(c, carry):
        rows = pl.ds(pl.multiple_of(c * blk, blk), blk)
        for hd in range(n_heads):
            cols = slice(hd * HEAD_DIM, (hd + 1) * HEAD_DIM)
            o, st = _hgrn_block(q_ref[rows, cols], f_ref[rows, cols], i_ref[rows, cols], g_ref[rows, cols],
                                lb[:, cols], head_gain, st_ref[hd], causal, tri, chunk=chunk)
            st_ref[hd] = st
            o_ref[rows, cols] = o.astype(o_ref.dtype)
        return carry

    lax.fori_loop(0, seq // blk, body, 0)


def hgrn2(proj, lb_logits, head_gains, layer, cast=(), *, chunk, unroll, group):
    bsz, seq, d4 = proj.shape
    d = d4 // 4
    gw = group * HEAD_DIM
    ng = d // gw
    nl = lb_logits.shape[0]
    col = lambda part: pl.BlockSpec((None, seq, gw), lambda b, g: (b, 0, part * ng + g))
    c_in, c_out, c_shape = _cast_plan(cast, bsz * ng, lambda b, g: b * ng + g)
    outs = pl.pallas_call(
        functools.partial(_hgrn_kernel, chunk=chunk, unroll=unroll, layer=layer, n_cast=len(cast)),
        grid=(bsz, ng),
        in_specs=[
            col(0), col(1), col(2), col(3),
            pl.BlockSpec((nl, gw), lambda b, g: (0, g)),
            pl.BlockSpec((None, 1, HEAD_DIM), lambda b, g: (layer, 0, 0)),
        ] + c_in,
        out_specs=[pl.BlockSpec((None, seq, gw), lambda b, g: (b, 0, g))] + c_out,
        out_shape=[jax.ShapeDtypeStruct((bsz, seq, d), BF16)] + c_shape,
        scratch_shapes=[pltpu.VMEM((group, HEAD_DIM, HEAD_DIM), F32)],
        compiler_params=_params(("parallel", "parallel")),
        name="hgrn2",
    )(proj, proj, proj, proj, lb_logits, _gain3(head_gains), *[w for w, _ in cast])
    return outs


def _dcum_kernel(lg_ref, bf_ref, dcol_ref, drow_ref, *, blk, heads):
    seq = lg_ref.shape[0]
    row = lax.broadcasted_iota(jnp.int32, (blk, blk), 0)
    col = lax.broadcasted_iota(jnp.int32, (blk, blk), 1)
    tri = jnp.where(row >= col, 1.0, 0.0).astype(BF16)
    carry = jnp.zeros((1, lg_ref.shape[1]), F32)
    for n in range(seq // blk):
        rows = slice(n * blk, (n + 1) * blk)
        logit = lg_ref[rows, :] + bf_ref[...]
        logsig = jnp.minimum(logit, 0.0) - jnp.log(1.0 + jnp.exp(-jnp.abs(logit)))
        c = _tri_cumsum(tri, logsig) + carry
        dcol_ref[rows, :] = c
        carry = c[blk - 1:blk, :]
    drow_ref[...] = dcol_ref[...].T[:heads, :]


def forget_cumsum(logits, b_f, *, heads, blk):
    bsz, seq, lanes = logits.shape
    return pl.pallas_call(
        functools.partial(_dcum_kernel, blk=blk, heads=heads),
        grid=(bsz,),
        in_specs=[
            pl.BlockSpec((None, seq, lanes), lambda b: (b, 0, 0)),
            pl.BlockSpec((1, lanes), lambda b: (0, 0)),
        ],
        out_specs=[
            pl.BlockSpec((None, seq, lanes), lambda b: (b, 0, 0)),
            pl.BlockSpec((None, heads, seq), lambda b: (b, 0, 0)),
        ],
        out_shape=[
            jax.ShapeDtypeStruct((bsz, seq, lanes), F32),
            jax.ShapeDtypeStruct((bsz, heads, seq), F32),
        ],
        compiler_params=_params(("parallel",)),
        name="forget_cumsum",
    )(logits, b_f)


def _fox_kernel(q_ref, k_ref, v_ref, dcol_ref, drow_ref, *rest, tq, n_cast):
    o_ref = rest[n_cast]
    _cast_slabs(rest[:n_cast], rest[n_cast + 1:])
    seq = q_ref.shape[0]
    n_heads = o_ref.shape[1] // HEAD_DIM
    lane = lax.broadcasted_iota(jnp.int32, (tq, dcol_ref.shape[1]), 1)
    row = lax.broadcasted_iota(jnp.int32, (tq, tq), 0)
    col = lax.broadcasted_iota(jnp.int32, (tq, tq), 1)
    causal = row >= col
    nt = (((1,), (1,)), ((), ()))
    for hd in range(n_heads):
        head = pl.program_id(1) * n_heads + hd
        hc = slice(hd * HEAD_DIM, (hd + 1) * HEAD_DIM)
        for qi in range(seq // tq):
            rows = slice(qi * tq, (qi + 1) * tq)
            q = q_ref[rows, hc]
            dq = jnp.sum(jnp.where(lane == head, dcol_ref[rows, :], 0.0), axis=1, keepdims=True) * LOG2E
            m = jnp.full((tq, 1), -jnp.inf, F32)
            l = jnp.zeros((tq, 1), F32)
            acc = jnp.zeros((tq, HEAD_DIM), F32)
            for ki in range(qi + 1):
                cols = slice(ki * tq, (ki + 1) * tq)
                t = lax.dot_general(q, k_ref[cols, hc], nt, preferred_element_type=F32)
                t = t - drow_ref[hd:hd + 1, cols] * LOG2E
                if ki == qi:
                    t = jnp.where(causal, t, -jnp.inf)
                m_new = jnp.maximum(m, jnp.max(t, axis=1, keepdims=True) + dq)
                alpha = jnp.exp2(m - m_new)
                prob = jnp.exp2(t + (dq - m_new))
                l = alpha * l + jnp.sum(prob, axis=1, keepdims=True)
                acc = alpha * acc + jnp.dot(prob.astype(BF16), v_ref[cols, hc], preferred_element_type=F32)
                m = m_new
            o_ref[rows, hc] = (acc / l).astype(o_ref.dtype)


def fox_attention(kvq, dcol, drow, cast=(), *, tq, group):
    bsz, seq, d3 = kvq.shape
    d = d3 // 3
    gw = group * HEAD_DIM
    ng = d // gw
    col = lambda part: pl.BlockSpec((None, seq, gw), lambda b, g: (b, 0, part * ng + g))
    c_in, c_out, c_shape = _cast_plan(cast, bsz * ng, lambda b, g: b * ng + g)
    return pl.pallas_call(
        functools.partial(_fox_kernel, tq=tq, n_cast=len(cast)),
        grid=(bsz, ng),
        in_specs=[
            col(2), col(0), col(1),
            pl.BlockSpec((None, seq, dcol.shape[2]), lambda b, g: (b, 0, 0)),
            pl.BlockSpec((None, group, seq), lambda b, g: (b * ng + g, 0, 0)),
        ] + c_in,
        out_specs=[pl.BlockSpec((None, seq, gw), lambda b, g: (b, 0, g))] + c_out,
        out_shape=[jax.ShapeDtypeStruct((bsz, seq, d), BF16)] + c_shape,
        compiler_params=_params(("parallel", "parallel")),
        name="fox_attention",
    )(kvq, kvq, kvq, dcol, drow.reshape(bsz * ng, group, seq), *[w for w, _ in cast])


def kernel(x, p, mix_norm, mlp_norm, ple_norm, w_a_in, a_lb_logits, a_head_gain, w_a_out,
           kv_norm, w_kvf, b_f, w_b_q, w_b_out, w_mlp_up, w_mlp_down, w_ple_gate, w_ple_up,
           final_norm):
    bsz, seq, d = x.shape
    depth = p.shape[0]
    t = bsz * seq
    heads = d // HEAD_DIM
    assert depth == 2 and w_a_in.shape[0] == 1 and w_b_q.shape[0] == 1
    p3 = p.reshape(depth, t, p.shape[3])
    mix3 = _gain3(mix_norm)
    kv3 = kv_norm.reshape(1, 1, d)
    fin3 = final_norm.reshape(1, 1, d)
    tail = functools.partial(mlp_ple, **TILES["mlp_ple"])

    h = x.reshape(t, d)
    proj = norm_matmul(h, mix_norm, w_a_in, 0, out_dtype=F32, **TILES["in_proj"])
    mixed, w_up_b, w_dn_b = hgrn2(proj.reshape(bsz, seq, 4 * d), a_lb_logits, a_head_gain, 0,
                                  cast=[(w_mlp_up, 0), (w_mlp_down, 0)], **TILES["hgrn"])
    h, u, ss = matmul_residual(mixed.reshape(t, d), w_a_out, 0, h, mlp_norm, 0, **TILES["out_proj"])
    h = tail(u, ss, w_up_b, w_dn_b, 0, 0, h, ple_norm, p3, w_ple_gate, w_ple_up)

    w_kvf_t = w_kvf.T
    w_f_t = jnp.pad(w_kvf_t[2 * d:], ((0, LANES - heads), (0, 0)))
    b_fp = jnp.pad(b_f.astype(F32), (0, LANES - heads)).reshape(1, LANES)
    kvq, logits = kvq_projection(h, kv3, mix3, 1, w_kvf_t, w_b_q, 0, w_f_t, n_kv_cols=2 * d,
                                 q_scale=HEAD_DIM ** -0.5 * LOG2E, **TILES["kvq"])
    dcol, drow = forget_cumsum(logits.reshape(bsz, seq, LANES), b_fp, heads=heads, **TILES["dcum"])
    mixed, w_up_b, w_dn_b = fox_attention(kvq.reshape(bsz, seq, 3 * d), dcol, drow,
                                          cast=[(w_mlp_up, 1), (w_mlp_down, 1)], **TILES["attn"])
    h, u, ss = matmul_residual(mixed.reshape(t, d), w_b_out, 0, h, mlp_norm, 1, **TILES["out_proj"])
    out = tail(u, ss, w_up_b, w_dn_b, 0, 1, h, ple_norm, p3, w_ple_gate, w_ple_up, g_final=fin3)
    return out.reshape(bsz, seq, d)
```

```python
import functools

import jax
import jax.numpy as jnp
from jax import lax
from jax.experimental import pallas as pl
from jax.experimental.pallas import tpu as pltpu

F32 = jnp.float32
BF16 = jnp.bfloat16
EPS = 1e-6

LOG2E = 1.4426950408889634
HEAD_DIM = 128
LANES = 128
V7X_VMEM_LIMIT = 56 * 1024 * 1024
NORM_ROWS = 256

TILES = dict(
    in_proj=dict(tm=2048, tn=512),
    kvq=dict(tm=1024, tn=512),
    out_proj=dict(tm=2048, tn=512),
    mlp_ple=dict(tm=1024, tf=1024, tn=512),
    hgrn=dict(chunk=64, unroll=32, group=2),
    dcum=dict(blk=256),
    attn=dict(tq=512, group=2),
)


def _params(sem):
    return pltpu.CompilerParams(dimension_semantics=sem, vmem_limit_bytes=V7X_VMEM_LIMIT)


def _once(block_shape, index_map):
    return pl.BlockSpec(block_shape, index_map, pipeline_mode=pl.Buffered(1))


def _sigmoid(x):
    return 1.0 / (1.0 + jnp.exp(-x))


def _norm_rows(h_ref, gains, out_refs):
    rows = h_ref.shape[0]
    rc = min(NORM_ROWS, rows)

    def body(r, carry):
        r0 = pl.multiple_of(r * rc, rc)
        hb = h_ref[pl.ds(r0, rc), :].astype(F32)
        y = hb * lax.rsqrt(jnp.mean(hb * hb, axis=-1, keepdims=True) + EPS)
        for g, o_ref in zip(gains, out_refs):
            o_ref[pl.ds(r0, rc), :] = (y * g).astype(o_ref.dtype)
        return carry

    lax.fori_loop(0, rows // rc, body, 0)


def _dot(a, w):
    return jnp.dot(a, w.astype(BF16), preferred_element_type=F32)


def _gain3(g):
    return g.reshape(g.shape[0], 1, g.shape[1])


def _gain_spec(layer, d):
    return pl.BlockSpec((None, 1, d), lambda i, j: (layer, 0, 0))


def _snake(i, j, n):
    return jnp.where(i % 2 == 0, j, n - 1 - j)


def _cast_plan(weights, n_steps, step_of):
    in_specs, out_specs, out_shapes = [], [], []
    for w, layer in weights:
        _, r, c = w.shape
        rb = r // n_steps
        in_specs.append(pl.BlockSpec((None, rb, c), lambda *idx, l=layer: (l, step_of(*idx), 0)))
        out_specs.append(pl.BlockSpec((None, rb, c), lambda *idx: (0, step_of(*idx), 0)))
        out_shapes.append(jax.ShapeDtypeStruct((1, r, c), BF16))
    return in_specs, out_specs, out_shapes


def _cast_slabs(src_refs, dst_refs):
    for src_ref, dst_ref in zip(src_refs, dst_refs):
        dst_ref[...] = src_ref[...].astype(dst_ref.dtype)


def _norm_mm_kernel(x_ref, g_ref, w_ref, o_ref, xn_ref):
    @pl.when(pl.program_id(1) == 0)
    def _():
        _norm_rows(x_ref, [g_ref[...]], [xn_ref])

    o_ref[...] = _dot(xn_ref[...], w_ref[...]).astype(o_ref.dtype)


def norm_matmul(x, gains, w, layer, *, tm, tn, out_dtype):
    m, k = x.shape
    n = w.shape[2]
    return pl.pallas_call(
        _norm_mm_kernel,
        grid=(m // tm, n // tn),
        in_specs=[
            _once((tm, k), lambda i, j: (i, 0)),
            _gain_spec(layer, k),
            pl.BlockSpec((None, k, tn), lambda i, j: (layer, 0, _snake(i, j, n // tn))),
        ],
        out_specs=pl.BlockSpec((tm, tn), lambda i, j: (i, _snake(i, j, n // tn))),
        out_shape=jax.ShapeDtypeStruct((m, n), out_dtype),
        scratch_shapes=[pltpu.VMEM((tm, k), BF16)],
        compiler_params=_params(("parallel", "arbitrary")),
        name="norm_matmul",
    )(x, _gain3(gains), w)


def _kvq_kernel(h_ref, gkv_ref, gq_ref, wkv_ref, wq_ref, wf_ref, o_ref, lg_ref, u_ref, *, n_kv, q_scale):
    col = _snake(pl.program_id(0), pl.program_id(1), pl.num_programs(1))
    nt = (((1,), (1,)), ((), ()))

    @pl.when(pl.program_id(1) == 0)
    def _():
        _norm_rows(h_ref, [gkv_ref[...], gq_ref[...]], [u_ref.at[0], u_ref.at[1]])
        lg_ref[...] = lax.dot_general(u_ref[0], wf_ref[...].astype(BF16), nt, preferred_element_type=F32)

    @pl.when(col < n_kv)
    def _():
        acc = lax.dot_general(u_ref[0], wkv_ref[...].astype(BF16), nt, preferred_element_type=F32)
        o_ref[...] = acc.astype(o_ref.dtype)

    @pl.when(col >= n_kv)
    def _():
        o_ref[...] = (_dot(u_ref[1], wq_ref[...]) * q_scale).astype(o_ref.dtype)


def kvq_projection(h, g_kv, g_q, q_layer, w_kvf_t, w_q, wq_layer, w_f_t, *, n_kv_cols, q_scale, tm, tn):
    m, d = h.shape
    nq_cols = w_q.shape[2]
    n_kv, n_q = n_kv_cols // tn, nq_cols // tn
    col = lambda i, j: _snake(i, j, n_kv + n_q)
    lanes = w_f_t.shape[0]
    return pl.pallas_call(
        functools.partial(_kvq_kernel, n_kv=n_kv, q_scale=q_scale),
        grid=(m // tm, n_kv + n_q),
        in_specs=[
            pl.BlockSpec((tm, d), lambda i, j: (i, 0)),
            _gain_spec(0, d),
            _gain_spec(q_layer, d),
            pl.BlockSpec((tn, d), lambda i, j: (jnp.minimum(col(i, j), n_kv - 1), 0)),
            pl.BlockSpec((None, d, tn), lambda i, j: (wq_layer, 0, jnp.clip(col(i, j) - n_kv, 0, n_q - 1))),
            pl.BlockSpec((lanes, d), lambda i, j: (0, 0)),
        ],
        out_specs=[
            pl.BlockSpec((tm, tn), lambda i, j: (i, col(i, j))),
            pl.BlockSpec((tm, lanes), lambda i, j: (i, 0)),
        ],
        out_shape=[
            jax.ShapeDtypeStruct((m, n_kv_cols + nq_cols), BF16),
            jax.ShapeDtypeStruct((m, lanes), F32),
        ],
        scratch_shapes=[pltpu.VMEM((2, tm, d), BF16)],
        compiler_params=_params(("parallel", "arbitrary")),
        name="kvq_projection",
    )(h, g_kv, g_q, w_kvf_t, w_q, w_f_t)


def _mm_res_kernel(a_ref, w_ref, x_ref, g_ref, h_ref, hg_ref, ss_ref):
    h = x_ref[...] + _dot(a_ref[...], w_ref[...])
    h_ref[...] = h
    hg_ref[...] = (h * g_ref[...]).astype(hg_ref.dtype)
    sq = h * h
    part = sq[:, :LANES]
    for n in range(1, h.shape[1] // LANES):
        part = part + sq[:, n * LANES:(n + 1) * LANES]
    ss_ref[...] = part


def matmul_residual(a, w, layer, x, gains, g_layer, *, tm, tn):
    m, k = a.shape
    n = w.shape[2]
    col = lambda i, j: _snake(i, j, n // tn)
    return pl.pallas_call(
        _mm_res_kernel,
        grid=(m // tm, n // tn),
        in_specs=[
            pl.BlockSpec((tm, k), lambda i, j: (i, 0)),
            pl.BlockSpec((None, k, tn), lambda i, j: (layer, 0, col(i, j))),
            pl.BlockSpec((tm, tn), lambda i, j: (i, col(i, j))),
            pl.BlockSpec((None, 1, tn), lambda i, j: (g_layer, 0, col(i, j))),
        ],
        out_specs=[
            pl.BlockSpec((tm, tn), lambda i, j: (i, col(i, j))),
            pl.BlockSpec((tm, tn), lambda i, j: (i, col(i, j))),
            pl.BlockSpec((tm, LANES), lambda i, j: (i, col(i, j))),
        ],
        out_shape=[
            jax.ShapeDtypeStruct((m, n), F32),
            jax.ShapeDtypeStruct((m, n), BF16),
            jax.ShapeDtypeStruct((m, (n // tn) * LANES), F32),
        ],
        compiler_params=_params(("parallel", "arbitrary")),
        name="matmul_residual",
    )(a, w, x, _gain3(gains))


def _mlp_ple_kernel(u_ref, ss_ref, wup_ref, wdn_ref, h_ref, gp_ref, p_ref, wg_ref, wu_ref, *rest,
                    nf, cn, tn, final):
    if final:
        gf_ref, o_ref, u2_ref, r_ref = rest
    else:
        o_ref, u2_ref, r_ref = rest
    s = pl.program_id(1)
    d = o_ref.shape[1]
    hw = h_ref.shape[1]

    def mlp_slab(first):
        hid = jnp.maximum(_dot(u_ref[...], wup_ref[...]) * r_ref[...], 0.0)
        hid = (hid * hid).astype(BF16)
        for n in range(d // cn):
            cs = slice(n * cn, (n + 1) * cn)
            part = _dot(hid, wdn_ref[:, cs])
            o_ref[:, cs] = part if first else o_ref[:, cs] + part

    @pl.when(s == 0)
    def _():
        r_ref[...] = lax.rsqrt(jnp.sum(ss_ref[...], axis=1, keepdims=True) * (1.0 / d) + EPS)
        mlp_slab(first=True)

    @pl.when(jnp.logical_and(s > 0, s < nf))
    def _():
        mlp_slab(first=False)

    @pl.when(s < d // hw)
    def _():
        c0 = pl.multiple_of(s * hw, hw)
        o_ref[:, pl.ds(c0, hw)] += h_ref[...]

    @pl.when(s == nf - 1)
    def _():
        _norm_rows(o_ref, [gp_ref[...]], [u2_ref])

    @pl.when(s >= nf)
    def _():
        c0 = pl.multiple_of(_snake(pl.program_id(0), s - nf, d // tn) * tn, tn)
        p_b = p_ref[...].astype(BF16)
        half = tn // 2
        for n in range(2):
            cs = slice(n * half, (n + 1) * half)
            gate = _sigmoid(_dot(u2_ref[...], wg_ref[:, cs]))
            up = _dot(p_b, wu_ref[:, cs])
            o_ref[:, pl.ds(c0 + n * half, half)] += up * gate

    if final:
        @pl.when(s == pl.num_programs(1) - 1)
        def _():
            _norm_rows(o_ref, [gf_ref[...]], [o_ref])


def mlp_ple(u, ss, w_up, w_down, w_layer, layer, h, g_ple, p, w_gate, w_ple_up, g_final=None, *, tm, tf, tn):
    m, d = u.shape
    dff = w_up.shape[2]
    ple_dim = p.shape[2]
    nf, n_p = dff // tf, d // tn
    hw = max(LANES, d // nf)
    n_pieces = d // hw
    assert n_pieces <= nf
    final = g_final is not None
    slab = lambda i, s: _snake(i, jnp.minimum(s, nf - 1), nf)
    ple_col = lambda i, s: (layer, 0, _snake(i, jnp.clip(s - nf, 0, n_p - 1), n_p))
    in_specs = [
        _once((tm, d), lambda i, s: (i, 0)),
        pl.BlockSpec((tm, ss.shape[1]), lambda i, s: (i, 0)),
        pl.BlockSpec((None, d, tf), lambda i, s: (w_layer, 0, slab(i, s))),
        pl.BlockSpec((None, tf, d), lambda i, s: (w_layer, slab(i, s), 0)),
        pl.BlockSpec((tm, hw), lambda i, s: (i, jnp.minimum(s, n_pieces - 1))),
        _gain_spec(layer, d),
        _once((None, tm, ple_dim), lambda i, s: (layer, i, 0)),
        pl.BlockSpec((None, d, tn), ple_col),
        pl.BlockSpec((None, ple_dim, tn), ple_col),
    ]
    args = [u, ss, w_up, w_down, h, _gain3(g_ple), p, w_gate, w_ple_up]
    if final:
        in_specs.append(_gain_spec(0, d))
        args.append(g_final)
    return pl.pallas_call(
        functools.partial(_mlp_ple_kernel, nf=nf, cn=512, tn=tn, final=final),
        grid=(m // tm, nf + n_p),
        in_specs=in_specs,
        out_specs=_once((tm, d), lambda i, s: (i, 0)),
        out_shape=jax.ShapeDtypeStruct((m, d), F32),
        scratch_shapes=[pltpu.VMEM((tm, d), BF16), pltpu.VMEM((tm, 1), F32)],
        compiler_params=_params(("parallel", "arbitrary")),
        name="mlp_ple",
    )(*args)


def _tri_cumsum(tri, x, terms=3):
    total = None
    rest = x
    for n in range(terms):
        piece = rest.astype(BF16)
        if n + 1 < terms:
            rest = rest - piece.astype(F32)
        part = jnp.dot(tri, piece, preferred_element_type=F32)
        total = part if total is None else total + part
    return total


def _hgrn_block(qv, fv, iv, gv, lb, head_gain, st, causal, tri, *, chunk):
    mid = chunk // 2
    nt = (((1,), (1,)), ((), ()))
    tn = (((0,), (0,)), ((), ()))
    fg = lb + (1.0 - lb) * _sigmoid(fv)
    k_all = 1.0 - fg
    log2f = jnp.log(fg) * LOG2E
    qc = qv * _sigmoid(qv) * (HEAD_DIM ** -0.5)
    v_all = iv.astype(BF16)
    parts = []
    for u in range(qv.shape[0] // chunk):
        sl = slice(u * chunk, (u + 1) * chunk)
        b = _tri_cumsum(tri, log2f[sl], terms=3)
        b_mid = b[mid - 1:mid, :]
        b_last = b[chunk - 1:chunk, :]
        q_mid = qc[sl] * jnp.exp2(b - b_mid)
        k_mid = k_all[sl] * jnp.exp2(b_mid - b)
        q_in = (q_mid * jnp.exp2(b_mid)).astype(BF16)
        k_end = (k_mid * jnp.exp2(b_last - b_mid)).astype(BF16)
        att = lax.dot_general(q_mid.astype(BF16), k_mid.astype(BF16), nt, preferred_element_type=F32)
        att = jnp.where(causal, att, 0.0).astype(BF16)
        o_intra = jnp.dot(att, v_all[sl], preferred_element_type=F32)
        kv = lax.dot_general(v_all[sl], k_end, tn, preferred_element_type=F32)
        parts.append((q_in, o_intra, kv, jnp.exp2(b_last)))
    outs = []
    for q_in, o_intra, kv, dec in parts:
        outs.append(o_intra + lax.dot_general(q_in, st.astype(BF16), nt, preferred_element_type=F32))
        st = dec * st + kv
    o = jnp.concatenate(outs, axis=0)
    o = o * lax.rsqrt(jnp.mean(o * o, axis=-1, keepdims=True) + EPS) * head_gain
    return o * (gv * _sigmoid(gv)), st


def _hgrn_kernel(q_ref, f_ref, i_ref, g_ref, lbl_ref, hg_ref, *rest, chunk, unroll, layer, n_cast):
    o_ref, st_ref = rest[n_cast], rest[-1]
    _cast_slabs(rest[:n_cast], rest[n_cast + 1:-1])
    seq = q_ref.shape[0]
    blk = chunk * unroll
    n_heads = o_ref.shape[1] // HEAD_DIM
    lg = lbl_ref[...]
    e = jnp.exp(lg - jnp.max(lg, axis=0, keepdims=True))
    sm = e / jnp.sum(e, axis=0, keepdims=True)
    lb = jnp.sum(sm[:layer + 1], axis=0, keepdims=True)
    head_gain = hg_ref[...]
    row = lax.broadcasted_iota(jnp.int32, (chunk, chunk), 0)
    col = lax.broadcasted_iota(jnp.int32, (chunk, chunk), 1)
    causal = row >= col
    tri = jnp.where(causal, 1.0, 0.0).astype(BF16)

    st_ref[...] = jnp.zeros_like(st_ref)

    def body(c, carry):
        rows = pl.ds(pl.multiple_of(c * blk, blk), blk)
        for hd in range(n_heads):
            cols = slice(hd * HEAD_DIM, (hd + 1) * HEAD_DIM)
            o, st = _hgrn_block(q_ref[rows, cols], f_ref[rows, cols], i_ref[rows, cols], g_ref[rows, cols],
                                lb[:, cols], head_gain, st_ref[hd], causal, tri, chunk=chunk)
            st_ref[hd] = st
            o_ref[rows, cols] = o.astype(o_ref.dtype)
        return carry

    lax.fori_loop(0, seq // blk, body, 0)


def hgrn2(proj, lb_logits, head_gains, layer, cast=(), *, chunk, unroll, group):
    bsz, seq, d4 = proj.shape
    d = d4 // 4
    gw = group * HEAD_DIM
    ng = d // gw
    nl = lb_logits.shape[0]
    col = lambda part: pl.BlockSpec((None, seq, gw), lambda b, g: (b, 0, part * ng + g))
    c_in, c_out, c_shape = _cast_plan(cast, bsz * ng, lambda b, g: b * ng + g)
    outs = pl.pallas_call(
        functools.partial(_hgrn_kernel, chunk=chunk, unroll=unroll, layer=layer, n_cast=len(cast)),
        grid=(bsz, ng),
        in_specs=[
            col(0), col(1), col(2), col(3),
            pl.BlockSpec((nl, gw), lambda b, g: (0, g)),
            pl.BlockSpec((None, 1, HEAD_DIM), lambda b, g: (layer, 0, 0)),
        ] + c_in,
        out_specs=[pl.BlockSpec((None, seq, gw), lambda b, g: (b, 0, g))] + c_out,
        out_shape=[jax.ShapeDtypeStruct((bsz, seq, d), BF16)] + c_shape,
        scratch_shapes=[pltpu.VMEM((group, HEAD_DIM, HEAD_DIM), F32)],
        compiler_params=_params(("parallel", "parallel")),
        name="hgrn2",
    )(proj, proj, proj, proj, lb_logits, _gain3(head_gains), *[w for w, _ in cast])
    return outs


def _dcum_kernel(lg_ref, bf_ref, dcol_ref, drow_ref, *, blk, heads):
    seq = lg_ref.shape[0]
    row = lax.broadcasted_iota(jnp.int32, (blk, blk), 0)
    col = lax.broadcasted_iota(jnp.int32, (blk, blk), 1)
    tri = jnp.where(row >= col, 1.0, 0.0).astype(BF16)
    carry = jnp.zeros((1, lg_ref.shape[1]), F32)
    for n in range(seq // blk):
        rows = slice(n * blk, (n + 1) * blk)
        logit = lg_ref[rows, :] + bf_ref[...]
        logsig = jnp.minimum(logit, 0.0) - jnp.log(1.0 + jnp.exp(-jnp.abs(logit)))
        c = _tri_cumsum(tri, logsig) + carry
        dcol_ref[rows, :] = c
        carry = c[blk - 1:blk, :]
    drow_ref[...] = dcol_ref[...].T[:heads, :]


def forget_cumsum(logits, b_f, *, heads, blk):
    bsz, seq, lanes = logits.shape
    return pl.pallas_call(
        functools.partial(_dcum_kernel, blk=blk, heads=heads),
        grid=(bsz,),
        in_specs=[
            pl.BlockSpec((None, seq, lanes), lambda b: (b, 0, 0)),
            pl.BlockSpec((1, lanes), lambda b: (0, 0)),
        ],
        out_specs=[
            pl.BlockSpec((None, seq, lanes), lambda b: (b, 0, 0)),
            pl.BlockSpec((None, heads, seq), lambda b: (b, 0, 0)),
        ],
        out_shape=[
            jax.ShapeDtypeStruct((bsz, seq, lanes), F32),
            jax.ShapeDtypeStruct((bsz, heads, seq), F32),
        ],
        compiler_params=_params(("parallel",)),
        name="forget_cumsum",
    )(logits, b_f)


def _fox_kernel(q_ref, k_ref, v_ref, dcol_ref, drow_ref, *rest, tq, n_cast):
    o_ref = rest[n_cast]
    _cast_slabs(rest[:n_cast], rest[n_cast + 1:])
    seq = q_ref.shape[0]
    n_heads = o_ref.shape[1] // HEAD_DIM
    lane = lax.broadcasted_iota(jnp.int32, (tq, dcol_ref.shape[1]), 1)
    row = lax.broadcasted_iota(jnp.int32, (tq, tq), 0)
    col = lax.broadcasted_iota(jnp.int32, (tq, tq), 1)
    causal = row >= col
    nt = (((1,), (1,)), ((), ()))
    for hd in range(n_heads):
        head = pl.program_id(1) * n_heads + hd
        hc = slice(hd * HEAD_DIM, (hd + 1) * HEAD_DIM)
        for qi in range(seq // tq):
            rows = slice(qi * tq, (qi + 1) * tq)
            q = q_ref[rows, hc]
            dq = jnp.sum(jnp.where(lane == head, dcol_ref[rows, :], 0.0), axis=1, keepdims=True) * LOG2E
            m = jnp.full((tq, 1), -jnp.inf, F32)
            l = jnp.zeros((tq, 1), F32)
            acc = jnp.zeros((tq, HEAD_DIM), F32)
            for ki in range(qi + 1):
                cols = slice(ki * tq, (ki + 1) * tq)
                t = lax.dot_general(q, k_ref[cols, hc], nt, preferred_element_type=F32)
                t = t - drow_ref[hd:hd + 1, cols] * LOG2E
                if ki == qi:
                    t = jnp.where(causal, t, -jnp.inf)
                m_new = jnp.maximum(m, jnp.max(t, axis=1, keepdims=True) + dq)
                alpha = jnp.exp2(m - m_new)
                prob = jnp.exp2(t + (dq - m_new))
                l = alpha * l + jnp.sum(prob, axis=1, keepdims=True)
                acc = alpha * acc + jnp.dot(prob.astype(BF16), v_ref[cols, hc], preferred_element_type=F32)
                m = m_new
            o_ref[rows, hc] = (acc / l).astype(o_ref.dtype)


def fox_attention(kvq, dcol, drow, cast=(), *, tq, group):
    bsz, seq, d3 = kvq.shape
    d = d3 // 3
    gw = group * HEAD_DIM
    ng = d // gw
    col = lambda part: pl.BlockSpec((None, seq, gw), lambda b, g: (b, 0, part * ng + g))
    c_in, c_out, c_shape = _cast_plan(cast, bsz * ng, lambda b, g: b * ng + g)
    return pl.pallas_call(
        functools.partial(_fox_kernel, tq=tq, n_cast=len(cast)),
        grid=(bsz, ng),
        in_specs=[
            col(2), col(0), col(1),
            pl.BlockSpec((None, seq, dcol.shape[2]), lambda b, g: (b, 0, 0)),
            pl.BlockSpec((None, group, seq), lambda b, g: (b * ng + g, 0, 0)),
        ] + c_in,
        out_specs=[pl.BlockSpec((None, seq, gw), lambda b, g: (b, 0, g))] + c_out,
        out_shape=[jax.ShapeDtypeStruct((bsz, seq, d), BF16)] + c_shape,
        compiler_params=_params(("parallel", "parallel")),
        name="fox_attention",
    )(kvq, kvq, kvq, dcol, drow.reshape(bsz * ng, group, seq), *[w for w, _ in cast])


def kernel(x, p, mix_norm, mlp_norm, ple_norm, w_a_in, a_lb_logits, a_head_gain, w_a_out,
           kv_norm, w_kvf, b_f, w_b_q, w_b_out, w_mlp_up, w_mlp_down, w_ple_gate, w_ple_up,
           final_norm):
    bsz, seq, d = x.shape
    depth = p.shape[0]
    t = bsz * seq
    heads = d // HEAD_DIM
    assert depth == 2 and w_a_in.shape[0] == 1 and w_b_q.shape[0] == 1
    p3 = p.reshape(depth, t, p.shape[3])
    mix3 = _gain3(mix_norm)
    kv3 = kv_norm.reshape(1, 1, d)
    fin3 = final_norm.reshape(1, 1, d)
    tail = functools.partial(mlp_ple, **TILES["mlp_ple"])

    h = x.reshape(t, d)
    proj = norm_matmul(h, mix_norm, w_a_in, 0, out_dtype=F32, **TILES["in_proj"])
    mixed, w_up_b, w_dn_b = hgrn2(proj.reshape(bsz, seq, 4 * d), a_lb_logits, a_head_gain, 0,
                                  cast=[(w_mlp_up, 0), (w_mlp_down, 0)], **TILES["hgrn"])
    h, u, ss = matmul_residual(mixed.reshape(t, d), w_a_out, 0, h, mlp_norm, 0, **TILES["out_proj"])
    h = tail(u, ss, w_up_b, w_dn_b, 0, 0, h, ple_norm, p3, w_ple_gate, w_ple_up)

    w_kvf_t = w_kvf.T
    w_f_t = jnp.pad(w_kvf_t[2 * d:], ((0, LANES - heads), (0, 0)))
    b_fp = jnp.pad(b_f.astype(F32), (0, LANES - heads)).reshape(1, LANES)
    kvq, logits = kvq_projection(h, kv3, mix3, 1, w_kvf_t, w_b_q, 0, w_f_t, n_kv_cols=2 * d,
                                 q_scale=HEAD_DIM ** -0.5 * LOG2E, **TILES["kvq"])
    dcol, drow = forget_cumsum(logits.reshape(bsz, seq, LANES), b_fp, heads=heads, **TILES["dcum"])
    mixed, w_up_b, w_dn_b = fox_attention(kvq.reshape(bsz, seq, 3 * d), dcol, drow,
                                          cast=[(w_mlp_up, 1), (w_mlp_down, 1)], **TILES["attn"])
    h, u, ss = matmul_residual(mixed.reshape(t, d), w_b_out, 0, h, mlp_norm, 1, **TILES["out_proj"])
    out = tail(u, ss, w_up_b, w_dn_b, 0, 1, h, ple_norm, p3, w_ple_gate, w_ple_up, g_final=fin3)
    return out.reshape(bsz, seq, d)
```

```python
import functools

import jax
import jax.numpy as jnp
from jax import lax
from jax.experimental import pallas as pl
from jax.experimental.pallas import tpu as pltpu

F32 = jnp.float32
BF16 = jnp.bfloat16
EPS = 1e-6

LOG2E = 1.4426950408889634
HEAD_DIM = 128
LANES = 128
V7X_VMEM_LIMIT = 56 * 1024 * 1024
NORM_ROWS = 256

TILES = dict(
    in_proj=dict(tm=1024, tn=1024),
    kvq=dict(tm=1024, tn=512),
    out_proj=dict(tm=2048, tn=512),
    mlp_ple=dict(tm=1024, tf=1024, tn=512),
    hgrn=dict(chunk=64, unroll=32, group=2),
    dcum=dict(blk=256),
    attn=dict(tq=512, group=2),
)


def _params(sem):
    return pltpu.CompilerParams(dimension_semantics=sem, vmem_limit_bytes=V7X_VMEM_LIMIT)


def _once(block_shape, index_map):
    return pl.BlockSpec(block_shape, index_map, pipeline_mode=pl.Buffered(1))


def _sigmoid(x):
    return 1.0 / (1.0 + jnp.exp(-x))


def _norm_rows(h_ref, gains, out_refs):
    rows = h_ref.shape[0]
    rc = min(NORM_ROWS, rows)

    def body(r, carry):
        r0 = pl.multiple_of(r * rc, rc)
        hb = h_ref[pl.ds(r0, rc), :].astype(F32)
        y = hb * lax.rsqrt(jnp.mean(hb * hb, axis=-1, keepdims=True) + EPS)
        for g, o_ref in zip(gains, out_refs):
            o_ref[pl.ds(r0, rc), :] = (y * g).astype(o_ref.dtype)
        return carry

    lax.fori_loop(0, rows // rc, body, 0)


def _dot(a, w):
    return jnp.dot(a, w.astype(BF16), preferred_element_type=F32)


def _gain3(g):
    return g.reshape(g.shape[0], 1, g.shape[1])


def _gain_spec(layer, d):
    return pl.BlockSpec((None, 1, d), lambda i, j: (layer, 0, 0))


def _snake(i, j, n):
    return jnp.where(i % 2 == 0, j, n - 1 - j)


def _cast_plan(weights, n_steps, step_of):
    in_specs, out_specs, out_shapes = [], [], []
    for w, layer in weights:
        _, r, c = w.shape
        rb = r // n_steps
        in_specs.append(pl.BlockSpec((None, rb, c), lambda *idx, l=layer: (l, step_of(*idx), 0)))
        out_specs.append(pl.BlockSpec((None, rb, c), lambda *idx: (0, step_of(*idx), 0)))
        out_shapes.append(jax.ShapeDtypeStruct((1, r, c), BF16))
    return in_specs, out_specs, out_shapes


def _cast_slabs(src_refs, dst_refs):
    for src_ref, dst_ref in zip(src_refs, dst_refs):
        dst_ref[...] = src_ref[...].astype(dst_ref.dtype)


def _norm_mm_kernel(x_ref, g_ref, w_ref, o_ref, xn_ref):
    @pl.when(pl.program_id(1) == 0)
    def _():
        _norm_rows(x_ref, [g_ref[...]], [xn_ref])

    o_ref[...] = _dot(xn_ref[...], w_ref[...]).astype(o_ref.dtype)


def norm_matmul(x, gains, w, layer, *, tm, tn, out_dtype):
    m, k = x.shape
    n = w.shape[2]
    return pl.pallas_call(
        _norm_mm_kernel,
        grid=(m // tm, n // tn),
        in_specs=[
            pl.BlockSpec((tm, k), lambda i, j: (i, 0)),
            _gain_spec(layer, k),
            pl.BlockSpec((None, k, tn), lambda i, j: (layer, 0, _snake(i, j, n // tn))),
        ],
        out_specs=pl.BlockSpec((tm, tn), lambda i, j: (i, _snake(i, j, n // tn))),
        out_shape=jax.ShapeDtypeStruct((m, n), out_dtype),
        scratch_shapes=[pltpu.VMEM((tm, k), BF16)],
        compiler_params=_params(("parallel", "arbitrary")),
        name="norm_matmul",
    )(x, _gain3(gains), w)


def _kvq_kernel(h_ref, gkv_ref, gq_ref, wkv_ref, wq_ref, wf_ref, o_ref, lg_ref, u_ref, *, n_kv, q_scale):
    col = _snake(pl.program_id(0), pl.program_id(1), pl.num_programs(1))
    nt = (((1,), (1,)), ((), ()))

    @pl.when(pl.program_id(1) == 0)
    def _():
        _norm_rows(h_ref, [gkv_ref[...], gq_ref[...]], [u_ref.at[0], u_ref.at[1]])
        lg_ref[...] = lax.dot_general(u_ref[0], wf_ref[...].astype(BF16), nt, preferred_element_type=F32)

    @pl.when(col < n_kv)
    def _():
        acc = lax.dot_general(u_ref[0], wkv_ref[...].astype(BF16), nt, preferred_element_type=F32)
        o_ref[...] = acc.astype(o_ref.dtype)

    @pl.when(col >= n_kv)
    def _():
        o_ref[...] = (_dot(u_ref[1], wq_ref[...]) * q_scale).astype(o_ref.dtype)


def kvq_projection(h, g_kv, g_q, q_layer, w_kvf_t, w_q, wq_layer, w_f_t, *, n_kv_cols, q_scale, tm, tn):
    m, d = h.shape
    nq_cols = w_q.shape[2]
    n_kv, n_q = n_kv_cols // tn, nq_cols // tn
    col = lambda i, j: _snake(i, j, n_kv + n_q)
    lanes = w_f_t.shape[0]
    return pl.pallas_call(
        functools.partial(_kvq_kernel, n_kv=n_kv, q_scale=q_scale),
        grid=(m // tm, n_kv + n_q),
        in_specs=[
            pl.BlockSpec((tm, d), lambda i, j: (i, 0)),
            _gain_spec(0, d),
            _gain_spec(q_layer, d),
            pl.BlockSpec((tn, d), lambda i, j: (jnp.minimum(col(i, j), n_kv - 1), 0)),
            pl.BlockSpec((None, d, tn), lambda i, j: (wq_layer, 0, jnp.clip(col(i, j) - n_kv, 0, n_q - 1))),
            pl.BlockSpec((lanes, d), lambda i, j: (0, 0)),
        ],
        out_specs=[
            pl.BlockSpec((tm, tn), lambda i, j: (i, col(i, j))),
            pl.BlockSpec((tm, lanes), lambda i, j: (i, 0)),
        ],
        out_shape=[
            jax.ShapeDtypeStruct((m, n_kv_cols + nq_cols), BF16),
            jax.ShapeDtypeStruct((m, lanes), F32),
        ],
        scratch_shapes=[pltpu.VMEM((2, tm, d), BF16)],
        compiler_params=_params(("parallel", "arbitrary")),
        name="kvq_projection",
    )(h, g_kv, g_q, w_kvf_t, w_q, w_f_t)


def _mm_res_kernel(a_ref, w_ref, x_ref, g_ref, h_ref, hg_ref, ss_ref):
    h = x_ref[...] + _dot(a_ref[...], w_ref[...])
    h_ref[...] = h
    hg_ref[...] = (h * g_ref[...]).astype(hg_ref.dtype)
    sq = h * h
    part = sq[:, :LANES]
    for n in range(1, h.shape[1] // LANES):
        part = part + sq[:, n * LANES:(n + 1) * LANES]
    ss_ref[...] = part


def matmul_residual(a, w, layer, x, gains, g_layer, *, tm, tn):
    m, k = a.shape
    n = w.shape[2]
    col = lambda i, j: _snake(i, j, n // tn)
    return pl.pallas_call(
        _mm_res_kernel,
        grid=(m // tm, n // tn),
        in_specs=[
            pl.BlockSpec((tm, k), lambda i, j: (i, 0)),
            pl.BlockSpec((None, k, tn), lambda i, j: (layer, 0, col(i, j))),
            pl.BlockSpec((tm, tn), lambda i, j: (i, col(i, j))),
            pl.BlockSpec((None, 1, tn), lambda i, j: (g_layer, 0, col(i, j))),
        ],
        out_specs=[
            pl.BlockSpec((tm, tn), lambda i, j: (i, col(i, j))),
            pl.BlockSpec((tm, tn), lambda i, j: (i, col(i, j))),
            pl.BlockSpec((tm, LANES), lambda i, j: (i, col(i, j))),
        ],
        out_shape=[
            jax.ShapeDtypeStruct((m, n), F32),
            jax.ShapeDtypeStruct((m, n), BF16),
            jax.ShapeDtypeStruct((m, (n // tn) * LANES), F32),
        ],
        compiler_params=_params(("parallel", "arbitrary")),
        name="matmul_residual",
    )(a, w, x, _gain3(gains))


def _mlp_ple_kernel(u_ref, ss_ref, wup_ref, wdn_ref, h_ref, gp_ref, p_ref, wg_ref, wu_ref, *rest,
                    nf, cn, tn, final):
    if final:
        gf_ref, o_ref, u2_ref, r_ref = rest
    else:
        o_ref, u2_ref, r_ref = rest
    s = pl.program_id(1)
    d = o_ref.shape[1]
    hw = h_ref.shape[1]

    def mlp_slab(first):
        hid = jnp.maximum(_dot(u_ref[...], wup_ref[...]) * r_ref[...], 0.0)
        hid = (hid * hid).astype(BF16)
        for n in range(d // cn):
            cs = slice(n * cn, (n + 1) * cn)
            part = _dot(hid, wdn_ref[:, cs])
            o_ref[:, cs] = part if first else o_ref[:, cs] + part

    @pl.when(s == 0)
    def _():
        r_ref[...] = lax.rsqrt(jnp.sum(ss_ref[...], axis=1, keepdims=True) * (1.0 / d) + EPS)
        mlp_slab(first=True)

    @pl.when(jnp.logical_and(s > 0, s < nf))
    def _():
        mlp_slab(first=False)

    @pl.when(s < d // hw)
    def _():
        c0 = pl.multiple_of(s * hw, hw)
        o_ref[:, pl.ds(c0, hw)] += h_ref[...]

    @pl.when(s == nf - 1)
    def _():
        _norm_rows(o_ref, [gp_ref[...]], [u2_ref])

    @pl.when(s >= nf)
    def _():
        c0 = pl.multiple_of(_snake(pl.program_id(0), s - nf, d // tn) * tn, tn)
        p_b = p_ref[...].astype(BF16)
        half = tn // 2
        for n in range(2):
            cs = slice(n * half, (n + 1) * half)
            gate = _sigmoid(_dot(u2_ref[...], wg_ref[:, cs]))
            up = _dot(p_b, wu_ref[:, cs])
            o_ref[:, pl.ds(c0 + n * half, half)] += up * gate

    if final:
        @pl.when(s == pl.num_programs(1) - 1)
        def _():
            _norm_rows(o_ref, [gf_ref[...]], [o_ref])


def mlp_ple(u, ss, w_up, w_down, w_layer, layer, h, g_ple, p, w_gate, w_ple_up, g_final=None, *, tm, tf, tn):
    m, d = u.shape
    dff = w_up.shape[2]
    ple_dim = p.shape[2]
    nf, n_p = dff // tf, d // tn
    hw = max(LANES, d // nf)
    n_pieces = d // hw
    assert n_pieces <= nf
    final = g_final is not None
    slab = lambda i, s: _snake(i, jnp.minimum(s, nf - 1), nf)
    ple_col = lambda i, s: (layer, 0, _snake(i, jnp.clip(s - nf, 0, n_p - 1), n_p))
    in_specs = [
        _once((tm, d), lambda i, s: (i, 0)),
        pl.BlockSpec((tm, ss.shape[1]), lambda i, s: (i, 0)),
        pl.BlockSpec((None, d, tf), lambda i, s: (w_layer, 0, slab(i, s))),
        pl.BlockSpec((None, tf, d), lambda i, s: (w_layer, slab(i, s), 0)),
        pl.BlockSpec((tm, hw), lambda i, s: (i, jnp.minimum(s, n_pieces - 1))),
        _gain_spec(layer, d),
        _once((None, tm, ple_dim), lambda i, s: (layer, i, 0)),
        pl.BlockSpec((None, d, tn), ple_col),
        pl.BlockSpec((None, ple_dim, tn), ple_col),
    ]
    args = [u, ss, w_up, w_down, h, _gain3(g_ple), p, w_gate, w_ple_up]
    if final:
        in_specs.append(_gain_spec(0, d))
        args.append(g_final)
    return pl.pallas_call(
        functools.partial(_mlp_ple_kernel, nf=nf, cn=512, tn=tn, final=final),
        grid=(m // tm, nf + n_p),
        in_specs=in_specs,
        out_specs=_once((tm, d), lambda i, s: (i, 0)),
        out_shape=jax.ShapeDtypeStruct((m, d), F32),
        scratch_shapes=[pltpu.VMEM((tm, d), BF16), pltpu.VMEM((tm, 1), F32)],
        compiler_params=_params(("parallel", "arbitrary")),
        name="mlp_ple",
    )(*args)


def _tri_cumsum(tri, x, terms=3):
    total = None
    rest = x
    for n in range(terms):
        piece = rest.astype(BF16)
        if n + 1 < terms:
            rest = rest - piece.astype(F32)
        part = jnp.dot(tri, piece, preferred_element_type=F32)
        total = part if total is None else total + part
    return total


def _hgrn_block(qv, fv, iv, gv, lb, head_gain, st, causal, tri, *, chunk):
    mid = chunk // 2
    nt = (((1,), (1,)), ((), ()))
    tn = (((0,), (0,)), ((), ()))
    fg = lb + (1.0 - lb) * _sigmoid(fv)
    k_all = 1.0 - fg
    log2f = jnp.log(fg) * LOG2E
    qc = qv * _sigmoid(qv) * (HEAD_DIM ** -0.5)
    v_all = iv.astype(BF16)
    parts = []
    for u in range(qv.shape[0] // chunk):
        sl = slice(u * chunk, (u + 1) * chunk)
        b = _tri_cumsum(tri, log2f[sl], terms=3)
        b_mid = b[mid - 1:mid, :]
        b_last = b[chunk - 1:chunk, :]
        q_mid = qc[sl] * jnp.exp2(b - b_mid)
        k_mid = k_all[sl] * jnp.exp2(b_mid - b)
        q_in = (q_mid * jnp.exp2(b_mid)).astype(BF16)
        k_end = (k_mid * jnp.exp2(b_last - b_mid)).astype(BF16)
        att = lax.dot_general(q_mid.astype(BF16), k_mid.astype(BF16), nt, preferred_element_type=F32)
        att = jnp.where(causal, att, 0.0).astype(BF16)
        o_intra = jnp.dot(att, v_all[sl], preferred_element_type=F32)
        kv = lax.dot_general(v_all[sl], k_end, tn, preferred_element_type=F32)
        parts.append((q_in, o_intra, kv, jnp.exp2(b_last)))
    outs = []
    for q_in, o_intra, kv, dec in parts:
        outs.append(o_intra + lax.dot_general(q_in, st.astype(BF16), nt, preferred_element_type=F32))
        st = dec * st + kv
    o = jnp.concatenate(outs, axis=0)
    o = o * lax.rsqrt(jnp.mean(o * o, axis=-1, keepdims=True) + EPS) * head_gain
    return o * (gv * _sigmoid(gv)), st


def _hgrn_kernel(q_ref, f_ref, i_ref, g_ref, lbl_ref, hg_ref, *rest, chunk, unroll, layer, n_cast):
    o_ref, st_ref = rest[n_cast], rest[-1]
    _cast_slabs(rest[:n_cast], rest[n_cast + 1:-1])
    seq = q_ref.shape[0]
    blk = chunk * unroll
    n_heads = o_ref.shape[1] // HEAD_DIM
    lg = lbl_ref[...]
    e = jnp.exp(lg - jnp.max(lg, axis=0, keepdims=True))
    sm = e / jnp.sum(e, axis=0, keepdims=True)
    lb = jnp.sum(sm[:layer + 1], axis=0, keepdims=True)
    head_gain = hg_ref[...]
    row = lax.broadcasted_iota(jnp.int32, (chunk, chunk), 0)
    col = lax.broadcasted_iota(jnp.int32, (chunk, chunk), 1)
    causal = row >= col
    tri = jnp.where(causal, 1.0, 0.0).astype(BF16)

    st_ref[...] = jnp.zeros_like(st_ref)

    def body(c, carry):
        rows = pl.ds(pl.multiple_of(c * blk, blk), blk)
        for hd in range(n_heads):
            cols = slice(hd * HEAD_DIM, (hd + 1) * HEAD_DIM)
            o, st = _hgrn_block(q_ref[rows, cols], f_ref[rows, cols], i_ref[rows, cols], g_ref[rows, cols],
                                lb[:, cols], head_gain, st_ref[hd], causal, tri, chunk=chunk)
            st_ref[hd] = st
            o_ref[rows, cols] = o.astype(o_ref.dtype)
        return carry

    lax.fori_loop(0, seq // blk, body, 0)


def hgrn2(proj, lb_logits, head_gains, layer, cast=(), *, chunk, unroll, group):
    bsz, seq, d4 = proj.shape
    d = d4 // 4
    gw = group * HEAD_DIM
    ng = d // gw
    nl = lb_logits.shape[0]
    col = lambda part: pl.BlockSpec((None, seq, gw), lambda b, g: (b, 0, part * ng + g))
    c_in, c_out, c_shape = _cast_plan(cast, bsz * ng, lambda b, g: b * ng + g)
    outs = pl.pallas_call(
        functools.partial(_hgrn_kernel, chunk=chunk, unroll=unroll, layer=layer, n_cast=len(cast)),
        grid=(bsz, ng),
        in_specs=[
            col(0), col(1), col(2), col(3),
            pl.BlockSpec((nl, gw), lambda b, g: (0, g)),
            pl.BlockSpec((None, 1, HEAD_DIM), lambda b, g: (layer, 0, 0)),
        ] + c_in,
        out_specs=[pl.BlockSpec((None, seq, gw), lambda b, g: (b, 0, g))] + c_out,
        out_shape=[jax.ShapeDtypeStruct((bsz, seq, d), BF16)] + c_shape,
        scratch_shapes=[pltpu.VMEM((group, HEAD_DIM, HEAD_DIM), F32)],
        compiler_params=_params(("parallel", "parallel")),
        name="hgrn2",
    )(proj, proj, proj, proj, lb_logits, _gain3(head_gains), *[w for w, _ in cast])
    return outs


def _dcum_kernel(lg_ref, bf_ref, dcol_ref, drow_ref, *, blk, heads):
    seq = lg_ref.shape[0]
    row = lax.broadcasted_iota(jnp.int32, (blk, blk), 0)
    col = lax.broadcasted_iota(jnp.int32, (blk, blk), 1)
    tri = jnp.where(row >= col, 1.0, 0.0).astype(BF16)
    carry = jnp.zeros((1, lg_ref.shape[1]), F32)
    for n in range(seq // blk):
        rows = slice(n * blk, (n + 1) * blk)
        logit = lg_ref[rows, :] + bf_ref[...]
        logsig = jnp.minimum(logit, 0.0) - jnp.log(1.0 + jnp.exp(-jnp.abs(logit)))
        c = _tri_cumsum(tri, logsig) + carry
        dcol_ref[rows, :] = c
        carry = c[blk - 1:blk, :]
    drow_ref[...] = dcol_ref[...].T[:heads, :]


def forget_cumsum(logits, b_f, *, heads, blk):
    bsz, seq, lanes = logits.shape
    return pl.pallas_call(
        functools.partial(_dcum_kernel, blk=blk, heads=heads),
        grid=(bsz,),
        in_specs=[
            pl.BlockSpec((None, seq, lanes), lambda b: (b, 0, 0)),
            pl.BlockSpec((1, lanes), lambda b: (0, 0)),
        ],
        out_specs=[
            pl.BlockSpec((None, seq, lanes), lambda b: (b, 0, 0)),
            pl.BlockSpec((None, heads, seq), lambda b: (b, 0, 0)),
        ],
        out_shape=[
            jax.ShapeDtypeStruct((bsz, seq, lanes), F32),
            jax.ShapeDtypeStruct((bsz, heads, seq), F32),
        ],
        compiler_params=_params(("parallel",)),
        name="forget_cumsum",
    )(logits, b_f)


def _fox_kernel(q_ref, k_ref, v_ref, dcol_ref, drow_ref, *rest, tq, n_cast):
    o_ref = rest[n_cast]
    _cast_slabs(rest[:n_cast], rest[n_cast + 1:])
    seq = q_ref.shape[0]
    n_heads = o_ref.shape[1] // HEAD_DIM
    lane = lax.broadcasted_iota(jnp.int32, (tq, dcol_ref.shape[1]), 1)
    row = lax.broadcasted_iota(jnp.int32, (tq, tq), 0)
    col = lax.broadcasted_iota(jnp.int32, (tq, tq), 1)
    causal = row >= col
    nt = (((1,), (1,)), ((), ()))
    for hd in range(n_heads):
        head = pl.program_id(1) * n_heads + hd
        hc = slice(hd * HEAD_DIM, (hd + 1) * HEAD_DIM)
        for qi in range(seq // tq):
            rows = slice(qi * tq, (qi + 1) * tq)
            q = q_ref[rows, hc]
            dq = jnp.sum(jnp.where(lane == head, dcol_ref[rows, :], 0.0), axis=1, keepdims=True) * LOG2E
            m = jnp.full((tq, 1), -jnp.inf, F32)
            l = jnp.zeros((tq, 1), F32)
            acc = jnp.zeros((tq, HEAD_DIM), F32)
            for ki in range(qi + 1):
                cols = slice(ki * tq, (ki + 1) * tq)
                t = lax.dot_general(q, k_ref[cols, hc], nt, preferred_element_type=F32)
                t = t - drow_ref[hd:hd + 1, cols] * LOG2E
                if ki == qi:
                    t = jnp.where(causal, t, -jnp.inf)
                m_new = jnp.maximum(m, jnp.max(t, axis=1, keepdims=True) + dq)
                alpha = jnp.exp2(m - m_new)
                prob = jnp.exp2(t + (dq - m_new))
                l = alpha * l + jnp.sum(prob, axis=1, keepdims=True)
                acc = alpha * acc + jnp.dot(prob.astype(BF16), v_ref[cols, hc], preferred_element_type=F32)
                m = m_new
            o_ref[rows, hc] = (acc / l).astype(o_ref.dtype)


def fox_attention(kvq, dcol, drow, cast=(), *, tq, group):
    bsz, seq, d3 = kvq.shape
    d = d3 // 3
    gw = group * HEAD_DIM
    ng = d // gw
    col = lambda part: pl.BlockSpec((None, seq, gw), lambda b, g: (b, 0, part * ng + g))
    c_in, c_out, c_shape = _cast_plan(cast, bsz * ng, lambda b, g: b * ng + g)
    return pl.pallas_call(
        functools.partial(_fox_kernel, tq=tq, n_cast=len(cast)),
        grid=(bsz, ng),
        in_specs=[
            col(2), col(0), col(1),
            pl.BlockSpec((None, seq, dcol.shape[2]), lambda b, g: (b, 0, 0)),
            pl.BlockSpec((None, group, seq), lambda b, g: (b * ng + g, 0, 0)),
        ] + c_in,
        out_specs=[pl.BlockSpec((None, seq, gw), lambda b, g: (b, 0, g))] + c_out,
        out_shape=[jax.ShapeDtypeStruct((bsz, seq, d), BF16)] + c_shape,
        compiler_params=_params(("parallel", "parallel")),
        name="fox_attention",
    )(kvq, kvq, kvq, dcol, drow.reshape(bsz * ng, group, seq), *[w for w, _ in cast])


def kernel(x, p, mix_norm, mlp_norm, ple_norm, w_a_in, a_lb_logits, a_head_gain, w_a_out,
           kv_norm, w_kvf, b_f, w_b_q, w_b_out, w_mlp_up, w_mlp_down, w_ple_gate, w_ple_up,
           final_norm):
    bsz, seq, d = x.shape
    depth = p.shape[0]
    t = bsz * seq
    heads = d // HEAD_DIM
    assert depth == 2 and w_a_in.shape[0] == 1 and w_b_q.shape[0] == 1
    p3 = p.reshape(depth, t, p.shape[3])
    mix3 = _gain3(mix_norm)
    kv3 = kv_norm.reshape(1, 1, d)
    fin3 = final_norm.reshape(1, 1, d)
    tail = functools.partial(mlp_ple, **TILES["mlp_ple"])

    h = x.reshape(t, d)
    proj = norm_matmul(h, mix_norm, w_a_in, 0, out_dtype=F32, **TILES["in_proj"])
    mixed, w_up_b, w_dn_b = hgrn2(proj.reshape(bsz, seq, 4 * d), a_lb_logits, a_head_gain, 0,
                                  cast=[(w_mlp_up, 0), (w_mlp_down, 0)], **TILES["hgrn"])
    h, u, ss = matmul_residual(mixed.reshape(t, d), w_a_out, 0, h, mlp_norm, 0, **TILES["out_proj"])
    h = tail(u, ss, w_up_b, w_dn_b, 0, 0, h, ple_norm, p3, w_ple_gate, w_ple_up)

    w_kvf_t = w_kvf.T
    w_f_t = jnp.pad(w_kvf_t[2 * d:], ((0, LANES - heads), (0, 0)))
    b_fp = jnp.pad(b_f.astype(F32), (0, LANES - heads)).reshape(1, LANES)
    kvq, logits = kvq_projection(h, kv3, mix3, 1, w_kvf_t, w_b_q, 0, w_f_t, n_kv_cols=2 * d,
                                 q_scale=HEAD_DIM ** -0.5 * LOG2E, **TILES["kvq"])
    dcol, drow = forget_cumsum(logits.reshape(bsz, seq, LANES), b_fp, heads=heads, **TILES["dcum"])
    mixed, w_up_b, w_dn_b = fox_attention(kvq.reshape(bsz, seq, 3 * d), dcol, drow,
                                          cast=[(w_mlp_up, 1), (w_mlp_down, 1)], **TILES["attn"])
    h, u, ss = matmul_residual(mixed.reshape(t, d), w_b_out, 0, h, mlp_norm, 1, **TILES["out_proj"])
    out = tail(u, ss, w_up_b, w_dn_b, 0, 1, h, ple_norm, p3, w_ple_gate, w_ple_up, g_final=fin3)
    return out.reshape(bsz, seq, d)
```

```python
import functools

import jax
import jax.numpy as jnp
from jax import lax
from jax.experimental import pallas as pl
from jax.experimental.pallas import tpu as pltpu

F32 = jnp.float32
BF16 = jnp.bfloat16
EPS = 1e-6

LOG2E = 1.4426950408889634
HEAD_DIM = 128
LANES = 128
V7X_VMEM_LIMIT = 56 * 1024 * 1024
NORM_ROWS = 256

TILES = dict(
    in_proj=dict(tm=1024, tn=1024),
    kvq=dict(tm=1024, tn=512),
    out_proj=dict(tm=2048, tn=512),
    mlp_ple=dict(tm=1024, tf=1024, tn=512, cn=512),
    hgrn=dict(chunk=64, unroll=32, group=2),
    dcum=dict(blk=256),
    attn=dict(tq=512, group=2),
)


def _params(sem):
    return pltpu.CompilerParams(dimension_semantics=sem, vmem_limit_bytes=V7X_VMEM_LIMIT)


def _once(block_shape, index_map):
    return pl.BlockSpec(block_shape, index_map, pipeline_mode=pl.Buffered(1))


def _sigmoid(x):
    return 1.0 / (1.0 + jnp.exp(-x))


def _norm_rows(h_ref, gains, out_refs):
    rows = h_ref.shape[0]
    rc = min(NORM_ROWS, rows)

    def body(r, carry):
        r0 = pl.multiple_of(r * rc, rc)
        hb = h_ref[pl.ds(r0, rc), :]
        y = hb * lax.rsqrt(jnp.mean(hb * hb, axis=-1, keepdims=True) + EPS)
        for g, o_ref in zip(gains, out_refs):
            o_ref[pl.ds(r0, rc), :] = (y * g).astype(o_ref.dtype)
        return carry

    lax.fori_loop(0, rows // rc, body, 0)


def _dot(a, w):
    return jnp.dot(a, w.astype(BF16), preferred_element_type=F32)


def _gain3(g):
    return g.reshape(g.shape[0], 1, g.shape[1])


def _gain_spec(layer, d):
    return pl.BlockSpec((None, 1, d), lambda i, j: (layer, 0, 0))


def _snake(i, j, n):
    return jnp.where(i % 2 == 0, j, n - 1 - j)


def _cast_plan(weights, n_steps, step_of):
    in_specs, out_specs, out_shapes = [], [], []
    for w, layer in weights:
        _, r, c = w.shape
        assert r % n_steps == 0
        rb = r // n_steps
        in_specs.append(pl.BlockSpec((None, rb, c), lambda *idx, l=layer: (l, step_of(*idx), 0)))
        out_specs.append(pl.BlockSpec((None, rb, c), lambda *idx: (0, step_of(*idx), 0)))
        out_shapes.append(jax.ShapeDtypeStruct((1, r, c), BF16))
    return in_specs, out_specs, out_shapes


def _cast_slabs(src_refs, dst_refs):
    for src_ref, dst_ref in zip(src_refs, dst_refs):
        dst_ref[...] = src_ref[...].astype(dst_ref.dtype)


def _norm_mm_kernel(x_ref, g_ref, w_ref, o_ref, xn_ref):
    @pl.when(pl.program_id(1) == 0)
    def _():
        _norm_rows(x_ref, [g_ref[...]], [xn_ref])

    o_ref[...] = _dot(xn_ref[...], w_ref[...]).astype(o_ref.dtype)


def norm_matmul(x, gains, w, layer, *, tm, tn, out_dtype):
    m, k = x.shape
    n = w.shape[2]
    return pl.pallas_call(
        _norm_mm_kernel,
        grid=(m // tm, n // tn),
        in_specs=[
            pl.BlockSpec((tm, k), lambda i, j: (i, 0)),
            _gain_spec(layer, k),
            pl.BlockSpec((None, k, tn), lambda i, j: (layer, 0, _snake(i, j, n // tn))),
        ],
        out_specs=pl.BlockSpec((tm, tn), lambda i, j: (i, _snake(i, j, n // tn))),
        out_shape=jax.ShapeDtypeStruct((m, n), out_dtype),
        scratch_shapes=[pltpu.VMEM((tm, k), BF16)],
        compiler_params=_params(("parallel", "arbitrary")),
        name="norm_matmul",
    )(x, _gain3(gains), w)


def _kvq_kernel(h_ref, gkv_ref, gq_ref, wkv_ref, wq_ref, wf_ref, o_ref, lg_ref, u_ref, *, n_kv, q_scale):
    col = _snake(pl.program_id(0), pl.program_id(1), pl.num_programs(1))
    nt = (((1,), (1,)), ((), ()))

    @pl.when(pl.program_id(1) == 0)
    def _():
        _norm_rows(h_ref, [gkv_ref[...], gq_ref[...]], [u_ref.at[0], u_ref.at[1]])
        lg_ref[...] = lax.dot_general(u_ref[0], wf_ref[...].astype(BF16), nt, preferred_element_type=F32)

    @pl.when(col < n_kv)
    def _():
        acc = lax.dot_general(u_ref[0], wkv_ref[...].astype(BF16), nt, preferred_element_type=F32)
        o_ref[...] = acc.astype(o_ref.dtype)

    @pl.when(col >= n_kv)
    def _():
        o_ref[...] = (_dot(u_ref[1], wq_ref[...]) * q_scale).astype(o_ref.dtype)


def kvq_projection(h, g_kv, g_q, q_layer, w_kvf_t, w_q, wq_layer, w_f_t, *, n_kv_cols, q_scale, tm, tn):
    m, d = h.shape
    nq_cols = w_q.shape[2]
    n_kv, n_q = n_kv_cols // tn, nq_cols // tn
    col = lambda i, j: _snake(i, j, n_kv + n_q)
    lanes = w_f_t.shape[0]
    return pl.pallas_call(
        functools.partial(_kvq_kernel, n_kv=n_kv, q_scale=q_scale),
        grid=(m // tm, n_kv + n_q),
        in_specs=[
            pl.BlockSpec((tm, d), lambda i, j: (i, 0)),
            _gain_spec(0, d),
            _gain_spec(q_layer, d),
            pl.BlockSpec((tn, d), lambda i, j: (jnp.minimum(col(i, j), n_kv - 1), 0)),
            pl.BlockSpec((None, d, tn), lambda i, j: (wq_layer, 0, jnp.clip(col(i, j) - n_kv, 0, n_q - 1))),
            pl.BlockSpec((lanes, d), lambda i, j: (0, 0)),
        ],
        out_specs=[
            pl.BlockSpec((tm, tn), lambda i, j: (i, col(i, j))),
            pl.BlockSpec((tm, lanes), lambda i, j: (i, 0)),
        ],
        out_shape=[
            jax.ShapeDtypeStruct((m, n_kv_cols + nq_cols), BF16),
            jax.ShapeDtypeStruct((m, lanes), F32),
        ],
        scratch_shapes=[pltpu.VMEM((2, tm, d), BF16)],
        compiler_params=_params(("parallel", "arbitrary")),
        name="kvq_projection",
    )(h, g_kv, g_q, w_kvf_t, w_q, w_f_t)


def _mm_res_kernel(a_ref, w_ref, x_ref, g_ref, h_ref, hg_ref, ss_ref):
    h = x_ref[...] + _dot(a_ref[...], w_ref[...])
    h_ref[...] = h
    hg_ref[...] = (h * g_ref[...]).astype(hg_ref.dtype)
    sq = h * h
    part = sq[:, :LANES]
    for n in range(1, h.shape[1] // LANES):
        part = part + sq[:, n * LANES:(n + 1) * LANES]
    ss_ref[...] = part


def matmul_residual(a, w, layer, x, gains, g_layer, *, tm, tn):
    m, k = a.shape
    n = w.shape[2]
    col = lambda i, j: _snake(i, j, n // tn)
    return pl.pallas_call(
        _mm_res_kernel,
        grid=(m // tm, n // tn),
        in_specs=[
            pl.BlockSpec((tm, k), lambda i, j: (i, 0)),
            pl.BlockSpec((None, k, tn), lambda i, j: (layer, 0, col(i, j))),
            pl.BlockSpec((tm, tn), lambda i, j: (i, col(i, j))),
            pl.BlockSpec((None, 1, tn), lambda i, j: (g_layer, 0, col(i, j))),
        ],
        out_specs=[
            pl.BlockSpec((tm, tn), lambda i, j: (i, col(i, j))),
            pl.BlockSpec((tm, tn), lambda i, j: (i, col(i, j))),
            pl.BlockSpec((tm, LANES), lambda i, j: (i, col(i, j))),
        ],
        out_shape=[
            jax.ShapeDtypeStruct((m, n), F32),
            jax.ShapeDtypeStruct((m, n), BF16),
            jax.ShapeDtypeStruct((m, (n // tn) * LANES), F32),
        ],
        compiler_params=_params(("parallel", "arbitrary")),
        name="matmul_residual",
    )(a, w, x, _gain3(gains))


def _mlp_ple_kernel(u_ref, ss_ref, wup_ref, wdn_ref, h_ref, gp_ref, p_ref, wg_ref, wu_ref, *rest,
                    nf, cn, tn, final):
    if final:
        gf_ref, o_ref, u2_ref, r_ref = rest
    else:
        o_ref, u2_ref, r_ref = rest
    s = pl.program_id(1)
    d = o_ref.shape[1]
    hw = h_ref.shape[1]

    def mlp_slab(first):
        hid = jnp.maximum(_dot(u_ref[...], wup_ref[...]) * r_ref[...], 0.0)
        hid = (hid * hid).astype(BF16)
        for n in range(d // cn):
            cs = slice(n * cn, (n + 1) * cn)
            part = _dot(hid, wdn_ref[:, cs])
            o_ref[:, cs] = part if first else o_ref[:, cs] + part

    @pl.when(s == 0)
    def _():
        r_ref[...] = lax.rsqrt(jnp.sum(ss_ref[...], axis=1, keepdims=True) * (1.0 / d) + EPS)
        mlp_slab(first=True)

    @pl.when(jnp.logical_and(s > 0, s < nf))
    def _():
        mlp_slab(first=False)

    @pl.when(s < d // hw)
    def _():
        c0 = pl.multiple_of(s * hw, hw)
        o_ref[:, pl.ds(c0, hw)] += h_ref[...]

    @pl.when(s == nf - 1)
    def _():
        _norm_rows(o_ref, [gp_ref[...]], [u2_ref])

    @pl.when(s >= nf)
    def _():
        c0 = pl.multiple_of(_snake(pl.program_id(0), s - nf, d // tn) * tn, tn)
        p_b = p_ref[...].astype(BF16)
        half = tn // 2
        for n in range(2):
            cs = slice(n * half, (n + 1) * half)
            gate = _sigmoid(_dot(u2_ref[...], wg_ref[:, cs]))
            up = _dot(p_b, wu_ref[:, cs])
            o_ref[:, pl.ds(c0 + n * half, half)] += up * gate

    if final:
        @pl.when(s == pl.num_programs(1) - 1)
        def _():
            _norm_rows(o_ref, [gf_ref[...]], [o_ref])


def mlp_ple(u, ss, w_up, w_down, w_layer, layer, h, g_ple, p, w_gate, w_ple_up, g_final=None,
            *, tm, tf, tn, cn):
    m, d = u.shape
    dff = w_up.shape[2]
    ple_dim = p.shape[2]
    nf, n_p = dff // tf, d // tn
    hw = max(LANES, d // nf)
    n_pieces = d // hw
    assert n_pieces <= nf
    final = g_final is not None
    slab = lambda i, s: _snake(i, jnp.minimum(s, nf - 1), nf)
    ple_col = lambda i, s: (layer, 0, _snake(i, jnp.clip(s - nf, 0, n_p - 1), n_p))
    in_specs = [
        _once((tm, d), lambda i, s: (i, 0)),
        pl.BlockSpec((tm, ss.shape[1]), lambda i, s: (i, 0)),
        pl.BlockSpec((None, d, tf), lambda i, s: (w_layer, 0, slab(i, s))),
        pl.BlockSpec((None, tf, d), lambda i, s: (w_layer, slab(i, s), 0)),
        pl.BlockSpec((tm, hw), lambda i, s: (i, jnp.minimum(s, n_pieces - 1))),
        _gain_spec(layer, d),
        _once((None, tm, ple_dim), lambda i, s: (layer, i, 0)),
        pl.BlockSpec((None, d, tn), ple_col),
        pl.BlockSpec((None, ple_dim, tn), ple_col),
    ]
    args = [u, ss, w_up, w_down, h, _gain3(g_ple), p, w_gate, w_ple_up]
    if final:
        in_specs.append(_gain_spec(0, d))
        args.append(g_final)
    return pl.pallas_call(
        functools.partial(_mlp_ple_kernel, nf=nf, cn=cn, tn=tn, final=final),
        grid=(m // tm, nf + n_p),
        in_specs=in_specs,
        out_specs=_once((tm, d), lambda i, s: (i, 0)),
        out_shape=jax.ShapeDtypeStruct((m, d), F32),
        scratch_shapes=[pltpu.VMEM((tm, d), BF16), pltpu.VMEM((tm, 1), F32)],
        compiler_params=_params(("parallel", "arbitrary")),
        name="mlp_ple",
    )(*args)


def _tri_cumsum(tri, x, terms=3):
    total = None
    rest = x
    for n in range(terms):
        piece = rest.astype(BF16)
        if n + 1 < terms:
            rest = rest - piece.astype(F32)
        part = jnp.dot(tri, piece, preferred_element_type=F32)
        total = part if total is None else total + part
    return total


def _hgrn_block(qv, fv, iv, gv, lb, head_gain, st, causal, tri, *, chunk):
    mid = chunk // 2
    nt = (((1,), (1,)), ((), ()))
    tn = (((0,), (0,)), ((), ()))
    fg = lb + (1.0 - lb) * _sigmoid(fv)
    k_all = 1.0 - fg
    log2f = jnp.log(fg) * LOG2E
    qc = qv * _sigmoid(qv) * (HEAD_DIM ** -0.5)
    v_all = iv.astype(BF16)
    parts = []
    for u in range(qv.shape[0] // chunk):
        sl = slice(u * chunk, (u + 1) * chunk)
        b = _tri_cumsum(tri, log2f[sl], terms=3)
        b_mid = b[mid - 1:mid, :]
        b_last = b[chunk - 1:chunk, :]
        q_mid = qc[sl] * jnp.exp2(b - b_mid)
        k_mid = k_all[sl] * jnp.exp2(b_mid - b)
        q_in = (q_mid * jnp.exp2(b_mid)).astype(BF16)
        k_end = (k_mid * jnp.exp2(b_last - b_mid)).astype(BF16)
        att = lax.dot_general(q_mid.astype(BF16), k_mid.astype(BF16), nt, preferred_element_type=F32)
        att = jnp.where(causal, att, 0.0).astype(BF16)
        o_intra = jnp.dot(att, v_all[sl], preferred_element_type=F32)
        kv = lax.dot_general(v_all[sl], k_end, tn, preferred_element_type=F32)
        parts.append((q_in, o_intra, kv, jnp.exp2(b_last)))
    outs = []
    for q_in, o_intra, kv, dec in parts:
        outs.append(o_intra + lax.dot_general(q_in, st.astype(BF16), nt, preferred_element_type=F32))
        st = dec * st + kv
    o = jnp.concatenate(outs, axis=0)
    o = o * lax.rsqrt(jnp.mean(o * o, axis=-1, keepdims=True) + EPS) * head_gain
    return o * (gv * _sigmoid(gv)), st


def _hgrn_kernel(q_ref, f_ref, i_ref, g_ref, lbl_ref, hg_ref, *rest, chunk, unroll, layer, n_cast):
    o_ref, st_ref = rest[n_cast], rest[-1]
    _cast_slabs(rest[:n_cast], rest[n_cast + 1:-1])
    seq = q_ref.shape[0]
    blk = chunk * unroll
    n_heads = o_ref.shape[1] // HEAD_DIM
    lg = lbl_ref[...]
    e = jnp.exp(lg - jnp.max(lg, axis=0, keepdims=True))
    sm = e / jnp.sum(e, axis=0, keepdims=True)
    lb = jnp.sum(sm[:layer + 1], axis=0, keepdims=True)
    head_gain = hg_ref[...]
    row = lax.broadcasted_iota(jnp.int32, (chunk, chunk), 0)
    col = lax.broadcasted_iota(jnp.int32, (chunk, chunk), 1)
    causal = row >= col
    tri = jnp.where(causal, 1.0, 0.0).astype(BF16)

    st_ref[...] = jnp.zeros_like(st_ref)

    def body(c, carry):
        rows = pl.ds(pl.multiple_of(c * blk, blk), blk)
        for hd in range(n_heads):
            cols = slice(hd * HEAD_DIM, (hd + 1) * HEAD_DIM)
            o, st = _hgrn_block(q_ref[rows, cols], f_ref[rows, cols], i_ref[rows, cols], g_ref[rows, cols],
                                lb[:, cols], head_gain, st_ref[hd], causal, tri, chunk=chunk)
            st_ref[hd] = st
            o_ref[rows, cols] = o.astype(o_ref.dtype)
        return carry

    lax.fori_loop(0, seq // blk, body, 0)


def hgrn2(proj, lb_logits, head_gains, layer, cast=(), *, chunk, unroll, group):
    bsz, seq, d4 = proj.shape
    d = d4 // 4
    gw = group * HEAD_DIM
    ng = d // gw
    nl = lb_logits.shape[0]
    col = lambda part: pl.BlockSpec((None, seq, gw), lambda b, g: (b, 0, part * ng + g))
    c_in, c_out, c_shape = _cast_plan(cast, bsz * ng, lambda b, g: b * ng + g)
    outs = pl.pallas_call(
        functools.partial(_hgrn_kernel, chunk=chunk, unroll=unroll, layer=layer, n_cast=len(cast)),
        grid=(bsz, ng),
        in_specs=[
            col(0), col(1), col(2), col(3),
            pl.BlockSpec((nl, gw), lambda b, g: (0, g)),
            pl.BlockSpec((None, 1, HEAD_DIM), lambda b, g: (layer, 0, 0)),
        ] + c_in,
        out_specs=[pl.BlockSpec((None, seq, gw), lambda b, g: (b, 0, g))] + c_out,
        out_shape=[jax.ShapeDtypeStruct((bsz, seq, d), BF16)] + c_shape,
        scratch_shapes=[pltpu.VMEM((group, HEAD_DIM, HEAD_DIM), F32)],
        compiler_params=_params(("parallel", "parallel")),
        name="hgrn2",
    )(proj, proj, proj, proj, lb_logits, _gain3(head_gains), *[w for w, _ in cast])
    return outs


def _dcum_kernel(lg_ref, bf_ref, dcol_ref, drow_ref, *, blk, heads):
    seq = lg_ref.shape[0]
    row = lax.broadcasted_iota(jnp.int32, (blk, blk), 0)
    col = lax.broadcasted_iota(jnp.int32, (blk, blk), 1)
    tri = jnp.where(row >= col, 1.0, 0.0).astype(BF16)
    carry = jnp.zeros((1, lg_ref.shape[1]), F32)
    for n in range(seq // blk):
        rows = slice(n * blk, (n + 1) * blk)
        logit = lg_ref[rows, :] + bf_ref[...]
        logsig = jnp.minimum(logit, 0.0) - jnp.log(1.0 + jnp.exp(-jnp.abs(logit)))
        c = _tri_cumsum(tri, logsig) + carry
        dcol_ref[rows, :] = c
        carry = c[blk - 1:blk, :]
    drow_ref[...] = dcol_ref[...].T[:heads, :]


def forget_cumsum(logits, b_f, *, heads, blk):
    bsz, seq, lanes = logits.shape
    return pl.pallas_call(
        functools.partial(_dcum_kernel, blk=blk, heads=heads),
        grid=(bsz,),
        in_specs=[
            pl.BlockSpec((None, seq, lanes), lambda b: (b, 0, 0)),
            pl.BlockSpec((1, lanes), lambda b: (0, 0)),
        ],
        out_specs=[
            pl.BlockSpec((None, seq, lanes), lambda b: (b, 0, 0)),
            pl.BlockSpec((None, heads, seq), lambda b: (b, 0, 0)),
        ],
        out_shape=[
            jax.ShapeDtypeStruct((bsz, seq, lanes), F32),
            jax.ShapeDtypeStruct((bsz, heads, seq), F32),
        ],
        compiler_params=_params(("parallel",)),
        name="forget_cumsum",
    )(logits, b_f)


def _fox_kernel(q_ref, k_ref, v_ref, dcol_ref, drow_ref, *rest, tq, n_cast):
    o_ref = rest[n_cast]
    _cast_slabs(rest[:n_cast], rest[n_cast + 1:])
    seq = q_ref.shape[0]
    n_heads = o_ref.shape[1] // HEAD_DIM
    lane = lax.broadcasted_iota(jnp.int32, (tq, dcol_ref.shape[1]), 1)
    row = lax.broadcasted_iota(jnp.int32, (tq, tq), 0)
    col = lax.broadcasted_iota(jnp.int32, (tq, tq), 1)
    causal = row >= col
    nt = (((1,), (1,)), ((), ()))
    for hd in range(n_heads):
        head = pl.program_id(1) * n_heads + hd
        hc = slice(hd * HEAD_DIM, (hd + 1) * HEAD_DIM)
        for qi in range(seq // tq):
            rows = slice(qi * tq, (qi + 1) * tq)
            q = q_ref[rows, hc]
            dq = jnp.sum(jnp.where(lane == head, dcol_ref[rows, :], 0.0), axis=1, keepdims=True) * LOG2E
            m = jnp.full((tq, 1), -jnp.inf, F32)
            l = jnp.zeros((tq, 1), F32)
            acc = jnp.zeros((tq, HEAD_DIM), F32)
            for ki in range(qi + 1):
                cols = slice(ki * tq, (ki + 1) * tq)
                t = lax.dot_general(q, k_ref[cols, hc], nt, preferred_element_type=F32)
                t = t - drow_ref[hd:hd + 1, cols] * LOG2E
                if ki == qi:
                    t = jnp.where(causal, t, -jnp.inf)
                m_new = jnp.maximum(m, jnp.max(t, axis=1, keepdims=True) + dq)
                alpha = jnp.exp2(m - m_new)
                prob = jnp.exp2(t + (dq - m_new))
                l = alpha * l + jnp.sum(prob, axis=1, keepdims=True)
                acc = alpha * acc + jnp.dot(prob.astype(BF16), v_ref[cols, hc], preferred_element_type=F32)
                m = m_new
            o_ref[rows, hc] = (acc / l).astype(o_ref.dtype)


def fox_attention(kvq, dcol, drow, cast=(), *, tq, group):
    bsz, seq, d3 = kvq.shape
    d = d3 // 3
    gw = group * HEAD_DIM
    ng = d // gw
    col = lambda part: pl.BlockSpec((None, seq, gw), lambda b, g: (b, 0, part * ng + g))
    c_in, c_out, c_shape = _cast_plan(cast, bsz * ng, lambda b, g: b * ng + g)
    return pl.pallas_call(
        functools.partial(_fox_kernel, tq=tq, n_cast=len(cast)),
        grid=(bsz, ng),
        in_specs=[
            col(2), col(0), col(1),
            pl.BlockSpec((None, seq, dcol.shape[2]), lambda b, g: (b, 0, 0)),
            pl.BlockSpec((None, group, seq), lambda b, g: (b * ng + g, 0, 0)),
        ] + c_in,
        out_specs=[pl.BlockSpec((None, seq, gw), lambda b, g: (b, 0, g))] + c_out,
        out_shape=[jax.ShapeDtypeStruct((bsz, seq, d), BF16)] + c_shape,
        compiler_params=_params(("parallel", "parallel")),
        name="fox_attention",
    )(kvq, kvq, kvq, dcol, drow.reshape(bsz * ng, group, seq), *[w for w, _ in cast])


def kernel(x, p, mix_norm, mlp_norm, ple_norm, w_a_in, a_lb_logits, a_head_gain, w_a_out,
           kv_norm, w_kvf, b_f, w_b_q, w_b_out, w_mlp_up, w_mlp_down, w_ple_gate, w_ple_up,
           final_norm):
    bsz, seq, d = x.shape
    depth = p.shape[0]
    t = bsz * seq
    heads = d // HEAD_DIM
    assert depth == 2 and w_a_in.shape[0] == 1 and w_b_q.shape[0] == 1
    p3 = p.reshape(depth, t, p.shape[3])
    mix3 = _gain3(mix_norm)
    kv3 = kv_norm.reshape(1, 1, d)
    fin3 = final_norm.reshape(1, 1, d)
    tail = functools.partial(mlp_ple, **TILES["mlp_ple"])

    h = x.reshape(t, d)
    proj = norm_matmul(h, mix_norm, w_a_in, 0, out_dtype=F32, **TILES["in_proj"])
    mixed, w_up_b, w_dn_b = hgrn2(proj.reshape(bsz, seq, 4 * d), a_lb_logits, a_head_gain, 0,
                                  cast=[(w_mlp_up, 0), (w_mlp_down, 0)], **TILES["hgrn"])
    h, u, ss = matmul_residual(mixed.reshape(t, d), w_a_out, 0, h, mlp_norm, 0, **TILES["out_proj"])
    h = tail(u, ss, w_up_b, w_dn_b, 0, 0, h, ple_norm, p3, w_ple_gate, w_ple_up)

    w_kvf_t = w_kvf.T
    w_f_t = jnp.pad(w_kvf_t[2 * d:], ((0, LANES - heads), (0, 0)))
    b_fp = jnp.pad(b_f.astype(F32), (0, LANES - heads)).reshape(1, LANES)
    kvq, logits = kvq_projection(h, kv3, mix3, 1, w_kvf_t, w_b_q, 0, w_f_t, n_kv_cols=2 * d,
                                 q_scale=HEAD_DIM ** -0.5 * LOG2E, **TILES["kvq"])
    dcol, drow = forget_cumsum(logits.reshape(bsz, seq, LANES), b_fp, heads=heads, **TILES["dcum"])
    mixed, w_up_b, w_dn_b = fox_attention(kvq.reshape(bsz, seq, 3 * d), dcol, drow,
                                          cast=[(w_mlp_up, 1), (w_mlp_down, 1)], **TILES["attn"])
    h, u, ss = matmul_residual(mixed.reshape(t, d), w_b_out, 0, h, mlp_norm, 1, **TILES["out_proj"])
    out = tail(u, ss, w_up_b, w_dn_b, 0, 1, h, ple_norm, p3, w_ple_gate, w_ple_up, g_final=fin3)
    return out.reshape(bsz, seq, d)
```

```python
import functools

import jax
import jax.numpy as jnp
from jax import lax
from jax.experimental import pallas as pl
from jax.experimental.pallas import tpu as pltpu

F32 = jnp.float32
BF16 = jnp.bfloat16
EPS = 1e-6

LOG2E = 1.4426950408889634
HEAD_DIM = 128
LANES = 128
V7X_VMEM_LIMIT = 58 * 1024 * 1024
NORM_ROWS = 256

TILES = dict(
    in_proj=dict(tm=1024, tn=1024),
    kvq=dict(tm=1024, tn=512),
    out_proj=dict(tm=2048, tn=512),
    mlp_ple=dict(tm=1024, tf=1024, tn=512, cn=512),
    hgrn=dict(chunk=64, unroll=32, group=2),
    dcum=dict(blk=256),
    attn=dict(tq=512, group=2),
)


def _params(sem):
    return pltpu.CompilerParams(dimension_semantics=sem, vmem_limit_bytes=V7X_VMEM_LIMIT)


def _once(block_shape, index_map):
    return pl.BlockSpec(block_shape, index_map, pipeline_mode=pl.Buffered(1))


def _sigmoid(x):
    return 1.0 / (1.0 + jnp.exp(-x))


def _norm_rows(h_ref, gains, out_refs):
    rows = h_ref.shape[0]
    rc = min(NORM_ROWS, rows)

    def body(r, carry):
        r0 = pl.multiple_of(r * rc, rc)
        hb = h_ref[pl.ds(r0, rc), :]
        y = hb * lax.rsqrt(jnp.mean(hb * hb, axis=-1, keepdims=True) + EPS)
        for g, o_ref in zip(gains, out_refs):
            o_ref[pl.ds(r0, rc), :] = (y * g).astype(o_ref.dtype)
        return carry

    lax.fori_loop(0, rows // rc, body, 0)


def _dot(a, w):
    return jnp.dot(a, w.astype(BF16), preferred_element_type=F32)


def _gain3(g):
    return g.reshape(g.shape[0], 1, g.shape[1])


def _gain_spec(layer, d):
    return pl.BlockSpec((None, 1, d), lambda i, j: (layer, 0, 0))


def _snake(i, j, n):
    return jnp.where(i % 2 == 0, j, n - 1 - j)


def _cast_plan(weights, n_steps, step_of):
    in_specs, out_specs, out_shapes = [], [], []
    for w, layer in weights:
        _, r, c = w.shape
        assert r % n_steps == 0
        rb = r // n_steps
        in_specs.append(pl.BlockSpec((None, rb, c), lambda *idx, l=layer: (l, step_of(*idx), 0)))
        out_specs.append(pl.BlockSpec((None, rb, c), lambda *idx: (0, step_of(*idx), 0)))
        out_shapes.append(jax.ShapeDtypeStruct((1, r, c), BF16))
    return in_specs, out_specs, out_shapes


def _cast_slabs(src_refs, dst_refs):
    for src_ref, dst_ref in zip(src_refs, dst_refs):
        dst_ref[...] = src_ref[...].astype(dst_ref.dtype)


def _norm_mm_kernel(x_ref, g_ref, w_ref, o_ref, xn_ref):
    @pl.when(pl.program_id(1) == 0)
    def _():
        _norm_rows(x_ref, [g_ref[...]], [xn_ref])

    o_ref[...] = _dot(xn_ref[...], w_ref[...]).astype(o_ref.dtype)


def norm_matmul(x, gains, w, layer, *, tm, tn, out_dtype):
    m, k = x.shape
    n = w.shape[2]
    return pl.pallas_call(
        _norm_mm_kernel,
        grid=(m // tm, n // tn),
        in_specs=[
            pl.BlockSpec((tm, k), lambda i, j: (i, 0)),
            _gain_spec(layer, k),
            pl.BlockSpec((None, k, tn), lambda i, j: (layer, 0, _snake(i, j, n // tn))),
        ],
        out_specs=pl.BlockSpec((tm, tn), lambda i, j: (i, _snake(i, j, n // tn))),
        out_shape=jax.ShapeDtypeStruct((m, n), out_dtype),
        scratch_shapes=[pltpu.VMEM((tm, k), BF16)],
        compiler_params=_params(("parallel", "arbitrary")),
        name="norm_matmul",
    )(x, _gain3(gains), w)


def _kvq_kernel(h_ref, gkv_ref, gq_ref, wkv_ref, wq_ref, wf_ref, o_ref, lg_ref, u_ref, *, n_kv, q_scale):
    col = _snake(pl.program_id(0), pl.program_id(1), pl.num_programs(1))
    nt = (((1,), (1,)), ((), ()))

    @pl.when(pl.program_id(1) == 0)
    def _():
        _norm_rows(h_ref, [gkv_ref[...], gq_ref[...]], [u_ref.at[0], u_ref.at[1]])
        lg_ref[...] = lax.dot_general(u_ref[0], wf_ref[...].astype(BF16), nt, preferred_element_type=F32)

    @pl.when(col < n_kv)
    def _():
        acc = lax.dot_general(u_ref[0], wkv_ref[...].astype(BF16), nt, preferred_element_type=F32)
        o_ref[...] = acc.astype(o_ref.dtype)

    @pl.when(col >= n_kv)
    def _():
        o_ref[...] = (_dot(u_ref[1], wq_ref[...]) * q_scale).astype(o_ref.dtype)


def kvq_projection(h, g_kv, g_q, q_layer, w_kvf_t, w_q, wq_layer, w_f_t, *, n_kv_cols, q_scale, tm, tn):
    m, d = h.shape
    nq_cols = w_q.shape[2]
    n_kv, n_q = n_kv_cols // tn, nq_cols // tn
    col = lambda i, j: _snake(i, j, n_kv + n_q)
    lanes = w_f_t.shape[0]
    return pl.pallas_call(
        functools.partial(_kvq_kernel, n_kv=n_kv, q_scale=q_scale),
        grid=(m // tm, n_kv + n_q),
        in_specs=[
            pl.BlockSpec((tm, d), lambda i, j: (i, 0)),
            _gain_spec(0, d),
            _gain_spec(q_layer, d),
            pl.BlockSpec((tn, d), lambda i, j: (jnp.minimum(col(i, j), n_kv - 1), 0)),
            pl.BlockSpec((None, d, tn), lambda i, j: (wq_layer, 0, jnp.clip(col(i, j) - n_kv, 0, n_q - 1))),
            pl.BlockSpec((lanes, d), lambda i, j: (0, 0)),
        ],
        out_specs=[
            pl.BlockSpec((tm, tn), lambda i, j: (i, col(i, j))),
            pl.BlockSpec((tm, lanes), lambda i, j: (i, 0)),
        ],
        out_shape=[
            jax.ShapeDtypeStruct((m, n_kv_cols + nq_cols), BF16),
            jax.ShapeDtypeStruct((m, lanes), F32),
        ],
        scratch_shapes=[pltpu.VMEM((2, tm, d), BF16)],
        compiler_params=_params(("parallel", "arbitrary")),
        name="kvq_projection",
    )(h, g_kv, g_q, w_kvf_t, w_q, w_f_t)


def _mm_res_kernel(a_ref, w_ref, x_ref, g_ref, h_ref, hg_ref, ss_ref):
    h = x_ref[...] + _dot(a_ref[...], w_ref[...])
    h_ref[...] = h
    hg_ref[...] = (h * g_ref[...]).astype(hg_ref.dtype)
    sq = h * h
    part = sq[:, :LANES]
    for n in range(1, h.shape[1] // LANES):
        part = part + sq[:, n * LANES:(n + 1) * LANES]
    ss_ref[...] = part


def matmul_residual(a, w, layer, x, gains, g_layer, *, tm, tn):
    m, k = a.shape
    n = w.shape[2]
    col = lambda i, j: _snake(i, j, n // tn)
    return pl.pallas_call(
        _mm_res_kernel,
        grid=(m // tm, n // tn),
        in_specs=[
            pl.BlockSpec((tm, k), lambda i, j: (i, 0)),
            pl.BlockSpec((None, k, tn), lambda i, j: (layer, 0, col(i, j))),
            pl.BlockSpec((tm, tn), lambda i, j: (i, col(i, j))),
            pl.BlockSpec((None, 1, tn), lambda i, j: (g_layer, 0, col(i, j))),
        ],
        out_specs=[
            pl.BlockSpec((tm, tn), lambda i, j: (i, col(i, j))),
            pl.BlockSpec((tm, tn), lambda i, j: (i, col(i, j))),
            pl.BlockSpec((tm, LANES), lambda i, j: (i, col(i, j))),
        ],
        out_shape=[
            jax.ShapeDtypeStruct((m, n), F32),
            jax.ShapeDtypeStruct((m, n), BF16),
            jax.ShapeDtypeStruct((m, (n // tn) * LANES), F32),
        ],
        compiler_params=_params(("parallel", "arbitrary")),
        name="matmul_residual",
    )(a, w, x, _gain3(gains))


def _mlp_ple_kernel(u_ref, ss_ref, wup_ref, wdn_ref, h_ref, gp_ref, p_ref, wg_ref, wu_ref, *rest,
                    nf, cn, tn, final):
    if final:
        gf_ref, o_ref, u2_ref, r_ref = rest
    else:
        o_ref, u2_ref, r_ref = rest
    s = pl.program_id(1)
    d = o_ref.shape[1]
    hw = h_ref.shape[1]

    def mlp_slab(first):
        hid = jnp.maximum(_dot(u_ref[...], wup_ref[...]) * r_ref[...], 0.0)
        hid = (hid * hid).astype(BF16)
        for n in range(d // cn):
            cs = slice(n * cn, (n + 1) * cn)
            part = _dot(hid, wdn_ref[:, cs])
            o_ref[:, cs] = part if first else o_ref[:, cs] + part

    @pl.when(s == 0)
    def _():
        r_ref[...] = lax.rsqrt(jnp.sum(ss_ref[...], axis=1, keepdims=True) * (1.0 / d) + EPS)
        mlp_slab(first=True)

    @pl.when(jnp.logical_and(s > 0, s < nf))
    def _():
        mlp_slab(first=False)

    @pl.when(s < d // hw)
    def _():
        c0 = pl.multiple_of(s * hw, hw)
        o_ref[:, pl.ds(c0, hw)] += h_ref[...]

    @pl.when(s == nf - 1)
    def _():
        _norm_rows(o_ref, [gp_ref[...]], [u2_ref])

    @pl.when(s >= nf)
    def _():
        c0 = pl.multiple_of(_snake(pl.program_id(0), s - nf, d // tn) * tn, tn)
        p_b = p_ref[...].astype(BF16)
        half = tn // 2
        for n in range(2):
            cs = slice(n * half, (n + 1) * half)
            gate = _sigmoid(_dot(u2_ref[...], wg_ref[:, cs]))
            up = _dot(p_b, wu_ref[:, cs])
            o_ref[:, pl.ds(c0 + n * half, half)] += up * gate

    if final:
        @pl.when(s == pl.num_programs(1) - 1)
        def _():
            _norm_rows(o_ref, [gf_ref[...]], [o_ref])


def mlp_ple(u, ss, w_up, w_down, w_layer, layer, h, g_ple, p, w_gate, w_ple_up, g_final=None,
            *, tm, tf, tn, cn):
    m, d = u.shape
    dff = w_up.shape[2]
    ple_dim = p.shape[2]
    nf, n_p = dff // tf, d // tn
    hw = max(LANES, d // nf)
    n_pieces = d // hw
    assert n_pieces <= nf
    final = g_final is not None
    slab = lambda i, s: _snake(i, jnp.minimum(s, nf - 1), nf)
    ple_col = lambda i, s: (layer, 0, _snake(i, jnp.clip(s - nf, 0, n_p - 1), n_p))
    in_specs = [
        pl.BlockSpec((tm, d), lambda i, s: (i, 0)),
        _once((tm, ss.shape[1]), lambda i, s: (i, 0)),
        pl.BlockSpec((None, d, tf), lambda i, s: (w_layer, 0, slab(i, s))),
        pl.BlockSpec((None, tf, d), lambda i, s: (w_layer, slab(i, s), 0)),
        pl.BlockSpec((tm, hw), lambda i, s: (i, jnp.minimum(s, n_pieces - 1))),
        _gain_spec(layer, d),
        _once((None, tm, ple_dim), lambda i, s: (layer, i, 0)),
        pl.BlockSpec((None, d, tn), ple_col),
        pl.BlockSpec((None, ple_dim, tn), ple_col),
    ]
    args = [u, ss, w_up, w_down, h, _gain3(g_ple), p, w_gate, w_ple_up]
    if final:
        in_specs.append(_gain_spec(0, d))
        args.append(g_final)
    return pl.pallas_call(
        functools.partial(_mlp_ple_kernel, nf=nf, cn=cn, tn=tn, final=final),
        grid=(m // tm, nf + n_p),
        in_specs=in_specs,
        out_specs=_once((tm, d), lambda i, s: (i, 0)),
        out_shape=jax.ShapeDtypeStruct((m, d), F32),
        scratch_shapes=[pltpu.VMEM((tm, d), BF16), pltpu.VMEM((tm, 1), F32)],
        compiler_params=_params(("parallel", "arbitrary")),
        name="mlp_ple",
    )(*args)


def _tri_cumsum(tri, x, terms=3):
    total = None
    rest = x
    for n in range(terms):
        piece = rest.astype(BF16)
        if n + 1 < terms:
            rest = rest - piece.astype(F32)
        part = jnp.dot(tri, piece, preferred_element_type=F32)
        total = part if total is None else total + part
    return total


def _hgrn_block(qv, fv, iv, gv, lb, head_gain, st, causal, tri, *, chunk):
    mid = chunk // 2
    nt = (((1,), (1,)), ((), ()))
    tn = (((0,), (0,)), ((), ()))
    fg = lb + (1.0 - lb) * _sigmoid(fv)
    k_all = 1.0 - fg
    log2f = jnp.log(fg) * LOG2E
    qc = qv * _sigmoid(qv) * (HEAD_DIM ** -0.5)
    v_all = iv.astype(BF16)
    parts = []
    for u in range(qv.shape[0] // chunk):
        sl = slice(u * chunk, (u + 1) * chunk)
        b = _tri_cumsum(tri, log2f[sl], terms=3)
        b_mid = b[mid - 1:mid, :]
        b_last = b[chunk - 1:chunk, :]
        q_mid = qc[sl] * jnp.exp2(b - b_mid)
        k_mid = k_all[sl] * jnp.exp2(b_mid - b)
        q_in = (q_mid * jnp.exp2(b_mid)).astype(BF16)
        k_end = (k_mid * jnp.exp2(b_last - b_mid)).astype(BF16)
        att = lax.dot_general(q_mid.astype(BF16), k_mid.astype(BF16), nt, preferred_element_type=F32)
        att = jnp.where(causal, att, 0.0).astype(BF16)
        o_intra = jnp.dot(att, v_all[sl], preferred_element_type=F32)
        kv = lax.dot_general(v_all[sl], k_end, tn, preferred_element_type=F32)
        parts.append((q_in, o_intra, kv, jnp.exp2(b_last)))
    outs = []
    for q_in, o_intra, kv, dec in parts:
        outs.append(o_intra + lax.dot_general(q_in, st.astype(BF16), nt, preferred_element_type=F32))
        st = dec * st + kv
    o = jnp.concatenate(outs, axis=0)
    o = o * lax.rsqrt(jnp.mean(o * o, axis=-1, keepdims=True) + EPS) * head_gain
    return o * (gv * _sigmoid(gv)), st


def _hgrn_kernel(q_ref, f_ref, i_ref, g_ref, lbl_ref, hg_ref, *rest, chunk, unroll, layer, n_cast):
    o_ref, st_ref = rest[n_cast], rest[-1]
    _cast_slabs(rest[:n_cast], rest[n_cast + 1:-1])
    seq = q_ref.shape[0]
    blk = chunk * unroll
    n_heads = o_ref.shape[1] // HEAD_DIM
    lg = lbl_ref[...]
    e = jnp.exp(lg - jnp.max(lg, axis=0, keepdims=True))
    sm = e / jnp.sum(e, axis=0, keepdims=True)
    lb = jnp.sum(sm[:layer + 1], axis=0, keepdims=True)
    head_gain = hg_ref[...]
    row = lax.broadcasted_iota(jnp.int32, (chunk, chunk), 0)
    col = lax.broadcasted_iota(jnp.int32, (chunk, chunk), 1)
    causal = row >= col
    tri = jnp.where(causal, 1.0, 0.0).astype(BF16)

    st_ref[...] = jnp.zeros_like(st_ref)

    def body(c, carry):
        rows = pl.ds(pl.multiple_of(c * blk, blk), blk)
        for hd in range(n_heads):
            cols = slice(hd * HEAD_DIM, (hd + 1) * HEAD_DIM)
            o, st = _hgrn_block(q_ref[rows, cols], f_ref[rows, cols], i_ref[rows, cols], g_ref[rows, cols],
                                lb[:, cols], head_gain, st_ref[hd], causal, tri, chunk=chunk)
            st_ref[hd] = st
            o_ref[rows, cols] = o.astype(o_ref.dtype)
        return carry

    lax.fori_loop(0, seq // blk, body, 0)


def hgrn2(proj, lb_logits, head_gains, layer, cast=(), *, chunk, unroll, group):
    bsz, seq, d4 = proj.shape
    d = d4 // 4
    gw = group * HEAD_DIM
    ng = d // gw
    nl = lb_logits.shape[0]
    col = lambda part: pl.BlockSpec((None, seq, gw), lambda b, g: (b, 0, part * ng + g))
    c_in, c_out, c_shape = _cast_plan(cast, bsz * ng, lambda b, g: b * ng + g)
    outs = pl.pallas_call(
        functools.partial(_hgrn_kernel, chunk=chunk, unroll=unroll, layer=layer, n_cast=len(cast)),
        grid=(bsz, ng),
        in_specs=[
            col(0), col(1), col(2), col(3),
            pl.BlockSpec((nl, gw), lambda b, g: (0, g)),
            pl.BlockSpec((None, 1, HEAD_DIM), lambda b, g: (layer, 0, 0)),
        ] + c_in,
        out_specs=[pl.BlockSpec((None, seq, gw), lambda b, g: (b, 0, g))] + c_out,
        out_shape=[jax.ShapeDtypeStruct((bsz, seq, d), BF16)] + c_shape,
        scratch_shapes=[pltpu.VMEM((group, HEAD_DIM, HEAD_DIM), F32)],
        compiler_params=_params(("parallel", "parallel")),
        name="hgrn2",
    )(proj, proj, proj, proj, lb_logits, _gain3(head_gains), *[w for w, _ in cast])
    return outs


def _dcum_kernel(lg_ref, bf_ref, dcol_ref, drow_ref, *, blk, heads):
    seq = lg_ref.shape[0]
    row = lax.broadcasted_iota(jnp.int32, (blk, blk), 0)
    col = lax.broadcasted_iota(jnp.int32, (blk, blk), 1)
    tri = jnp.where(row >= col, 1.0, 0.0).astype(BF16)
    carry = jnp.zeros((1, lg_ref.shape[1]), F32)
    for n in range(seq // blk):
        rows = slice(n * blk, (n + 1) * blk)
        logit = lg_ref[rows, :] + bf_ref[...]
        logsig = jnp.minimum(logit, 0.0) - jnp.log(1.0 + jnp.exp(-jnp.abs(logit)))
        c = _tri_cumsum(tri, logsig) + carry
        dcol_ref[rows, :] = c
        carry = c[blk - 1:blk, :]
    drow_ref[...] = dcol_ref[...].T[:heads, :]


def forget_cumsum(logits, b_f, *, heads, blk):
    bsz, seq, lanes = logits.shape
    return pl.pallas_call(
        functools.partial(_dcum_kernel, blk=blk, heads=heads),
        grid=(bsz,),
        in_specs=[
            pl.BlockSpec((None, seq, lanes), lambda b: (b, 0, 0)),
            pl.BlockSpec((1, lanes), lambda b: (0, 0)),
        ],
        out_specs=[
            pl.BlockSpec((None, seq, lanes), lambda b: (b, 0, 0)),
            pl.BlockSpec((None, heads, seq), lambda b: (b, 0, 0)),
        ],
        out_shape=[
            jax.ShapeDtypeStruct((bsz, seq, lanes), F32),
            jax.ShapeDtypeStruct((bsz, heads, seq), F32),
        ],
        compiler_params=_params(("parallel",)),
        name="forget_cumsum",
    )(logits, b_f)


def _fox_kernel(q_ref, k_ref, v_ref, dcol_ref, drow_ref, *rest, tq, n_cast):
    o_ref = rest[n_cast]
    _cast_slabs(rest[:n_cast], rest[n_cast + 1:])
    seq = q_ref.shape[0]
    n_heads = o_ref.shape[1] // HEAD_DIM
    lane = lax.broadcasted_iota(jnp.int32, (tq, dcol_ref.shape[1]), 1)
    row = lax.broadcasted_iota(jnp.int32, (tq, tq), 0)
    col = lax.broadcasted_iota(jnp.int32, (tq, tq), 1)
    causal = row >= col
    nt = (((1,), (1,)), ((), ()))
    for hd in range(n_heads):
        head = pl.program_id(1) * n_heads + hd
        hc = slice(hd * HEAD_DIM, (hd + 1) * HEAD_DIM)
        for qi in range(seq // tq):
            rows = slice(qi * tq, (qi + 1) * tq)
            q = q_ref[rows, hc]
            dq = jnp.sum(jnp.where(lane == head, dcol_ref[rows, :], 0.0), axis=1, keepdims=True) * LOG2E
            m = jnp.full((tq, 1), -jnp.inf, F32)
            l = jnp.zeros((tq, 1), F32)
            acc = jnp.zeros((tq, HEAD_DIM), F32)
            for ki in range(qi + 1):
                cols = slice(ki * tq, (ki + 1) * tq)
                t = lax.dot_general(q, k_ref[cols, hc], nt, preferred_element_type=F32)
                t = t - drow_ref[hd:hd + 1, cols] * LOG2E
                if ki == qi:
                    t = jnp.where(causal, t, -jnp.inf)
                m_new = jnp.maximum(m, jnp.max(t, axis=1, keepdims=True) + dq)
                alpha = jnp.exp2(m - m_new)
                prob = jnp.exp2(t + (dq - m_new))
                l = alpha * l + jnp.sum(prob, axis=1, keepdims=True)
                acc = alpha * acc + jnp.dot(prob.astype(BF16), v_ref[cols, hc], preferred_element_type=F32)
                m = m_new
            o_ref[rows, hc] = (acc / l).astype(o_ref.dtype)


def fox_attention(kvq, dcol, drow, cast=(), *, tq, group):
    bsz, seq, d3 = kvq.shape
    d = d3 // 3
    gw = group * HEAD_DIM
    ng = d // gw
    col = lambda part: pl.BlockSpec((None, seq, gw), lambda b, g: (b, 0, part * ng + g))
    c_in, c_out, c_shape = _cast_plan(cast, bsz * ng, lambda b, g: b * ng + g)
    return pl.pallas_call(
        functools.partial(_fox_kernel, tq=tq, n_cast=len(cast)),
        grid=(bsz, ng),
        in_specs=[
            col(2), col(0), col(1),
            pl.BlockSpec((None, seq, dcol.shape[2]), lambda b, g: (b, 0, 0)),
            pl.BlockSpec((None, group, seq), lambda b, g: (b * ng + g, 0, 0)),
        ] + c_in,
        out_specs=[pl.BlockSpec((None, seq, gw), lambda b, g: (b, 0, g))] + c_out,
        out_shape=[jax.ShapeDtypeStruct((bsz, seq, d), BF16)] + c_shape,
        compiler_params=_params(("parallel", "parallel")),
        name="fox_attention",
    )(kvq, kvq, kvq, dcol, drow.reshape(bsz * ng, group, seq), *[w for w, _ in cast])


def kernel(x, p, mix_norm, mlp_norm, ple_norm, w_a_in, a_lb_logits, a_head_gain, w_a_out,
           kv_norm, w_kvf, b_f, w_b_q, w_b_out, w_mlp_up, w_mlp_down, w_ple_gate, w_ple_up,
           final_norm):
    bsz, seq, d = x.shape
    depth = p.shape[0]
    t = bsz * seq
    heads = d // HEAD_DIM
    assert depth == 2 and w_a_in.shape[0] == 1 and w_b_q.shape[0] == 1
    p3 = p.reshape(depth, t, p.shape[3])
    mix3 = _gain3(mix_norm)
    kv3 = kv_norm.reshape(1, 1, d)
    fin3 = final_norm.reshape(1, 1, d)
    tail = functools.partial(mlp_ple, **TILES["mlp_ple"])

    h = x.reshape(t, d)
    proj = norm_matmul(h, mix_norm, w_a_in, 0, out_dtype=F32, **TILES["in_proj"])
    mixed, w_up_b, w_dn_b = hgrn2(proj.reshape(bsz, seq, 4 * d), a_lb_logits, a_head_gain, 0,
                                  cast=[(w_mlp_up, 0), (w_mlp_down, 0)], **TILES["hgrn"])
    h, u, ss = matmul_residual(mixed.reshape(t, d), w_a_out, 0, h, mlp_norm, 0, **TILES["out_proj"])
    h = tail(u, ss, w_up_b, w_dn_b, 0, 0, h, ple_norm, p3, w_ple_gate, w_ple_up)

    w_kvf_t = w_kvf.T
    w_f_t = jnp.pad(w_kvf_t[2 * d:], ((0, LANES - heads), (0, 0)))
    b_fp = jnp.pad(b_f.astype(F32), (0, LANES - heads)).reshape(1, LANES)
    kvq, logits = kvq_projection(h, kv3, mix3, 1, w_kvf_t, w_b_q, 0, w_f_t, n_kv_cols=2 * d,
                                 q_scale=HEAD_DIM ** -0.5 * LOG2E, **TILES["kvq"])
    dcol, drow = forget_cumsum(logits.reshape(bsz, seq, LANES), b_fp, heads=heads, **TILES["dcum"])
    mixed, w_up_b, w_dn_b = fox_attention(kvq.reshape(bsz, seq, 3 * d), dcol, drow,
                                          cast=[(w_mlp_up, 1), (w_mlp_down, 1)], **TILES["attn"])
    h, u, ss = matmul_residual(mixed.reshape(t, d), w_b_out, 0, h, mlp_norm, 1, **TILES["out_proj"])
    out = tail(u, ss, w_up_b, w_dn_b, 0, 1, h, ple_norm, p3, w_ple_gate, w_ple_up, g_final=fin3)
    return out.reshape(bsz, seq, d)
```

```python
import functools

import jax
import jax.numpy as jnp
from jax import lax
from jax.experimental import pallas as pl
from jax.experimental.pallas import tpu as pltpu

F32 = jnp.float32
BF16 = jnp.bfloat16
EPS = 1e-6

LOG2E = 1.4426950408889634
HEAD_DIM = 128
LANES = 128
V7X_VMEM_LIMIT = 56 * 1024 * 1024
NORM_ROWS = 256

TILES = dict(
    in_proj=dict(tm=1024, tn=1024),
    kvq=dict(tm=1024, tn=512),
    out_proj=dict(tm=2048, tn=512),
    mlp_ple=dict(tm=1024, tf=1024, tn=512, cn=512),
    hgrn=dict(chunk=64, unroll=32, group=2),
    dcum=dict(blk=256),
    attn=dict(tq=512, group=2),
)


def _params(sem):
    return pltpu.CompilerParams(dimension_semantics=sem, vmem_limit_bytes=V7X_VMEM_LIMIT)


def _once(block_shape, index_map):
    return pl.BlockSpec(block_shape, index_map, pipeline_mode=pl.Buffered(1))


def _sigmoid(x):
    return 1.0 / (1.0 + jnp.exp(-x))


def _norm_rows(h_ref, gains, out_refs):
    rows = h_ref.shape[0]
    rc = min(NORM_ROWS, rows)

    def body(r, carry):
        r0 = pl.multiple_of(r * rc, rc)
        hb = h_ref[pl.ds(r0, rc), :]
        y = hb * lax.rsqrt(jnp.mean(hb * hb, axis=-1, keepdims=True) + EPS)
        for g, o_ref in zip(gains, out_refs):
            o_ref[pl.ds(r0, rc), :] = (y * g).astype(o_ref.dtype)
        return carry

    lax.fori_loop(0, rows // rc, body, 0)


def _dot(a, w):
    return jnp.dot(a, w.astype(BF16), preferred_element_type=F32)


def _gain3(g):
    return g.reshape(g.shape[0], 1, g.shape[1])


def _gain_spec(layer, d):
    return pl.BlockSpec((None, 1, d), lambda i, j: (layer, 0, 0))


def _snake(i, j, n):
    return jnp.where(i % 2 == 0, j, n - 1 - j)


def _cast_plan(weights, n_steps, step_of):
    in_specs, out_specs, out_shapes = [], [], []
    for w, layer in weights:
        _, r, c = w.shape
        assert r % n_steps == 0
        rb = r // n_steps
        in_specs.append(pl.BlockSpec((None, rb, c), lambda *idx, l=layer: (l, step_of(*idx), 0)))
        out_specs.append(pl.BlockSpec((None, rb, c), lambda *idx: (0, step_of(*idx), 0)))
        out_shapes.append(jax.ShapeDtypeStruct((1, r, c), BF16))
    return in_specs, out_specs, out_shapes


def _cast_slabs(src_refs, dst_refs):
    for src_ref, dst_ref in zip(src_refs, dst_refs):
        dst_ref[...] = src_ref[...].astype(dst_ref.dtype)


def _norm_mm_kernel(x_ref, g_ref, w_ref, o_ref, xn_ref):
    @pl.when(pl.program_id(1) == 0)
    def _():
        _norm_rows(x_ref, [g_ref[...]], [xn_ref])

    o_ref[...] = _dot(xn_ref[...], w_ref[...]).astype(o_ref.dtype)


def norm_matmul(x, gains, w, layer, *, tm, tn, out_dtype):
    m, k = x.shape
    n = w.shape[2]
    return pl.pallas_call(
        _norm_mm_kernel,
        grid=(m // tm, n // tn),
        in_specs=[
            pl.BlockSpec((tm, k), lambda i, j: (i, 0)),
            _gain_spec(layer, k),
            pl.BlockSpec((None, k, tn), lambda i, j: (layer, 0, _snake(i, j, n // tn))),
        ],
        out_specs=pl.BlockSpec((tm, tn), lambda i, j: (i, _snake(i, j, n // tn))),
        out_shape=jax.ShapeDtypeStruct((m, n), out_dtype),
        scratch_shapes=[pltpu.VMEM((tm, k), BF16)],
        compiler_params=_params(("parallel", "arbitrary")),
        name="norm_matmul",
    )(x, _gain3(gains), w)


def _kvq_kernel(h_ref, gkv_ref, gq_ref, wkv_ref, wq_ref, wf_ref, o_ref, lg_ref, u_ref, *, n_kv, q_scale):
    col = _snake(pl.program_id(0), pl.program_id(1), pl.num_programs(1))
    nt = (((1,), (1,)), ((), ()))

    @pl.when(pl.program_id(1) == 0)
    def _():
        _norm_rows(h_ref, [gkv_ref[...], gq_ref[...]], [u_ref.at[0], u_ref.at[1]])
        lg_ref[...] = lax.dot_general(u_ref[0], wf_ref[...].astype(BF16), nt, preferred_element_type=F32)

    @pl.when(col < n_kv)
    def _():
        acc = lax.dot_general(u_ref[0], wkv_ref[...].astype(BF16), nt, preferred_element_type=F32)
        o_ref[...] = acc.astype(o_ref.dtype)

    @pl.when(col >= n_kv)
    def _():
        o_ref[...] = (_dot(u_ref[1], wq_ref[...]) * q_scale).astype(o_ref.dtype)


def kvq_projection(h, g_kv, g_q, q_layer, w_kvf_t, w_q, wq_layer, w_f_t, *, n_kv_cols, q_scale, tm, tn):
    m, d = h.shape
    nq_cols = w_q.shape[2]
    n_kv, n_q = n_kv_cols // tn, nq_cols // tn
    col = lambda i, j: _snake(i, j, n_kv + n_q)
    lanes = w_f_t.shape[0]
    return pl.pallas_call(
        functools.partial(_kvq_kernel, n_kv=n_kv, q_scale=q_scale),
        grid=(m // tm, n_kv + n_q),
        in_specs=[
            pl.BlockSpec((tm, d), lambda i, j: (i, 0)),
            _gain_spec(0, d),
            _gain_spec(q_layer, d),
            pl.BlockSpec((tn, d), lambda i, j: (jnp.minimum(col(i, j), n_kv - 1), 0)),
            pl.BlockSpec((None, d, tn), lambda i, j: (wq_layer, 0, jnp.clip(col(i, j) - n_kv, 0, n_q - 1))),
            pl.BlockSpec((lanes, d), lambda i, j: (0, 0)),
        ],
        out_specs=[
            pl.BlockSpec((tm, tn), lambda i, j: (i, col(i, j))),
            pl.BlockSpec((tm, lanes), lambda i, j: (i, 0)),
        ],
        out_shape=[
            jax.ShapeDtypeStruct((m, n_kv_cols + nq_cols), BF16),
            jax.ShapeDtypeStruct((m, lanes), F32),
        ],
        scratch_shapes=[pltpu.VMEM((2, tm, d), BF16)],
        compiler_params=_params(("parallel", "arbitrary")),
        name="kvq_projection",
    )(h, g_kv, g_q, w_kvf_t, w_q, w_f_t)


def _mm_res_kernel(a_ref, w_ref, x_ref, g_ref, h_ref, hg_ref, ss_ref):
    h = x_ref[...] + _dot(a_ref[...], w_ref[...])
    h_ref[...] = h
    hg_ref[...] = (h * g_ref[...]).astype(hg_ref.dtype)
    sq = h * h
    part = sq[:, :LANES]
    for n in range(1, h.shape[1] // LANES):
        part = part + sq[:, n * LANES:(n + 1) * LANES]
    ss_ref[...] = part


def matmul_residual(a, w, layer, x, gains, g_layer, *, tm, tn):
    m, k = a.shape
    n = w.shape[2]
    col = lambda i, j: _snake(i, j, n // tn)
    return pl.pallas_call(
        _mm_res_kernel,
        grid=(m // tm, n // tn),
        in_specs=[
            pl.BlockSpec((tm, k), lambda i, j: (i, 0)),
            pl.BlockSpec((None, k, tn), lambda i, j: (layer, 0, col(i, j))),
            pl.BlockSpec((tm, tn), lambda i, j: (i, col(i, j))),
            pl.BlockSpec((None, 1, tn), lambda i, j: (g_layer, 0, col(i, j))),
        ],
        out_specs=[
            pl.BlockSpec((tm, tn), lambda i, j: (i, col(i, j))),
            pl.BlockSpec((tm, tn), lambda i, j: (i, col(i, j))),
            pl.BlockSpec((tm, LANES), lambda i, j: (i, col(i, j))),
        ],
        out_shape=[
            jax.ShapeDtypeStruct((m, n), F32),
            jax.ShapeDtypeStruct((m, n), BF16),
            jax.ShapeDtypeStruct((m, (n // tn) * LANES), F32),
        ],
        compiler_params=_params(("parallel", "arbitrary")),
        name="matmul_residual",
    )(a, w, x, _gain3(gains))


def _mlp_ple_kernel(u_ref, ss_ref, wup_ref, wdn_ref, h_ref, gp_ref, p_ref, wg_ref, wu_ref, *rest,
                    nf, cn, tn, final):
    if final:
        gf_ref, o_ref, u2_ref, r_ref = rest
    else:
        o_ref, u2_ref, r_ref = rest
    s = pl.program_id(1)
    d = o_ref.shape[1]
    hw = h_ref.shape[1]

    def mlp_slab(first):
        hid = jnp.maximum(_dot(u_ref[...], wup_ref[...]) * r_ref[...], 0.0)
        hid = (hid * hid).astype(BF16)
        for n in range(d // cn):
            cs = slice(n * cn, (n + 1) * cn)
            part = _dot(hid, wdn_ref[:, cs])
            o_ref[:, cs] = part if first else o_ref[:, cs] + part

    @pl.when(s == 0)
    def _():
        r_ref[...] = lax.rsqrt(jnp.sum(ss_ref[...], axis=1, keepdims=True) * (1.0 / d) + EPS)
        mlp_slab(first=True)

    @pl.when(jnp.logical_and(s > 0, s < nf))
    def _():
        mlp_slab(first=False)

    @pl.when(s < d // hw)
    def _():
        c0 = pl.multiple_of(s * hw, hw)
        o_ref[:, pl.ds(c0, hw)] += h_ref[...]

    @pl.when(s == nf - 1)
    def _():
        _norm_rows(o_ref, [gp_ref[...]], [u2_ref])

    @pl.when(s >= nf)
    def _():
        c0 = pl.multiple_of(_snake(pl.program_id(0), s - nf, d // tn) * tn, tn)
        p_b = p_ref[...].astype(BF16)
        half = tn // 2
        for n in range(2):
            cs = slice(n * half, (n + 1) * half)
            gate = _sigmoid(_dot(u2_ref[...], wg_ref[:, cs]))
            up = _dot(p_b, wu_ref[:, cs])
            o_ref[:, pl.ds(c0 + n * half, half)] += up * gate

    if final:
        @pl.when(s == pl.num_programs(1) - 1)
        def _():
            _norm_rows(o_ref, [gf_ref[...]], [o_ref])


def mlp_ple(u, ss, w_up, w_down, w_layer, layer, h, g_ple, p, w_gate, w_ple_up, g_final=None,
            *, tm, tf, tn, cn):
    m, d = u.shape
    dff = w_up.shape[2]
    ple_dim = p.shape[2]
    nf, n_p = dff // tf, d // tn
    hw = max(LANES, d // nf)
    n_pieces = d // hw
    assert n_pieces <= nf
    final = g_final is not None
    slab = lambda i, s: _snake(i, jnp.minimum(s, nf - 1), nf)
    ple_col = lambda i, s: (layer, 0, _snake(i, jnp.clip(s - nf, 0, n_p - 1), n_p))
    in_specs = [
        _once((tm, d), lambda i, s: (i, 0)),
        pl.BlockSpec((tm, ss.shape[1]), lambda i, s: (i, 0)),
        pl.BlockSpec((None, d, tf), lambda i, s: (w_layer, 0, slab(i, s))),
        pl.BlockSpec((None, tf, d), lambda i, s: (w_layer, slab(i, s), 0)),
        pl.BlockSpec((tm, hw), lambda i, s: (i, jnp.minimum(s, n_pieces - 1))),
        _gain_spec(layer, d),
        _once((None, tm, ple_dim), lambda i, s: (layer, i, 0)),
        pl.BlockSpec((None, d, tn), ple_col),
        pl.BlockSpec((None, ple_dim, tn), ple_col),
    ]
    args = [u, ss, w_up, w_down, h, _gain3(g_ple), p, w_gate, w_ple_up]
    if final:
        in_specs.append(_gain_spec(0, d))
        args.append(g_final)
    return pl.pallas_call(
        functools.partial(_mlp_ple_kernel, nf=nf, cn=cn, tn=tn, final=final),
        grid=(m // tm, nf + n_p),
        in_specs=in_specs,
        out_specs=_once((tm, d), lambda i, s: (i, 0)),
        out_shape=jax.ShapeDtypeStruct((m, d), F32),
        scratch_shapes=[pltpu.VMEM((tm, d), BF16), pltpu.VMEM((tm, 1), F32)],
        compiler_params=_params(("parallel", "arbitrary")),
        name="mlp_ple",
    )(*args)


def _tri_cumsum(tri, x, terms=3):
    total = None
    rest = x
    for n in range(terms):
        piece = rest.astype(BF16)
        if n + 1 < terms:
            rest = rest - piece.astype(F32)
        part = jnp.dot(tri, piece, preferred_element_type=F32)
        total = part if total is None else total + part
    return total


def _hgrn_block(qv, fv, iv, gv, lb, head_gain, st, causal, tri, *, chunk):
    mid = chunk // 2
    nt = (((1,), (1,)), ((), ()))
    tn = (((0,), (0,)), ((), ()))
    fg = lb + (1.0 - lb) * _sigmoid(fv)
    k_all = 1.0 - fg
    log2f = jnp.log(fg) * LOG2E
    qc = qv * _sigmoid(qv) * (HEAD_DIM ** -0.5)
    v_all = iv.astype(BF16)
    parts = []
    for u in range(qv.shape[0] // chunk):
        sl = slice(u * chunk, (u + 1) * chunk)
        b = _tri_cumsum(tri, log2f[sl], terms=3)
        b_mid = b[mid - 1:mid, :]
        b_last = b[chunk - 1:chunk, :]
        q_mid = qc[sl] * jnp.exp2(b - b_mid)
        k_mid = k_all[sl] * jnp.exp2(b_mid - b)
        q_in = (q_mid * jnp.exp2(b_mid)).astype(BF16)
        k_end = (k_mid * jnp.exp2(b_last - b_mid)).astype(BF16)
        att = lax.dot_general(q_mid.astype(BF16), k_mid.astype(BF16), nt, preferred_element_type=F32)
        att = jnp.where(causal, att, 0.0).astype(BF16)
        o_intra = jnp.dot(att, v_all[sl], preferred_element_type=F32)
        kv = lax.dot_general(v_all[sl], k_end, tn, preferred_element_type=F32)
        parts.append((q_in, o_intra, kv, jnp.exp2(b_last)))
    outs = []
    for q_in, o_intra, kv, dec in parts:
        outs.append(o_intra + lax.dot_general(q_in, st.astype(BF16), nt, preferred_element_type=F32))
        st = dec * st + kv
    o = jnp.concatenate(outs, axis=0)
    o = o * lax.rsqrt(jnp.mean(o * o, axis=-1, keepdims=True) + EPS) * head_gain
    return o * (gv * _sigmoid(gv)), st


def _hgrn_kernel(q_ref, f_ref, i_ref, g_ref, lbl_ref, hg_ref, *rest, chunk, unroll, layer, n_cast):
    o_ref, st_ref = rest[n_cast], rest[-1]
    _cast_slabs(rest[:n_cast], rest[n_cast + 1:-1])
    seq = q_ref.shape[0]
    blk = chunk * unroll
    n_heads = o_ref.shape[1] // HEAD_DIM
    lg = lbl_ref[...]
    e = jnp.exp(lg - jnp.max(lg, axis=0, keepdims=True))
    sm = e / jnp.sum(e, axis=0, keepdims=True)
    lb = jnp.sum(sm[:layer + 1], axis=0, keepdims=True)
    head_gain = hg_ref[...]
    row = lax.broadcasted_iota(jnp.int32, (chunk, chunk), 0)
    col = lax.broadcasted_iota(jnp.int32, (chunk, chunk), 1)
    causal = row >= col
    tri = jnp.where(causal, 1.0, 0.0).astype(BF16)

    st_ref[...] = jnp.zeros_like(st_ref)

    def body(c, carry):
        rows = pl.ds(pl.multiple_of(c * blk, blk), blk)
        for hd in range(n_heads):
            cols = slice(hd * HEAD_DIM, (hd + 1) * HEAD_DIM)
            o, st = _hgrn_block(q_ref[rows, cols], f_ref[rows, cols], i_ref[rows, cols], g_ref[rows, cols],
                                lb[:, cols], head_gain, st_ref[hd], causal, tri, chunk=chunk)
            st_ref[hd] = st
            o_ref[rows, cols] = o.astype(o_ref.dtype)
        return carry

    lax.fori_loop(0, seq // blk, body, 0)


def hgrn2(proj, lb_logits, head_gains, layer, cast=(), *, chunk, unroll, group):
    bsz, seq, d4 = proj.shape
    d = d4 // 4
    gw = group * HEAD_DIM
    ng = d // gw
    nl = lb_logits.shape[0]
    col = lambda part: pl.BlockSpec((None, seq, gw), lambda b, g: (b, 0, part * ng + g))
    c_in, c_out, c_shape = _cast_plan(cast, bsz * ng, lambda b, g: b * ng + g)
    outs = pl.pallas_call(
        functools.partial(_hgrn_kernel, chunk=chunk, unroll=unroll, layer=layer, n_cast=len(cast)),
        grid=(bsz, ng),
        in_specs=[
            col(0), col(1), col(2), col(3),
            pl.BlockSpec((nl, gw), lambda b, g: (0, g)),
            pl.BlockSpec((None, 1, HEAD_DIM), lambda b, g: (layer, 0, 0)),
        ] + c_in,
        out_specs=[pl.BlockSpec((None, seq, gw), lambda b, g: (b, 0, g))] + c_out,
        out_shape=[jax.ShapeDtypeStruct((bsz, seq, d), BF16)] + c_shape,
        scratch_shapes=[pltpu.VMEM((group, HEAD_DIM, HEAD_DIM), F32)],
        compiler_params=_params(("parallel", "parallel")),
        name="hgrn2",
    )(proj, proj, proj, proj, lb_logits, _gain3(head_gains), *[w for w, _ in cast])
    return outs


def _dcum_kernel(lg_ref, bf_ref, dcol_ref, drow_ref, *, blk, heads):
    seq = lg_ref.shape[0]
    row = lax.broadcasted_iota(jnp.int32, (blk, blk), 0)
    col = lax.broadcasted_iota(jnp.int32, (blk, blk), 1)
    tri = jnp.where(row >= col, 1.0, 0.0).astype(BF16)
    carry = jnp.zeros((1, lg_ref.shape[1]), F32)
    for n in range(seq // blk):
        rows = slice(n * blk, (n + 1) * blk)
        logit = lg_ref[rows, :] + bf_ref[...]
        logsig = jnp.minimum(logit, 0.0) - jnp.log(1.0 + jnp.exp(-jnp.abs(logit)))
        c = _tri_cumsum(tri, logsig) + carry
        dcol_ref[rows, :] = c
        carry = c[blk - 1:blk, :]
    drow_ref[...] = dcol_ref[...].T[:heads, :]


def forget_cumsum(logits, b_f, *, heads, blk):
    bsz, seq, lanes = logits.shape
    return pl.pallas_call(
        functools.partial(_dcum_kernel, blk=blk, heads=heads),
        grid=(bsz,),
        in_specs=[
            pl.BlockSpec((None, seq, lanes), lambda b: (b, 0, 0)),
            pl.BlockSpec((1, lanes), lambda b: (0, 0)),
        ],
        out_specs=[
            pl.BlockSpec((None, seq, lanes), lambda b: (b, 0, 0)),
            pl.BlockSpec((None, heads, seq), lambda b: (b, 0, 0)),
        ],
        out_shape=[
            jax.ShapeDtypeStruct((bsz, seq, lanes), F32),
            jax.ShapeDtypeStruct((bsz, heads, seq), F32),
        ],
        compiler_params=_params(("parallel",)),
        name="forget_cumsum",
    )(logits, b_f)


def _fox_kernel(q_ref, k_ref, v_ref, dcol_ref, drow_ref, *rest, tq, n_cast):
    o_ref = rest[n_cast]
    _cast_slabs(rest[:n_cast], rest[n_cast + 1:])
    seq = q_ref.shape[0]
    n_heads = o_ref.shape[1] // HEAD_DIM
    lane = lax.broadcasted_iota(jnp.int32, (tq, dcol_ref.shape[1]), 1)
    row = lax.broadcasted_iota(jnp.int32, (tq, tq), 0)
    col = lax.broadcasted_iota(jnp.int32, (tq, tq), 1)
    causal = row >= col
    ones_col = jnp.where(lax.broadcasted_iota(jnp.int32, (tq, HEAD_DIM), 1) == 0, 1.0, 0.0).astype(BF16)
    nt = (((1,), (1,)), ((), ()))
    for hd in range(n_heads):
        head = pl.program_id(1) * n_heads + hd
        hc = slice(hd * HEAD_DIM, (hd + 1) * HEAD_DIM)
        for qi in range(seq // tq):
            rows = slice(qi * tq, (qi + 1) * tq)
            q = q_ref[rows, hc]
            dq = jnp.sum(jnp.where(lane == head, dcol_ref[rows, :], 0.0), axis=1, keepdims=True) * LOG2E
            m = jnp.full((tq, 1), -jnp.inf, F32)
            acc = jnp.zeros((tq, 2 * HEAD_DIM), F32)
            for ki in range(qi + 1):
                cols = slice(ki * tq, (ki + 1) * tq)
                t = lax.dot_general(q, k_ref[cols, hc], nt, preferred_element_type=F32)
                t = t - drow_ref[hd:hd + 1, cols] * LOG2E
                if ki == qi:
                    t = jnp.where(causal, t, -jnp.inf)
                m_new = jnp.maximum(m, jnp.max(t, axis=1, keepdims=True) + dq)
                alpha = jnp.exp2(m - m_new)
                prob = jnp.exp2(t + (dq - m_new))
                v_one = jnp.concatenate([v_ref[cols, hc], ones_col], axis=1)
                acc = alpha * acc + jnp.dot(prob.astype(BF16), v_one, preferred_element_type=F32)
                m = m_new
            o_ref[rows, hc] = (acc[:, :HEAD_DIM] / acc[:, HEAD_DIM:HEAD_DIM + 1]).astype(o_ref.dtype)


def fox_attention(kvq, dcol, drow, cast=(), *, tq, group):
    bsz, seq, d3 = kvq.shape
    d = d3 // 3
    gw = group * HEAD_DIM
    ng = d // gw
    col = lambda part: pl.BlockSpec((None, seq, gw), lambda b, g: (b, 0, part * ng + g))
    c_in, c_out, c_shape = _cast_plan(cast, bsz * ng, lambda b, g: b * ng + g)
    return pl.pallas_call(
        functools.partial(_fox_kernel, tq=tq, n_cast=len(cast)),
        grid=(bsz, ng),
        in_specs=[
            col(2), col(0), col(1),
            pl.BlockSpec((None, seq, dcol.shape[2]), lambda b, g: (b, 0, 0)),
            pl.BlockSpec((None, group, seq), lambda b, g: (b * ng + g, 0, 0)),
        ] + c_in,
        out_specs=[pl.BlockSpec((None, seq, gw), lambda b, g: (b, 0, g))] + c_out,
        out_shape=[jax.ShapeDtypeStruct((bsz, seq, d), BF16)] + c_shape,
        compiler_params=_params(("parallel", "parallel")),
        name="fox_attention",
    )(kvq, kvq, kvq, dcol, drow.reshape(bsz * ng, group, seq), *[w for w, _ in cast])


def kernel(x, p, mix_norm, mlp_norm, ple_norm, w_a_in, a_lb_logits, a_head_gain, w_a_out,
           kv_norm, w_kvf, b_f, w_b_q, w_b_out, w_mlp_up, w_mlp_down, w_ple_gate, w_ple_up,
           final_norm):
    bsz, seq, d = x.shape
    depth = p.shape[0]
    t = bsz * seq
    heads = d // HEAD_DIM
    assert depth == 2 and w_a_in.shape[0] == 1 and w_b_q.shape[0] == 1
    p3 = p.reshape(depth, t, p.shape[3])
    mix3 = _gain3(mix_norm)
    kv3 = kv_norm.reshape(1, 1, d)
    fin3 = final_norm.reshape(1, 1, d)
    tail = functools.partial(mlp_ple, **TILES["mlp_ple"])

    h = x.reshape(t, d)
    proj = norm_matmul(h, mix_norm, w_a_in, 0, out_dtype=F32, **TILES["in_proj"])
    mixed, w_up_b, w_dn_b = hgrn2(proj.reshape(bsz, seq, 4 * d), a_lb_logits, a_head_gain, 0,
                                  cast=[(w_mlp_up, 0), (w_mlp_down, 0)], **TILES["hgrn"])
    h, u, ss = matmul_residual(mixed.reshape(t, d), w_a_out, 0, h, mlp_norm, 0, **TILES["out_proj"])
    h = tail(u, ss, w_up_b, w_dn_b, 0, 0, h, ple_norm, p3, w_ple_gate, w_ple_up)

    w_kvf_t = w_kvf.T
    w_f_t = jnp.pad(w_kvf_t[2 * d:], ((0, LANES - heads), (0, 0)))
    b_fp = jnp.pad(b_f.astype(F32), (0, LANES - heads)).reshape(1, LANES)
    kvq, logits = kvq_projection(h, kv3, mix3, 1, w_kvf_t, w_b_q, 0, w_f_t, n_kv_cols=2 * d,
                                 q_scale=HEAD_DIM ** -0.5 * LOG2E, **TILES["kvq"])
    dcol, drow = forget_cumsum(logits.reshape(bsz, seq, LANES), b_fp, heads=heads, **TILES["dcum"])
    mixed, w_up_b, w_dn_b = fox_attention(kvq.reshape(bsz, seq, 3 * d), dcol, drow,
                                          cast=[(w_mlp_up, 1), (w_mlp_down, 1)], **TILES["attn"])
    h, u, ss = matmul_residual(mixed.reshape(t, d), w_b_out, 0, h, mlp_norm, 1, **TILES["out_proj"])
    out = tail(u, ss, w_up_b, w_dn_b, 0, 1, h, ple_norm, p3, w_ple_gate, w_ple_up, g_final=fin3)
    return out.reshape(bsz, seq, d)
```

```python
import functools

import jax
import jax.numpy as jnp
from jax import lax
from jax.experimental import pallas as pl
from jax.experimental.pallas import tpu as pltpu

F32 = jnp.float32
BF16 = jnp.bfloat16
EPS = 1e-6

LOG2E = 1.4426950408889634
HEAD_DIM = 128
LANES = 128
V7X_VMEM_LIMIT = 56 * 1024 * 1024
NORM_ROWS = 256

TILES = dict(
    in_proj=dict(tm=1024, tn=1024),
    kvq=dict(tm=1024, tn=512),
    out_proj=dict(tm=512, tn=2048),
    mlp_ple=dict(tm=1024, tf=1024, tn=512, cn=512),
    hgrn=dict(chunk=64, unroll=32, group=2),
    dcum=dict(blk=256),
    attn=dict(tq=512, group=2),
)


def _params(sem):
    return pltpu.CompilerParams(dimension_semantics=sem, vmem_limit_bytes=V7X_VMEM_LIMIT)


def _once(block_shape, index_map):
    return pl.BlockSpec(block_shape, index_map, pipeline_mode=pl.Buffered(1))


def _sigmoid(x):
    return 1.0 / (1.0 + jnp.exp(-x))


def _norm_rows(h_ref, gains, out_refs):
    rows = h_ref.shape[0]
    rc = min(NORM_ROWS, rows)

    def body(r, carry):
        r0 = pl.multiple_of(r * rc, rc)
        hb = h_ref[pl.ds(r0, rc), :]
        y = hb * lax.rsqrt(jnp.mean(hb * hb, axis=-1, keepdims=True) + EPS)
        for g, o_ref in zip(gains, out_refs):
            o_ref[pl.ds(r0, rc), :] = (y * g).astype(o_ref.dtype)
        return carry

    lax.fori_loop(0, rows // rc, body, 0)


def _dot(a, w):
    return jnp.dot(a, w.astype(BF16), preferred_element_type=F32)


def _gain3(g):
    return g.reshape(g.shape[0], 1, g.shape[1])


def _gain_spec(layer, d):
    return pl.BlockSpec((None, 1, d), lambda i, j: (layer, 0, 0))


def _snake(i, j, n):
    return jnp.where(i % 2 == 0, j, n - 1 - j)


def _cast_plan(weights, n_steps, step_of):
    in_specs, out_specs, out_shapes = [], [], []
    for w, layer in weights:
        _, r, c = w.shape
        assert r % n_steps == 0
        rb = r // n_steps
        in_specs.append(pl.BlockSpec((None, rb, c), lambda *idx, l=layer: (l, step_of(*idx), 0)))
        out_specs.append(pl.BlockSpec((None, rb, c), lambda *idx: (0, step_of(*idx), 0)))
        out_shapes.append(jax.ShapeDtypeStruct((1, r, c), BF16))
    return in_specs, out_specs, out_shapes


def _cast_slabs(src_refs, dst_refs):
    for src_ref, dst_ref in zip(src_refs, dst_refs):
        dst_ref[...] = src_ref[...].astype(dst_ref.dtype)


def _norm_mm_kernel(x_ref, g_ref, w_ref, o_ref, xn_ref):
    @pl.when(pl.program_id(1) == 0)
    def _():
        _norm_rows(x_ref, [g_ref[...]], [xn_ref])

    o_ref[...] = _dot(xn_ref[...], w_ref[...]).astype(o_ref.dtype)


def norm_matmul(x, gains, w, layer, *, tm, tn, out_dtype):
    m, k = x.shape
    n = w.shape[2]
    return pl.pallas_call(
        _norm_mm_kernel,
        grid=(m // tm, n // tn),
        in_specs=[
            pl.BlockSpec((tm, k), lambda i, j: (i, 0)),
            _gain_spec(layer, k),
            pl.BlockSpec((None, k, tn), lambda i, j: (layer, 0, _snake(i, j, n // tn))),
        ],
        out_specs=pl.BlockSpec((tm, tn), lambda i, j: (i, _snake(i, j, n // tn))),
        out_shape=jax.ShapeDtypeStruct((m, n), out_dtype),
        scratch_shapes=[pltpu.VMEM((tm, k), BF16)],
        compiler_params=_params(("parallel", "arbitrary")),
        name="norm_matmul",
    )(x, _gain3(gains), w)


def _kvq_kernel(h_ref, gkv_ref, gq_ref, wkv_ref, wq_ref, wf_ref, o_ref, lg_ref, u_ref, *, n_kv, q_scale):
    col = _snake(pl.program_id(0), pl.program_id(1), pl.num_programs(1))
    nt = (((1,), (1,)), ((), ()))

    @pl.when(pl.program_id(1) == 0)
    def _():
        _norm_rows(h_ref, [gkv_ref[...], gq_ref[...]], [u_ref.at[0], u_ref.at[1]])
        lg_ref[...] = lax.dot_general(u_ref[0], wf_ref[...].astype(BF16), nt, preferred_element_type=F32)

    @pl.when(col < n_kv)
    def _():
        acc = lax.dot_general(u_ref[0], wkv_ref[...].astype(BF16), nt, preferred_element_type=F32)
        o_ref[...] = acc.astype(o_ref.dtype)

    @pl.when(col >= n_kv)
    def _():
        o_ref[...] = (_dot(u_ref[1], wq_ref[...]) * q_scale).astype(o_ref.dtype)


def kvq_projection(h, g_kv, g_q, q_layer, w_kvf_t, w_q, wq_layer, w_f_t, *, n_kv_cols, q_scale, tm, tn):
    m, d = h.shape
    nq_cols = w_q.shape[2]
    n_kv, n_q = n_kv_cols // tn, nq_cols // tn
    col = lambda i, j: _snake(i, j, n_kv + n_q)
    lanes = w_f_t.shape[0]
    return pl.pallas_call(
        functools.partial(_kvq_kernel, n_kv=n_kv, q_scale=q_scale),
        grid=(m // tm, n_kv + n_q),
        in_specs=[
            pl.BlockSpec((tm, d), lambda i, j: (i, 0)),
            _gain_spec(0, d),
            _gain_spec(q_layer, d),
            pl.BlockSpec((tn, d), lambda i, j: (jnp.minimum(col(i, j), n_kv - 1), 0)),
            pl.BlockSpec((None, d, tn), lambda i, j: (wq_layer, 0, jnp.clip(col(i, j) - n_kv, 0, n_q - 1))),
            pl.BlockSpec((lanes, d), lambda i, j: (0, 0)),
        ],
        out_specs=[
            pl.BlockSpec((tm, tn), lambda i, j: (i, col(i, j))),
            pl.BlockSpec((tm, lanes), lambda i, j: (i, 0)),
        ],
        out_shape=[
            jax.ShapeDtypeStruct((m, n_kv_cols + nq_cols), BF16),
            jax.ShapeDtypeStruct((m, lanes), F32),
        ],
        scratch_shapes=[pltpu.VMEM((2, tm, d), BF16)],
        compiler_params=_params(("parallel", "arbitrary")),
        name="kvq_projection",
    )(h, g_kv, g_q, w_kvf_t, w_q, w_f_t)


def _mm_res_kernel(a_ref, w_ref, x_ref, g_ref, h_ref, hg_ref, ss_ref):
    h = x_ref[...] + _dot(a_ref[...], w_ref[...])
    h_ref[...] = h
    hg_ref[...] = (h * g_ref[...]).astype(hg_ref.dtype)
    sq = h * h
    part = sq[:, :LANES]
    for n in range(1, h.shape[1] // LANES):
        part = part + sq[:, n * LANES:(n + 1) * LANES]
    ss_ref[...] = part


def matmul_residual(a, w, layer, x, gains, g_layer, *, tm, tn):
    m, k = a.shape
    n = w.shape[2]
    col = lambda i, j: _snake(i, j, n // tn)
    w_spec = _once if tn == n else pl.BlockSpec
    return pl.pallas_call(
        _mm_res_kernel,
        grid=(m // tm, n // tn),
        in_specs=[
            pl.BlockSpec((tm, k), lambda i, j: (i, 0)),
            w_spec((None, k, tn), lambda i, j: (layer, 0, col(i, j))),
            pl.BlockSpec((tm, tn), lambda i, j: (i, col(i, j))),
            pl.BlockSpec((None, 1, tn), lambda i, j: (g_layer, 0, col(i, j))),
        ],
        out_specs=[
            pl.BlockSpec((tm, tn), lambda i, j: (i, col(i, j))),
            pl.BlockSpec((tm, tn), lambda i, j: (i, col(i, j))),
            pl.BlockSpec((tm, LANES), lambda i, j: (i, col(i, j))),
        ],
        out_shape=[
            jax.ShapeDtypeStruct((m, n), F32),
            jax.ShapeDtypeStruct((m, n), BF16),
            jax.ShapeDtypeStruct((m, (n // tn) * LANES), F32),
        ],
        compiler_params=_params(("parallel", "arbitrary")),
        name="matmul_residual",
    )(a, w, x, _gain3(gains))


def _mlp_ple_kernel(u_ref, ss_ref, wup_ref, wdn_ref, h_ref, gp_ref, p_ref, wg_ref, wu_ref, *rest,
                    nf, cn, tn, final):
    if final:
        gf_ref, o_ref, u2_ref, r_ref = rest
    else:
        o_ref, u2_ref, r_ref = rest
    s = pl.program_id(1)
    d = o_ref.shape[1]
    hw = h_ref.shape[1]

    def mlp_slab(first):
        hid = jnp.maximum(_dot(u_ref[...], wup_ref[...]) * r_ref[...], 0.0)
        hid = (hid * hid).astype(BF16)
        for n in range(d // cn):
            cs = slice(n * cn, (n + 1) * cn)
            part = _dot(hid, wdn_ref[:, cs])
            o_ref[:, cs] = part if first else o_ref[:, cs] + part

    @pl.when(s == 0)
    def _():
        r_ref[...] = lax.rsqrt(jnp.sum(ss_ref[...], axis=1, keepdims=True) * (1.0 / d) + EPS)
        mlp_slab(first=True)

    @pl.when(jnp.logical_and(s > 0, s < nf))
    def _():
        mlp_slab(first=False)

    @pl.when(s < d // hw)
    def _():
        c0 = pl.multiple_of(s * hw, hw)
        o_ref[:, pl.ds(c0, hw)] += h_ref[...]

    @pl.when(s == nf - 1)
    def _():
        _norm_rows(o_ref, [gp_ref[...]], [u2_ref])

    @pl.when(s >= nf)
    def _():
        c0 = pl.multiple_of(_snake(pl.program_id(0), s - nf, d // tn) * tn, tn)
        p_b = p_ref[...].astype(BF16)
        half = tn // 2
        for n in range(2):
            cs = slice(n * half, (n + 1) * half)
            gate = _sigmoid(_dot(u2_ref[...], wg_ref[:, cs]))
            up = _dot(p_b, wu_ref[:, cs])
            o_ref[:, pl.ds(c0 + n * half, half)] += up * gate

    if final:
        @pl.when(s == pl.num_programs(1) - 1)
        def _():
            _norm_rows(o_ref, [gf_ref[...]], [o_ref])


def mlp_ple(u, ss, w_up, w_down, w_layer, layer, h, g_ple, p, w_gate, w_ple_up, g_final=None,
            *, tm, tf, tn, cn):
    m, d = u.shape
    dff = w_up.shape[2]
    ple_dim = p.shape[2]
    nf, n_p = dff // tf, d // tn
    hw = max(LANES, d // nf)
    n_pieces = d // hw
    assert n_pieces <= nf
    final = g_final is not None
    slab = lambda i, s: _snake(i, jnp.minimum(s, nf - 1), nf)
    ple_col = lambda i, s: (layer, 0, _snake(i, jnp.clip(s - nf, 0, n_p - 1), n_p))
    in_specs = [
        _once((tm, d), lambda i, s: (i, 0)),
        pl.BlockSpec((tm, ss.shape[1]), lambda i, s: (i, 0)),
        pl.BlockSpec((None, d, tf), lambda i, s: (w_layer, 0, slab(i, s))),
        pl.BlockSpec((None, tf, d), lambda i, s: (w_layer, slab(i, s), 0)),
        pl.BlockSpec((tm, hw), lambda i, s: (i, jnp.minimum(s, n_pieces - 1))),
        _gain_spec(layer, d),
        _once((None, tm, ple_dim), lambda i, s: (layer, i, 0)),
        pl.BlockSpec((None, d, tn), ple_col),
        pl.BlockSpec((None, ple_dim, tn), ple_col),
    ]
    args = [u, ss, w_up, w_down, h, _gain3(g_ple), p, w_gate, w_ple_up]
    if final:
        in_specs.append(_gain_spec(0, d))
        args.append(g_final)
    return pl.pallas_call(
        functools.partial(_mlp_ple_kernel, nf=nf, cn=cn, tn=tn, final=final),
        grid=(m // tm, nf + n_p),
        in_specs=in_specs,
        out_specs=_once((tm, d), lambda i, s: (i, 0)),
        out_shape=jax.ShapeDtypeStruct((m, d), F32),
        scratch_shapes=[pltpu.VMEM((tm, d), BF16), pltpu.VMEM((tm, 1), F32)],
        compiler_params=_params(("parallel", "arbitrary")),
        name="mlp_ple",
    )(*args)


def _tri_cumsum(tri, x, terms=3):
    total = None
    rest = x
    for n in range(terms):
        piece = rest.astype(BF16)
        if n + 1 < terms:
            rest = rest - piece.astype(F32)
        part = jnp.dot(tri, piece, preferred_element_type=F32)
        total = part if total is None else total + part
    return total


def _hgrn_block(qv, fv, iv, gv, lb, head_gain, st, causal, tri, *, chunk):
    mid = chunk // 2
    nt = (((1,), (1,)), ((), ()))
    tn = (((0,), (0,)), ((), ()))
    fg = lb + (1.0 - lb) * _sigmoid(fv)
    k_all = 1.0 - fg
    log2f = jnp.log(fg) * LOG2E
    qc = qv * _sigmoid(qv) * (HEAD_DIM ** -0.5)
    v_all = iv.astype(BF16)
    parts = []
    for u in range(qv.shape[0] // chunk):
        sl = slice(u * chunk, (u + 1) * chunk)
        b = _tri_cumsum(tri, log2f[sl], terms=3)
        b_mid = b[mid - 1:mid, :]
        b_last = b[chunk - 1:chunk, :]
        q_mid = qc[sl] * jnp.exp2(b - b_mid)
        k_mid = k_all[sl] * jnp.exp2(b_mid - b)
        q_in = (q_mid * jnp.exp2(b_mid)).astype(BF16)
        k_end = (k_mid * jnp.exp2(b_last - b_mid)).astype(BF16)
        att = lax.dot_general(q_mid.astype(BF16), k_mid.astype(BF16), nt, preferred_element_type=F32)
        att = jnp.where(causal, att, 0.0).astype(BF16)
        o_intra = jnp.dot(att, v_all[sl], preferred_element_type=F32)
        kv = lax.dot_general(v_all[sl], k_end, tn, preferred_element_type=F32)
        parts.append((q_in, o_intra, kv, jnp.exp2(b_last)))
    outs = []
    for q_in, o_intra, kv, dec in parts:
        outs.append(o_intra + lax.dot_general(q_in, st.astype(BF16), nt, preferred_element_type=F32))
        st = dec * st + kv
    o = jnp.concatenate(outs, axis=0)
    o = o * lax.rsqrt(jnp.mean(o * o, axis=-1, keepdims=True) + EPS) * head_gain
    return o * (gv * _sigmoid(gv)), st


def _hgrn_kernel(q_ref, f_ref, i_ref, g_ref, lbl_ref, hg_ref, *rest, chunk, unroll, layer, n_cast):
    o_ref, st_ref = rest[n_cast], rest[-1]
    _cast_slabs(rest[:n_cast], rest[n_cast + 1:-1])
    seq = q_ref.shape[0]
    blk = chunk * unroll
    n_heads = o_ref.shape[1] // HEAD_DIM
    lg = lbl_ref[...]
    e = jnp.exp(lg - jnp.max(lg, axis=0, keepdims=True))
    sm = e / jnp.sum(e, axis=0, keepdims=True)
    lb = jnp.sum(sm[:layer + 1], axis=0, keepdims=True)
    head_gain = hg_ref[...]
    row = lax.broadcasted_iota(jnp.int32, (chunk, chunk), 0)
    col = lax.broadcasted_iota(jnp.int32, (chunk, chunk), 1)
    causal = row >= col
    tri = jnp.where(causal, 1.0, 0.0).astype(BF16)

    st_ref[...] = jnp.zeros_like(st_ref)

    def body(c, carry):
        rows = pl.ds(pl.multiple_of(c * blk, blk), blk)
        for hd in range(n_heads):
            cols = slice(hd * HEAD_DIM, (hd + 1) * HEAD_DIM)
            o, st = _hgrn_block(q_ref[rows, cols], f_ref[rows, cols], i_ref[rows, cols], g_ref[rows, cols],
                                lb[:, cols], head_gain, st_ref[hd], causal, tri, chunk=chunk)
            st_ref[hd] = st
            o_ref[rows, cols] = o.astype(o_ref.dtype)
        return carry

    lax.fori_loop(0, seq // blk, body, 0)


def hgrn2(proj, lb_logits, head_gains, layer, cast=(), *, chunk, unroll, group):
    bsz, seq, d4 = proj.shape
    d = d4 // 4
    gw = group * HEAD_DIM
    ng = d // gw
    nl = lb_logits.shape[0]
    col = lambda part: pl.BlockSpec((None, seq, gw), lambda b, g: (b, 0, part * ng + g))
    c_in, c_out, c_shape = _cast_plan(cast, bsz * ng, lambda b, g: b * ng + g)
    outs = pl.pallas_call(
        functools.partial(_hgrn_kernel, chunk=chunk, unroll=unroll, layer=layer, n_cast=len(cast)),
        grid=(bsz, ng),
        in_specs=[
            col(0), col(1), col(2), col(3),
            pl.BlockSpec((nl, gw), lambda b, g: (0, g)),
            pl.BlockSpec((None, 1, HEAD_DIM), lambda b, g: (layer, 0, 0)),
        ] + c_in,
        out_specs=[pl.BlockSpec((None, seq, gw), lambda b, g: (b, 0, g))] + c_out,
        out_shape=[jax.ShapeDtypeStruct((bsz, seq, d), BF16)] + c_shape,
        scratch_shapes=[pltpu.VMEM((group, HEAD_DIM, HEAD_DIM), F32)],
        compiler_params=_params(("parallel", "parallel")),
        name="hgrn2",
    )(proj, proj, proj, proj, lb_logits, _gain3(head_gains), *[w for w, _ in cast])
    return outs


def _dcum_kernel(lg_ref, bf_ref, dcol_ref, drow_ref, *, blk, heads):
    seq = lg_ref.shape[0]
    row = lax.broadcasted_iota(jnp.int32, (blk, blk), 0)
    col = lax.broadcasted_iota(jnp.int32, (blk, blk), 1)
    tri = jnp.where(row >= col, 1.0, 0.0).astype(BF16)
    carry = jnp.zeros((1, lg_ref.shape[1]), F32)
    for n in range(seq // blk):
        rows = slice(n * blk, (n + 1) * blk)
        logit = lg_ref[rows, :] + bf_ref[...]
        logsig = jnp.minimum(logit, 0.0) - jnp.log(1.0 + jnp.exp(-jnp.abs(logit)))
        c = _tri_cumsum(tri, logsig) + carry
        dcol_ref[rows, :] = c
        carry = c[blk - 1:blk, :]
    drow_ref[...] = dcol_ref[...].T[:heads, :]


def forget_cumsum(logits, b_f, *, heads, blk):
    bsz, seq, lanes = logits.shape
    return pl.pallas_call(
        functools.partial(_dcum_kernel, blk=blk, heads=heads),
        grid=(bsz,),
        in_specs=[
            pl.BlockSpec((None, seq, lanes), lambda b: (b, 0, 0)),
            pl.BlockSpec((1, lanes), lambda b: (0, 0)),
        ],
        out_specs=[
            pl.BlockSpec((None, seq, lanes), lambda b: (b, 0, 0)),
            pl.BlockSpec((None, heads, seq), lambda b: (b, 0, 0)),
        ],
        out_shape=[
            jax.ShapeDtypeStruct((bsz, seq, lanes), F32),
            jax.ShapeDtypeStruct((bsz, heads, seq), F32),
        ],
        compiler_params=_params(("parallel",)),
        name="forget_cumsum",
    )(logits, b_f)


def _fox_kernel(q_ref, k_ref, v_ref, dcol_ref, drow_ref, *rest, tq, n_cast):
    o_ref = rest[n_cast]
    _cast_slabs(rest[:n_cast], rest[n_cast + 1:])
    seq = q_ref.shape[0]
    n_heads = o_ref.shape[1] // HEAD_DIM
    lane = lax.broadcasted_iota(jnp.int32, (tq, dcol_ref.shape[1]), 1)
    row = lax.broadcasted_iota(jnp.int32, (tq, tq), 0)
    col = lax.broadcasted_iota(jnp.int32, (tq, tq), 1)
    causal = row >= col
    ones_col = jnp.where(lax.broadcasted_iota(jnp.int32, (tq, HEAD_DIM), 1) == 0, 1.0, 0.0).astype(BF16)
    nt = (((1,), (1,)), ((), ()))
    for hd in range(n_heads):
        head = pl.program_id(1) * n_heads + hd
        hc = slice(hd * HEAD_DIM, (hd + 1) * HEAD_DIM)
        for qi in range(seq // tq):
            rows = slice(qi * tq, (qi + 1) * tq)
            q = q_ref[rows, hc]
            dq = jnp.sum(jnp.where(lane == head, dcol_ref[rows, :], 0.0), axis=1, keepdims=True) * LOG2E
            m = jnp.full((tq, 1), -jnp.inf, F32)
            acc = jnp.zeros((tq, 2 * HEAD_DIM), F32)
            for ki in range(qi + 1):
                cols = slice(ki * tq, (ki + 1) * tq)
                t = lax.dot_general(q, k_ref[cols, hc], nt, preferred_element_type=F32)
                t = t - drow_ref[hd:hd + 1, cols] * LOG2E
                if ki == qi:
                    t = jnp.where(causal, t, -jnp.inf)
                m_new = jnp.maximum(m, jnp.max(t, axis=1, keepdims=True) + dq)
                alpha = jnp.exp2(m - m_new)
                prob = jnp.exp2(t + (dq - m_new))
                v_one = jnp.concatenate([v_ref[cols, hc], ones_col], axis=1)
                acc = alpha * acc + jnp.dot(prob.astype(BF16), v_one, preferred_element_type=F32)
                m = m_new
            o_ref[rows, hc] = (acc[:, :HEAD_DIM] / acc[:, HEAD_DIM:HEAD_DIM + 1]).astype(o_ref.dtype)


def fox_attention(kvq, dcol, drow, cast=(), *, tq, group):
    bsz, seq, d3 = kvq.shape
    d = d3 // 3
    gw = group * HEAD_DIM
    ng = d // gw
    col = lambda part: pl.BlockSpec((None, seq, gw), lambda b, g: (b, 0, part * ng + g))
    c_in, c_out, c_shape = _cast_plan(cast, bsz * ng, lambda b, g: b * ng + g)
    return pl.pallas_call(
        functools.partial(_fox_kernel, tq=tq, n_cast=len(cast)),
        grid=(bsz, ng),
        in_specs=[
            col(2), col(0), col(1),
            pl.BlockSpec((None, seq, dcol.shape[2]), lambda b, g: (b, 0, 0)),
            pl.BlockSpec((None, group, seq), lambda b, g: (b * ng + g, 0, 0)),
        ] + c_in,
        out_specs=[pl.BlockSpec((None, seq, gw), lambda b, g: (b, 0, g))] + c_out,
        out_shape=[jax.ShapeDtypeStruct((bsz, seq, d), BF16)] + c_shape,
        compiler_params=_params(("parallel", "parallel")),
        name="fox_attention",
    )(kvq, kvq, kvq, dcol, drow.reshape(bsz * ng, group, seq), *[w for w, _ in cast])


def kernel(x, p, mix_norm, mlp_norm, ple_norm, w_a_in, a_lb_logits, a_head_gain, w_a_out,
           kv_norm, w_kvf, b_f, w_b_q, w_b_out, w_mlp_up, w_mlp_down, w_ple_gate, w_ple_up,
           final_norm):
    bsz, seq, d = x.shape
    depth = p.shape[0]
    t = bsz * seq
    heads = d // HEAD_DIM
    assert depth == 2 and w_a_in.shape[0] == 1 and w_b_q.shape[0] == 1
    p3 = p.reshape(depth, t, p.shape[3])
    mix3 = _gain3(mix_norm)
    kv3 = kv_norm.reshape(1, 1, d)
    fin3 = final_norm.reshape(1, 1, d)
    tail = functools.partial(mlp_ple, **TILES["mlp_ple"])

    h = x.reshape(t, d)
    proj = norm_matmul(h, mix_norm, w_a_in, 0, out_dtype=F32, **TILES["in_proj"])
    mixed, w_up_b, w_dn_b, w_out_b = hgrn2(proj.reshape(bsz, seq, 4 * d), a_lb_logits, a_head_gain, 0,
                                           cast=[(w_mlp_up, 0), (w_mlp_down, 0), (w_a_out, 0)], **TILES["hgrn"])
    h, u, ss = matmul_residual(mixed.reshape(t, d), w_out_b, 0, h, mlp_norm, 0, **TILES["out_proj"])
    h = tail(u, ss, w_up_b, w_dn_b, 0, 0, h, ple_norm, p3, w_ple_gate, w_ple_up)

    w_kvf_t = w_kvf.T
    w_f_t = jnp.pad(w_kvf_t[2 * d:], ((0, LANES - heads), (0, 0)))
    b_fp = jnp.pad(b_f.astype(F32), (0, LANES - heads)).reshape(1, LANES)
    kvq, logits = kvq_projection(h, kv3, mix3, 1, w_kvf_t, w_b_q, 0, w_f_t, n_kv_cols=2 * d,
                                 q_scale=HEAD_DIM ** -0.5 * LOG2E, **TILES["kvq"])
    dcol, drow = forget_cumsum(logits.reshape(bsz, seq, LANES), b_fp, heads=heads, **TILES["dcum"])
    mixed, w_up_b, w_dn_b, w_out_b = fox_attention(kvq.reshape(bsz, seq, 3 * d), dcol, drow,
                                                   cast=[(w_mlp_up, 1), (w_mlp_down, 1), (w_b_out, 0)],
                                                   **TILES["attn"])
    h, u, ss = matmul_residual(mixed.reshape(t, d), w_out_b, 0, h, mlp_norm, 1, **TILES["out_proj"])
    out = tail(u, ss, w_up_b, w_dn_b, 0, 1, h, ple_norm, p3, w_ple_gate, w_ple_up, g_final=fin3)
    return out.reshape(bsz, seq, d)
```

```python
import functools

import jax
import jax.numpy as jnp
from jax import lax
from jax.experimental import pallas as pl
from jax.experimental.pallas import tpu as pltpu

F32 = jnp.float32
BF16 = jnp.bfloat16
EPS = 1e-6

LOG2E = 1.4426950408889634
HEAD_DIM = 128
LANES = 128
V7X_VMEM_LIMIT = 56 * 1024 * 1024
NORM_ROWS = 256

TILES = dict(
    in_proj=dict(tm=1024, tn=1024),
    kvq=dict(tm=1024, tn=512),
    out_proj=dict(tm=512, tn=2048),
    mlp_ple=dict(tm=1024, tf=1024, tn=512, cn=512),
    hgrn=dict(chunk=64, unroll=32, group=2),
    dcum=dict(blk=256),
    attn=dict(tq=512, group=2),
)


def _params(sem):
    return pltpu.CompilerParams(dimension_semantics=sem, vmem_limit_bytes=V7X_VMEM_LIMIT)


def _once(block_shape, index_map):
    return pl.BlockSpec(block_shape, index_map, pipeline_mode=pl.Buffered(1))


def _sigmoid(x):
    return 1.0 / (1.0 + jnp.exp(-x))


def _norm_rows(h_ref, gains, out_refs):
    rows = h_ref.shape[0]
    rc = min(NORM_ROWS, rows)

    def body(r, carry):
        r0 = pl.multiple_of(r * rc, rc)
        hb = h_ref[pl.ds(r0, rc), :]
        y = hb * lax.rsqrt(jnp.mean(hb * hb, axis=-1, keepdims=True) + EPS)
        for g, o_ref in zip(gains, out_refs):
            o_ref[pl.ds(r0, rc), :] = (y * g).astype(o_ref.dtype)
        return carry

    lax.fori_loop(0, rows // rc, body, 0)


def _dot(a, w):
    return jnp.dot(a, w.astype(BF16), preferred_element_type=F32)


def _gain3(g):
    return g.reshape(g.shape[0], 1, g.shape[1])


def _gain_spec(layer, d):
    return pl.BlockSpec((None, 1, d), lambda i, j: (layer, 0, 0))


def _snake(i, j, n):
    return jnp.where(i % 2 == 0, j, n - 1 - j)


def _cast_plan(weights, n_steps, step_of):
    in_specs, out_specs, out_shapes = [], [], []
    for w, layer in weights:
        _, r, c = w.shape
        assert r % n_steps == 0
        rb = r // n_steps
        in_specs.append(pl.BlockSpec((None, rb, c), lambda *idx, l=layer: (l, step_of(*idx), 0)))
        out_specs.append(pl.BlockSpec((None, rb, c), lambda *idx: (0, step_of(*idx), 0)))
        out_shapes.append(jax.ShapeDtypeStruct((1, r, c), BF16))
    return in_specs, out_specs, out_shapes


def _cast_slabs(src_refs, dst_refs):
    for src_ref, dst_ref in zip(src_refs, dst_refs):
        dst_ref[...] = src_ref[...].astype(dst_ref.dtype)


def _norm_mm_kernel(x_ref, g_ref, w_ref, o_ref, xn_ref):
    @pl.when(pl.program_id(1) == 0)
    def _():
        _norm_rows(x_ref, [g_ref[...]], [xn_ref])

    o_ref[...] = _dot(xn_ref[...], w_ref[...]).astype(o_ref.dtype)


def norm_matmul(x, gains, w, layer, *, tm, tn, out_dtype):
    m, k = x.shape
    n = w.shape[2]
    return pl.pallas_call(
        _norm_mm_kernel,
        grid=(m // tm, n // tn),
        in_specs=[
            pl.BlockSpec((tm, k), lambda i, j: (i, 0)),
            _gain_spec(layer, k),
            pl.BlockSpec((None, k, tn), lambda i, j: (layer, 0, _snake(i, j, n // tn))),
        ],
        out_specs=pl.BlockSpec((tm, tn), lambda i, j: (i, _snake(i, j, n // tn))),
        out_shape=jax.ShapeDtypeStruct((m, n), out_dtype),
        scratch_shapes=[pltpu.VMEM((tm, k), BF16)],
        compiler_params=_params(("parallel", "arbitrary")),
        name="norm_matmul",
    )(x, _gain3(gains), w)


def _kvq_kernel(h_ref, gkv_ref, gq_ref, wkv_ref, wq_ref, wf_ref, o_ref, lg_ref, u_ref, *, n_kv, q_scale):
    col = _snake(pl.program_id(0), pl.program_id(1), pl.num_programs(1))
    nt = (((1,), (1,)), ((), ()))

    @pl.when(pl.program_id(1) == 0)
    def _():
        _norm_rows(h_ref, [gkv_ref[...], gq_ref[...]], [u_ref.at[0], u_ref.at[1]])
        lg_ref[...] = lax.dot_general(u_ref[0], wf_ref[...].astype(BF16), nt, preferred_element_type=F32)

    @pl.when(col < n_kv)
    def _():
        acc = lax.dot_general(u_ref[0], wkv_ref[...].astype(BF16), nt, preferred_element_type=F32)
        o_ref[...] = acc.astype(o_ref.dtype)

    @pl.when(col >= n_kv)
    def _():
        o_ref[...] = (_dot(u_ref[1], wq_ref[...]) * q_scale).astype(o_ref.dtype)


def kvq_projection(h, g_kv, g_q, q_layer, w_kvf_t, w_q, wq_layer, w_f_t, *, n_kv_cols, q_scale, tm, tn):
    m, d = h.shape
    nq_cols = w_q.shape[2]
    n_kv, n_q = n_kv_cols // tn, nq_cols // tn
    col = lambda i, j: _snake(i, j, n_kv + n_q)
    lanes = w_f_t.shape[0]
    return pl.pallas_call(
        functools.partial(_kvq_kernel, n_kv=n_kv, q_scale=q_scale),
        grid=(m // tm, n_kv + n_q),
        in_specs=[
            pl.BlockSpec((tm, d), lambda i, j: (i, 0)),
            _gain_spec(0, d),
            _gain_spec(q_layer, d),
            pl.BlockSpec((tn, d), lambda i, j: (jnp.minimum(col(i, j), n_kv - 1), 0)),
            pl.BlockSpec((None, d, tn), lambda i, j: (wq_layer, 0, jnp.clip(col(i, j) - n_kv, 0, n_q - 1))),
            pl.BlockSpec((lanes, d), lambda i, j: (0, 0)),
        ],
        out_specs=[
            pl.BlockSpec((tm, tn), lambda i, j: (i, col(i, j))),
            pl.BlockSpec((tm, lanes), lambda i, j: (i, 0)),
        ],
        out_shape=[
            jax.ShapeDtypeStruct((m, n_kv_cols + nq_cols), BF16),
            jax.ShapeDtypeStruct((m, lanes), F32),
        ],
        scratch_shapes=[pltpu.VMEM((2, tm, d), BF16)],
        compiler_params=_params(("parallel", "arbitrary")),
        name="kvq_projection",
    )(h, g_kv, g_q, w_kvf_t, w_q, w_f_t)


def _mm_res_kernel(a_ref, w_ref, x_ref, g_ref, h_ref, hg_ref, ss_ref):
    h = x_ref[...] + _dot(a_ref[...], w_ref[...])
    h_ref[...] = h
    hg_ref[...] = (h * g_ref[...]).astype(hg_ref.dtype)
    sq = h * h
    part = sq[:, :LANES]
    for n in range(1, h.shape[1] // LANES):
        part = part + sq[:, n * LANES:(n + 1) * LANES]
    ss_ref[...] = part


def matmul_residual(a, w, layer, x, gains, g_layer, *, tm, tn):
    m, k = a.shape
    n = w.shape[2]
    col = lambda i, j: _snake(i, j, n // tn)
    w_spec = _once if tn == n else pl.BlockSpec
    return pl.pallas_call(
        _mm_res_kernel,
        grid=(m // tm, n // tn),
        in_specs=[
            pl.BlockSpec((tm, k), lambda i, j: (i, 0)),
            w_spec((None, k, tn), lambda i, j: (layer, 0, col(i, j))),
            pl.BlockSpec((tm, tn), lambda i, j: (i, col(i, j))),
            pl.BlockSpec((None, 1, tn), lambda i, j: (g_layer, 0, col(i, j))),
        ],
        out_specs=[
            pl.BlockSpec((tm, tn), lambda i, j: (i, col(i, j))),
            pl.BlockSpec((tm, tn), lambda i, j: (i, col(i, j))),
            pl.BlockSpec((tm, LANES), lambda i, j: (i, col(i, j))),
        ],
        out_shape=[
            jax.ShapeDtypeStruct((m, n), F32),
            jax.ShapeDtypeStruct((m, n), BF16),
            jax.ShapeDtypeStruct((m, (n // tn) * LANES), F32),
        ],
        compiler_params=_params(("parallel", "arbitrary")),
        name="matmul_residual",
    )(a, w, x, _gain3(gains))


def _mlp_ple_kernel(u_ref, ss_ref, wup_ref, wdn_ref, h_ref, gp_ref, p_ref, wg_ref, wu_ref, *rest,
                    nf, cn, tn, final):
    if final:
        gf_ref, o_ref, u2_ref, r_ref = rest
    else:
        o_ref, u2_ref, r_ref = rest
    s = pl.program_id(1)
    d = o_ref.shape[1]
    hw = h_ref.shape[1]

    def mlp_slab(first):
        hid = jnp.maximum(_dot(u_ref[...], wup_ref[...]) * r_ref[...], 0.0)
        hid = (hid * hid).astype(BF16)
        for n in range(d // cn):
            cs = slice(n * cn, (n + 1) * cn)
            part = _dot(hid, wdn_ref[:, cs])
            o_ref[:, cs] = part if first else o_ref[:, cs] + part

    @pl.when(s == 0)
    def _():
        r_ref[...] = lax.rsqrt(jnp.sum(ss_ref[...], axis=1, keepdims=True) * (1.0 / d) + EPS)
        mlp_slab(first=True)

    @pl.when(jnp.logical_and(s > 0, s < nf))
    def _():
        mlp_slab(first=False)

    @pl.when(s < d // hw)
    def _():
        c0 = pl.multiple_of(s * hw, hw)
        o_ref[:, pl.ds(c0, hw)] += h_ref[...]

    @pl.when(s == nf - 1)
    def _():
        _norm_rows(o_ref, [gp_ref[...]], [u2_ref])

    @pl.when(s >= nf)
    def _():
        c0 = pl.multiple_of(_snake(pl.program_id(0), s - nf, d // tn) * tn, tn)
        p_b = p_ref[...].astype(BF16)
        half = tn // 2
        for n in range(2):
            cs = slice(n * half, (n + 1) * half)
            gate = _sigmoid(_dot(u2_ref[...], wg_ref[:, cs]))
            up = _dot(p_b, wu_ref[:, cs])
            o_ref[:, pl.ds(c0 + n * half, half)] += up * gate

    if final:
        @pl.when(s == pl.num_programs(1) - 1)
        def _():
            _norm_rows(o_ref, [gf_ref[...]], [o_ref])


def mlp_ple(u, ss, w_up, w_down, w_layer, layer, h, g_ple, p, w_gate, w_ple_up, g_final=None,
            *, tm, tf, tn, cn):
    m, d = u.shape
    dff = w_up.shape[2]
    ple_dim = p.shape[2]
    nf, n_p = dff // tf, d // tn
    hw = max(LANES, d // nf)
    n_pieces = d // hw
    assert n_pieces <= nf
    final = g_final is not None
    slab = lambda i, s: _snake(i, jnp.minimum(s, nf - 1), nf)
    ple_col = lambda i, s: (layer, 0, _snake(i, jnp.clip(s - nf, 0, n_p - 1), n_p))
    in_specs = [
        _once((tm, d), lambda i, s: (i, 0)),
        pl.BlockSpec((tm, ss.shape[1]), lambda i, s: (i, 0)),
        pl.BlockSpec((None, d, tf), lambda i, s: (w_layer, 0, slab(i, s))),
        pl.BlockSpec((None, tf, d), lambda i, s: (w_layer, slab(i, s), 0)),
        pl.BlockSpec((tm, hw), lambda i, s: (i, jnp.minimum(s, n_pieces - 1))),
        _gain_spec(layer, d),
        _once((None, tm, ple_dim), lambda i, s: (layer, i, 0)),
        pl.BlockSpec((None, d, tn), ple_col),
        pl.BlockSpec((None, ple_dim, tn), ple_col),
    ]
    args = [u, ss, w_up, w_down, h, _gain3(g_ple), p, w_gate, w_ple_up]
    if final:
        in_specs.append(_gain_spec(0, d))
        args.append(g_final)
    return pl.pallas_call(
        functools.partial(_mlp_ple_kernel, nf=nf, cn=cn, tn=tn, final=final),
        grid=(m // tm, nf + n_p),
        in_specs=in_specs,
        out_specs=_once((tm, d), lambda i, s: (i, 0)),
        out_shape=jax.ShapeDtypeStruct((m, d), F32),
        scratch_shapes=[pltpu.VMEM((tm, d), BF16), pltpu.VMEM((tm, 1), F32)],
        compiler_params=_params(("parallel", "arbitrary")),
        name="mlp_ple",
    )(*args)


def _tri_cumsum(tri, x, terms=3):
    total = None
    rest = x
    for n in range(terms):
        piece = rest.astype(BF16)
        if n + 1 < terms:
            rest = rest - piece.astype(F32)
        part = jnp.dot(tri, piece, preferred_element_type=F32)
        total = part if total is None else total + part
    return total


def _hgrn_block(qv, fv, iv, gv, lb, head_gain, st, causal, tri, *, chunk):
    mid = chunk // 2
    nt = (((1,), (1,)), ((), ()))
    tn = (((0,), (0,)), ((), ()))
    fg = lb + (1.0 - lb) * _sigmoid(fv)
    k_all = 1.0 - fg
    log2f = jnp.log(fg) * LOG2E
    qc = qv * _sigmoid(qv) * (HEAD_DIM ** -0.5)
    v_all = iv.astype(BF16)
    parts = []
    for u in range(qv.shape[0] // chunk):
        sl = slice(u * chunk, (u + 1) * chunk)
        b = _tri_cumsum(tri, log2f[sl], terms=3)
        b_mid = b[mid - 1:mid, :]
        b_last = b[chunk - 1:chunk, :]
        q_mid = qc[sl] * jnp.exp2(b - b_mid)
        k_mid = k_all[sl] * jnp.exp2(b_mid - b)
        q_in = (q_mid * jnp.exp2(b_mid)).astype(BF16)
        k_end = (k_mid * jnp.exp2(b_last - b_mid)).astype(BF16)
        att = lax.dot_general(q_mid.astype(BF16), k_mid.astype(BF16), nt, preferred_element_type=F32)
        att = jnp.where(causal, att, 0.0).astype(BF16)
        o_intra = jnp.dot(att, v_all[sl], preferred_element_type=F32)
        kv = lax.dot_general(v_all[sl], k_end, tn, preferred_element_type=F32)
        parts.append((q_in, o_intra, kv, jnp.exp2(b_last)))
    outs = []
    for q_in, o_intra, kv, dec in parts:
        outs.append(o_intra + lax.dot_general(q_in, st.astype(BF16), nt, preferred_element_type=F32))
        st = dec * st + kv
    o = jnp.concatenate(outs, axis=0)
    o = o * lax.rsqrt(jnp.mean(o * o, axis=-1, keepdims=True) + EPS) * head_gain
    return o * (gv * _sigmoid(gv)), st


def _hgrn_kernel(q_ref, f_ref, i_ref, g_ref, lbl_ref, hg_ref, *rest, chunk, unroll, layer, n_cast):
    o_ref, st_ref = rest[n_cast], rest[-1]
    _cast_slabs(rest[:n_cast], rest[n_cast + 1:-1])
    seq = q_ref.shape[0]
    blk = chunk * unroll
    n_heads = o_ref.shape[1] // HEAD_DIM
    lg = lbl_ref[...]
    e = jnp.exp(lg - jnp.max(lg, axis=0, keepdims=True))
    sm = e / jnp.sum(e, axis=0, keepdims=True)
    lb = jnp.sum(sm[:layer + 1], axis=0, keepdims=True)
    head_gain = hg_ref[...]
    row = lax.broadcasted_iota(jnp.int32, (chunk, chunk), 0)
    col = lax.broadcasted_iota(jnp.int32, (chunk, chunk), 1)
    causal = row >= col
    tri = jnp.where(causal, 1.0, 0.0).astype(BF16)

    st_ref[...] = jnp.zeros_like(st_ref)

    def body(c, carry):
        rows = pl.ds(pl.multiple_of(c * blk, blk), blk)
        for hd in range(n_heads):
            cols = slice(hd * HEAD_DIM, (hd + 1) * HEAD_DIM)
            o, st = _hgrn_block(q_ref[rows, cols], f_ref[rows, cols], i_ref[rows, cols], g_ref[rows, cols],
                                lb[:, cols], head_gain, st_ref[hd], causal, tri, chunk=chunk)
            st_ref[hd] = st
            o_ref[rows, cols] = o.astype(o_ref.dtype)
        return carry

    lax.fori_loop(0, seq // blk, body, 0)


def hgrn2(proj, lb_logits, head_gains, layer, cast=(), *, chunk, unroll, group):
    bsz, seq, d4 = proj.shape
    d = d4 // 4
    gw = group * HEAD_DIM
    ng = d // gw
    nl = lb_logits.shape[0]
    col = lambda part: pl.BlockSpec((None, seq, gw), lambda b, g: (b, 0, part * ng + g))
    c_in, c_out, c_shape = _cast_plan(cast, bsz * ng, lambda b, g: b * ng + g)
    outs = pl.pallas_call(
        functools.partial(_hgrn_kernel, chunk=chunk, unroll=unroll, layer=layer, n_cast=len(cast)),
        grid=(bsz, ng),
        in_specs=[
            col(0), col(1), col(2), col(3),
            pl.BlockSpec((nl, gw), lambda b, g: (0, g)),
            pl.BlockSpec((None, 1, HEAD_DIM), lambda b, g: (layer, 0, 0)),
        ] + c_in,
        out_specs=[pl.BlockSpec((None, seq, gw), lambda b, g: (b, 0, g))] + c_out,
        out_shape=[jax.ShapeDtypeStruct((bsz, seq, d), BF16)] + c_shape,
        scratch_shapes=[pltpu.VMEM((group, HEAD_DIM, HEAD_DIM), F32)],
        compiler_params=_params(("parallel", "parallel")),
        name="hgrn2",
    )(proj, proj, proj, proj, lb_logits, _gain3(head_gains), *[w for w, _ in cast])
    return outs


def _dcum_kernel(lg_ref, bf_ref, dcol_ref, drow_ref, *, blk, heads):
    seq = lg_ref.shape[0]
    row = lax.broadcasted_iota(jnp.int32, (blk, blk), 0)
    col = lax.broadcasted_iota(jnp.int32, (blk, blk), 1)
    tri = jnp.where(row >= col, 1.0, 0.0).astype(BF16)
    carry = jnp.zeros((1, lg_ref.shape[1]), F32)
    for n in range(seq // blk):
        rows = slice(n * blk, (n + 1) * blk)
        logit = lg_ref[rows, :] + bf_ref[...]
        logsig = jnp.minimum(logit, 0.0) - jnp.log(1.0 + jnp.exp(-jnp.abs(logit)))
        c = _tri_cumsum(tri, logsig) + carry
        dcol_ref[rows, :] = c
        carry = c[blk - 1:blk, :]
    drow_ref[...] = dcol_ref[...].T[:heads, :]


def forget_cumsum(logits, b_f, *, heads, blk):
    bsz, seq, lanes = logits.shape
    return pl.pallas_call(
        functools.partial(_dcum_kernel, blk=blk, heads=heads),
        grid=(bsz,),
        in_specs=[
            pl.BlockSpec((None, seq, lanes), lambda b: (b, 0, 0)),
            pl.BlockSpec((1, lanes), lambda b: (0, 0)),
        ],
        out_specs=[
            pl.BlockSpec((None, seq, lanes), lambda b: (b, 0, 0)),
            pl.BlockSpec((None, heads, seq), lambda b: (b, 0, 0)),
        ],
        out_shape=[
            jax.ShapeDtypeStruct((bsz, seq, lanes), F32),
            jax.ShapeDtypeStruct((bsz, heads, seq), F32),
        ],
        compiler_params=_params(("parallel",)),
        name="forget_cumsum",
    )(logits, b_f)


def _fox_kernel(q_ref, k_ref, v_ref, dcol_ref, drow_ref, *rest, tq, n_cast):
    o_ref = rest[n_cast]
    _cast_slabs(rest[:n_cast], rest[n_cast + 1:])
    seq = q_ref.shape[0]
    n_heads = o_ref.shape[1] // HEAD_DIM
    lane = lax.broadcasted_iota(jnp.int32, (tq, dcol_ref.shape[1]), 1)
    row = lax.broadcasted_iota(jnp.int32, (tq, tq), 0)
    col = lax.broadcasted_iota(jnp.int32, (tq, tq), 1)
    causal = row >= col
    ones_col = jnp.where(lax.broadcasted_iota(jnp.int32, (tq, HEAD_DIM), 1) == 0, 1.0, 0.0).astype(BF16)
    nt = (((1,), (1,)), ((), ()))
    for hd in range(n_heads):
        head = pl.program_id(1) * n_heads + hd
        hc = slice(hd * HEAD_DIM, (hd + 1) * HEAD_DIM)
        for qi in range(seq // tq):
            rows = slice(qi * tq, (qi + 1) * tq)
            q = q_ref[rows, hc]
            dq = jnp.sum(jnp.where(lane == head, dcol_ref[rows, :], 0.0), axis=1, keepdims=True) * LOG2E
            m = jnp.full((tq, 1), -jnp.inf, F32)
            acc = jnp.zeros((tq, 2 * HEAD_DIM), F32)
            for ki in range(qi + 1):
                cols = slice(ki * tq, (ki + 1) * tq)
                t = lax.dot_general(q, k_ref[cols, hc], nt, preferred_element_type=F32)
                t = t - drow_ref[hd:hd + 1, cols] * LOG2E
                if ki == qi:
                    t = jnp.where(causal, t, -jnp.inf)
                m_new = jnp.maximum(m, jnp.max(t, axis=1, keepdims=True) + dq)
                alpha = jnp.exp2(m - m_new)
                prob = jnp.exp2(t + (dq - m_new))
                v_one = jnp.concatenate([v_ref[cols, hc], ones_col], axis=1)
                acc = alpha * acc + jnp.dot(prob.astype(BF16), v_one, preferred_element_type=F32)
                m = m_new
            o_ref[rows, hc] = (acc[:, :HEAD_DIM] / acc[:, HEAD_DIM:HEAD_DIM + 1]).astype(o_ref.dtype)


def fox_attention(kvq, dcol, drow, cast=(), *, tq, group):
    bsz, seq, d3 = kvq.shape
    d = d3 // 3
    gw = group * HEAD_DIM
    ng = d // gw
    col = lambda part: pl.BlockSpec((None, seq, gw), lambda b, g: (b, 0, part * ng + g))
    c_in, c_out, c_shape = _cast_plan(cast, bsz * ng, lambda b, g: b * ng + g)
    return pl.pallas_call(
        functools.partial(_fox_kernel, tq=tq, n_cast=len(cast)),
        grid=(bsz, ng),
        in_specs=[
            col(2), col(0), col(1),
            pl.BlockSpec((None, seq, dcol.shape[2]), lambda b, g: (b, 0, 0)),
            pl.BlockSpec((None, group, seq), lambda b, g: (b * ng + g, 0, 0)),
        ] + c_in,
        out_specs=[pl.BlockSpec((None, seq, gw), lambda b, g: (b, 0, g))] + c_out,
        out_shape=[jax.ShapeDtypeStruct((bsz, seq, d), BF16)] + c_shape,
        compiler_params=_params(("parallel", "parallel")),
        name="fox_attention",
    )(kvq, kvq, kvq, dcol, drow.reshape(bsz * ng, group, seq), *[w for w, _ in cast])


def kernel(x, p, mix_norm, mlp_norm, ple_norm, w_a_in, a_lb_logits, a_head_gain, w_a_out,
           kv_norm, w_kvf, b_f, w_b_q, w_b_out, w_mlp_up, w_mlp_down, w_ple_gate, w_ple_up,
           final_norm):
    bsz, seq, d = x.shape
    depth = p.shape[0]
    t = bsz * seq
    heads = d // HEAD_DIM
    assert depth == 2 and w_a_in.shape[0] == 1 and w_b_q.shape[0] == 1
    p3 = p.reshape(depth, t, p.shape[3])
    mix3 = _gain3(mix_norm)
    kv3 = kv_norm.reshape(1, 1, d)
    fin3 = final_norm.reshape(1, 1, d)
    tail = functools.partial(mlp_ple, **TILES["mlp_ple"])

    h = x.reshape(t, d)
    proj = norm_matmul(h, mix_norm, w_a_in, 0, out_dtype=F32, **TILES["in_proj"])
    mixed, w_up_b, w_dn_b, w_out_b, w_q_b = hgrn2(
        proj.reshape(bsz, seq, 4 * d), a_lb_logits, a_head_gain, 0,
        cast=[(w_mlp_up, 0), (w_mlp_down, 0), (w_a_out, 0), (w_b_q, 0)], **TILES["hgrn"])
    h, u, ss = matmul_residual(mixed.reshape(t, d), w_out_b, 0, h, mlp_norm, 0, **TILES["out_proj"])
    h = tail(u, ss, w_up_b, w_dn_b, 0, 0, h, ple_norm, p3, w_ple_gate, w_ple_up)

    w_kvf_t = w_kvf.T
    w_f_t = jnp.pad(w_kvf_t[2 * d:], ((0, LANES - heads), (0, 0)))
    b_fp = jnp.pad(b_f.astype(F32), (0, LANES - heads)).reshape(1, LANES)
    kvq, logits = kvq_projection(h, kv3, mix3, 1, w_kvf_t, w_q_b, 0, w_f_t, n_kv_cols=2 * d,
                                 q_scale=HEAD_DIM ** -0.5 * LOG2E, **TILES["kvq"])
    dcol, drow = forget_cumsum(logits.reshape(bsz, seq, LANES), b_fp, heads=heads, **TILES["dcum"])
    mixed, w_up_b, w_dn_b, w_out_b = fox_attention(kvq.reshape(bsz, seq, 3 * d), dcol, drow,
                                                   cast=[(w_mlp_up, 1), (w_mlp_down, 1), (w_b_out, 0)],
                                                   **TILES["attn"])
    h, u, ss = matmul_residual(mixed.reshape(t, d), w_out_b, 0, h, mlp_norm, 1, **TILES["out_proj"])
    out = tail(u, ss, w_up_b, w_dn_b, 0, 1, h, ple_norm, p3, w_ple_gate, w_ple_up, g_final=fin3)
    return out.reshape(bsz, seq, d)
```
